```python
import math
import jax, jax.numpy as jnp
from jax import lax
import numpy as np

D_MODEL = 1024
BATCH = 8
SEQ = 4096
DEPTH = 4

HEAD_DIM = 64
N_META = 16
BLOCK = 128
NEG_INF = -1e30
SWA_WINDOW = 128
SWA_Q_HEADS = 8
SWA_KV_HEADS = 2
SWA_GROUP = SWA_Q_HEADS // SWA_KV_HEADS
FOX_HEADS = 8
LRU_WIDTH = D_MODEL // 2
LRU_BLOCKS = 8
LRU_BLOCK_DIM = LRU_WIDTH // LRU_BLOCKS
CONV_WIDTH = 4
LRU_C = 8.0
REL_BUCKETS = 32
REL_MAX_DIST = 128
D_FF = ((-(-8 * D_MODEL // 3)) + 255) // 256 * 256
N_BRANCH = 3
SPLIT_SIZES = (
    SWA_Q_HEADS * HEAD_DIM,
    SWA_KV_HEADS * HEAD_DIM,
    SWA_KV_HEADS * HEAD_DIM,
    FOX_HEADS * HEAD_DIM,
    FOX_HEADS * HEAD_DIM,
    FOX_HEADS * HEAD_DIM,
    FOX_HEADS,
    LRU_WIDTH,
    LRU_WIDTH,
    N_BRANCH * D_MODEL,
)
IN_COLS = sum(SPLIT_SIZES)

kernel_name = "hybrid_swa_fox_rglru_block"


def rms_norm(x, g, eps=1e-6):
    xf = x.astype(jnp.float32)
    y = xf * lax.rsqrt(jnp.mean(xf * xf, axis=-1, keepdims=True) + eps)
    return (y * g.astype(jnp.float32)).astype(x.dtype)


def pad_left(a, n):
    return jnp.pad(a, [(0, 0), (n, 0)] + [(0, 0)] * (a.ndim - 2))


def t5_bucket(dist):
    max_exact = REL_BUCKETS // 2
    d = jnp.maximum(dist, 0)
    scaled = jnp.log(jnp.maximum(d, 1).astype(jnp.float32) / max_exact) / math.log(REL_MAX_DIST / max_exact)
    large = jnp.minimum(max_exact + (scaled * (REL_BUCKETS - max_exact)).astype(jnp.int32), REL_BUCKETS - 1)
    return jnp.where(d < max_exact, d, large)


def swa_sink_attention(q, k, v, sinks, rel_table, n_pad):
    b, tp, _, dh = q.shape
    nb = tp // BLOCK
    qb = q.reshape(b, nb, BLOCK, SWA_KV_HEADS, SWA_GROUP, dh)

    def band(a):
        a = a.reshape(b, nb, BLOCK, SWA_KV_HEADS, dh)
        prev = jnp.pad(a, ((0, 0), (1, 0), (0, 0), (0, 0), (0, 0)))[:, :-1]
        return jnp.concatenate([prev, a], axis=2)

    k_band, v_band = band(k), band(v)
    s = jnp.einsum('bnqhgd,bnkhd->bnhgqk', qb, k_band).astype(jnp.float32) * (dh ** -0.5)
    q_idx = jnp.arange(BLOCK)[:, None]
    k_idx = jnp.arange(2 * BLOCK)[None, :]
    dist = q_idx + BLOCK - k_idx
    bias = rel_table.astype(jnp.float32)[t5_bucket(dist)]
    bias = bias.transpose(2, 0, 1).reshape(SWA_KV_HEADS, SWA_GROUP, BLOCK, 2 * BLOCK)
    key_abs = (jnp.arange(nb)[:, None] - 1) * BLOCK + k_idx
    mask = ((dist >= 0) & (dist < SWA_WINDOW))[None] & (key_abs >= n_pad)[:, None, :]
    s = jnp.where(mask[None, :, None, None], s + bias, NEG_INF)
    sink = sinks.astype(jnp.float32).reshape(SWA_KV_HEADS, SWA_GROUP)[None, None, :, :, None, None]
    m = jnp.maximum(jnp.max(s, axis=-1, keepdims=True), sink)
    p = jnp.exp(s - m)
    denom = jnp.sum(p, axis=-1, keepdims=True) + jnp.exp(sink - m)
    p = (p / denom).astype(v.dtype)
    o = jnp.einsum('bnhgqk,bnkhd->bnqhgd', p, v_band)
    return o.reshape(b, tp, SWA_Q_HEADS * dh)


def forgetting_attention(q, k, v, log_f, n_pad):
    b, tp, h, dh = q.shape
    nb = tp // BLOCK
    cum = jnp.cumsum(log_f, axis=1).transpose(0, 2, 1)
    k_pos = jnp.arange(tp)
    qb = q.reshape(b, nb, BLOCK, h, dh).transpose(1, 0, 2, 3, 4)
    cb = cum.reshape(b, h, nb, BLOCK).transpose(2, 0, 1, 3)

    def one_block(args):
        qi, ci, n = args
        s = jnp.einsum('bqhd,bkhd->bhqk', qi, k).astype(jnp.float32) * (dh ** -0.5)
        s = s + ci[..., :, None] - cum[:, :, None, :]
        q_pos = n * BLOCK + jnp.arange(BLOCK)
        mask = (k_pos[None, :] <= q_pos[:, None]) & (k_pos >= n_pad)[None, :]
        s = jnp.where(mask, s, NEG_INF)
        p = jax.nn.softmax(s, axis=-1).astype(v.dtype)
        return jnp.einsum('bhqk,bkhd->bqhd', p, v)

    o = lax.map(one_block, (qb, cb, jnp.arange(nb)))
    return o.transpose(1, 0, 2, 3, 4).reshape(b, tp, h * dh)


def causal_depthwise_conv(x, w, bias):
    t = x.shape[1]
    xp = jnp.pad(x, ((0, 0), (CONV_WIDTH - 1, 0), (0, 0)))
    out = xp[:, 0:t] * w[0]
    for i in range(1, CONV_WIDTH):
        out = out + xp[:, i:i + t] * w[i]
    return out + bias


def rg_lru(x, w_r, b_r, w_i, b_i, lam):
    b, t, c = x.shape
    xb = x.reshape(b, t, LRU_BLOCKS, LRU_BLOCK_DIM)
    r = jax.nn.sigmoid(jnp.einsum('bthi,hij->bthj', xb, w_r).reshape(b, t, c).astype(jnp.float32) + b_r)
    gi = jax.nn.sigmoid(jnp.einsum('bthi,hij->bthj', xb, w_i).reshape(b, t, c).astype(jnp.float32) + b_i)
    log_a = LRU_C * r * jax.nn.log_sigmoid(lam.astype(jnp.float32))
    a = jnp.exp(log_a)
    inp = jnp.sqrt(-jnp.expm1(2.0 * log_a)) * (gi * x.astype(jnp.float32))

    def combine(left, right):
        a1, b1 = left
        a2, b2 = right
        return a1 * a2, a2 * b1 + b2

    _, h = lax.associative_scan(combine, (a, inp), axis=1)
    return h.astype(x.dtype)


def setup_inputs(seed: int = 0) -> dict:
    key = jax.random.key(seed)
    ks = jax.random.split(key, 24)
    f32 = jnp.float32
    nrm = lambda k, shape, scale: jax.random.normal(k, shape, f32) * scale
    u = jax.random.uniform(ks[10], (DEPTH, LRU_WIDTH), f32, 0.9, 0.999)
    a0 = u ** (1.0 / LRU_C)
    return {
        "x": nrm(ks[0], (BATCH, SEQ, D_MODEL), 1.0),
        "meta_tokens": nrm(ks[1], (N_META, D_MODEL), 1.0),
        "rel_bias_table": nrm(ks[2], (REL_BUCKETS, SWA_Q_HEADS), 0.5),
        "norm_mix": 1.0 + nrm(ks[3], (DEPTH, D_MODEL), 0.02),
        "w_in": nrm(ks[4], (DEPTH, D_MODEL, IN_COLS), D_MODEL ** -0.5),
        "swa_sinks": nrm(ks[5], (DEPTH, SWA_Q_HEADS), 0.5),
        "fox_forget_bias": 2.0 + 3.0 * jax.random.uniform(ks[6], (DEPTH, FOX_HEADS), f32),
        "conv_w": nrm(ks[7], (DEPTH, CONV_WIDTH, LRU_WIDTH), CONV_WIDTH ** -0.5),
        "conv_b": nrm(ks[8], (DEPTH, LRU_WIDTH), 0.02),
        "lru_w_r": nrm(ks[9], (DEPTH, LRU_BLOCKS, LRU_BLOCK_DIM, LRU_BLOCK_DIM), LRU_BLOCK_DIM ** -0.5),
        "lru_b_r": nrm(ks[11], (DEPTH, LRU_WIDTH), 0.02),
        "lru_w_i": nrm(ks[12], (DEPTH, LRU_BLOCKS, LRU_BLOCK_DIM, LRU_BLOCK_DIM), LRU_BLOCK_DIM ** -0.5),
        "lru_b_i": nrm(ks[13], (DEPTH, LRU_WIDTH), 0.02),
        "lru_lambda": jnp.log(a0) - jnp.log1p(-a0),
        "w_branch": nrm(ks[14], (DEPTH, N_BRANCH, LRU_WIDTH, D_MODEL), LRU_WIDTH ** -0.5),
        "w_out": nrm(ks[15], (DEPTH, D_MODEL, D_MODEL), D_MODEL ** -0.5),
        "norm_ffn": 1.0 + nrm(ks[16], (DEPTH, D_MODEL), 0.02),
        "w_ffn_in": nrm(ks[17], (DEPTH, D_MODEL, 2 * D_FF), D_MODEL ** -0.5),
        "w_ffn_out": nrm(ks[18], (DEPTH, D_FF, D_MODEL), D_FF ** -0.5),
        "norm_final": 1.0 + nrm(ks[19], (D_MODEL,), 0.02),
    }


def reference(x, meta_tokens, rel_bias_table, norm_mix, w_in, swa_sinks, fox_forget_bias, conv_w, conv_b,
              lru_w_r, lru_b_r, lru_w_i, lru_b_i, lru_lambda, w_branch, w_out, norm_ffn, w_ffn_in,
              w_ffn_out, norm_final):
    b = x.shape[0]
    meta = jnp.broadcast_to(meta_tokens.astype(x.dtype)[None], (b, N_META, D_MODEL))
    h = jnp.concatenate([meta, x], axis=1)
    t = h.shape[1]
    n_pad = (-t) % BLOCK
    split_points = [int(p) for p in np.cumsum(SPLIT_SIZES)[:-1]]
    for l in range(DEPTH):
        u = rms_norm(h, norm_mix[l])
        proj = u @ w_in[l]
        qa, ka, va, qf, kf, vf, fl, xc, yc, gates = jnp.split(proj, split_points, axis=-1)
        o_a = swa_sink_attention(
            pad_left(qa.reshape(b, t, SWA_Q_HEADS, HEAD_DIM), n_pad),
            pad_left(ka.reshape(b, t, SWA_KV_HEADS, HEAD_DIM), n_pad),
            pad_left(va.reshape(b, t, SWA_KV_HEADS, HEAD_DIM), n_pad),
            swa_sinks[l], rel_bias_table, n_pad)[:, n_pad:]
        log_f = jax.nn.log_sigmoid(fl.astype(jnp.float32) + fox_forget_bias[l].astype(jnp.float32))
        o_f = forgetting_attention(
            pad_left(qf.reshape(b, t, FOX_HEADS, HEAD_DIM), n_pad),
            pad_left(kf.reshape(b, t, FOX_HEADS, HEAD_DIM), n_pad),
            pad_left(vf.reshape(b, t, FOX_HEADS, HEAD_DIM), n_pad),
            pad_left(log_f, n_pad), n_pad)[:, n_pad:]
        xc = causal_depthwise_conv(xc, conv_w[l], conv_b[l])
        o_c = rg_lru(xc, lru_w_r[l], lru_b_r[l], lru_w_i[l], lru_b_i[l], lru_lambda[l]) * jax.nn.gelu(yc)
        g = jax.nn.sigmoid(gates).reshape(b, t, N_BRANCH, D_MODEL)
        merged = (g[:, :, 0] * (o_a @ w_branch[l, 0])
                  + g[:, :, 1] * (o_f @ w_branch[l, 1])
                  + g[:, :, 2] * (o_c @ w_branch[l, 2]))
        h = h + merged @ w_out[l]
        u = rms_norm(h, norm_ffn[l])
        gate_ff, up_ff = jnp.split(u @ w_ffn_in[l], 2, axis=-1)
        h = h + (jax.nn.silu(gate_ff) * up_ff) @ w_ffn_out[l]
    return rms_norm(h, norm_final)[:, N_META:]
```

```python
import functools
import math

import jax
import jax.numpy as jnp
from jax import lax
from jax.experimental import pallas as pl
from jax.experimental.pallas import tpu as pltpu

F32 = jnp.float32
BF16 = jnp.bfloat16

D_MODEL = 1024
HEAD_DIM = 64
N_META = 16
BLOCK = 128
LANES = 128
NEG_INF = -1e30
SWA_WINDOW = 128
SWA_Q_HEADS = 8
SWA_KV_HEADS = 2
FOX_HEADS = 8
LRU_WIDTH = D_MODEL // 2
LRU_BLOCKS = 8
CONV_WIDTH = 4
LRU_C = 8.0
REL_BUCKETS = 32
REL_MAX_DIST = 128
D_FF = 2816
N_BRANCH = 3
EPS = 1e-6

A_COLS = SWA_Q_HEADS * HEAD_DIM + 2 * SWA_KV_HEADS * HEAD_DIM
F_COLS = 3 * FOX_HEADS * HEAD_DIM
C_COLS = 2 * LRU_WIDTH
G_COLS = N_BRANCH * D_MODEL
OFF_A = 0
OFF_F = OFF_A + A_COLS
OFF_FL = OFF_F + F_COLS
OFF_C = OFF_FL + LANES
OFF_G = OFF_C + C_COLS
IN_COLS_PACKED = OFF_G + G_COLS

VMEM_LIMIT = 56 * 1024 * 1024


def _sigmoid(x):
    return 1.0 / (1.0 + jnp.exp(-x))


def _log_sigmoid(x):
    return jnp.minimum(x, 0.0) - jnp.log1p(jnp.exp(-jnp.abs(x)))


def _gelu_tanh(x):
    c = math.sqrt(2.0 / math.pi)
    return 0.5 * x * (1.0 + jnp.tanh(c * (x + 0.044715 * (x * x * x))))


def _rms_norm(x, g):
    ms = jnp.mean(x * x, axis=-1, keepdims=True)
    return x * lax.rsqrt(ms + EPS) * g


def _dot(a, b):
    return jnp.dot(a, b, preferred_element_type=F32)


def _dot_nt(a, b):
    return lax.dot_general(a, b, (((1,), (1,)), ((), ())), preferred_element_type=F32)


def _resident(shape):
    nd = len(shape)
    return pl.BlockSpec(shape, lambda *_: (0,) * nd, pipeline_mode=pl.Buffered(1))


def _row_tile(tp):
    return 384 if tp % 384 == 0 else BLOCK


def _in_proj_kernel(x_ref, g_ref, w_ref, qa_ref, qf_ref, fl_ref, xy_ref, gt_ref):
    u = _rms_norm(x_ref[...], g_ref[...]).astype(BF16)

    def proj(c0, width):
        return _dot(u, w_ref[:, c0:c0 + width])

    for c in range(0, A_COLS, 256):
        qa_ref[:, c:c + 256] = proj(OFF_A + c, 256).astype(BF16)
    for c in range(0, F_COLS, 512):
        qf_ref[:, c:c + 512] = proj(OFF_F + c, 512).astype(BF16)
    fl_ref[...] = proj(OFF_FL, LANES)
    xy_ref[:, 0:LRU_WIDTH] = proj(OFF_C, LRU_WIDTH).astype(BF16)
    xy_ref[:, LRU_WIDTH:] = _gelu_tanh(proj(OFF_C + LRU_WIDTH, LRU_WIDTH)).astype(BF16)
    for c in range(0, G_COLS, 512):
        gt_ref[:, c:c + 512] = _sigmoid(proj(OFF_G + c, 512)).astype(BF16)


def _in_proj(h2, g, w):
    rows = h2.shape[0]
    tm = 384 if rows % 384 == 0 else BLOCK
    row_spec = lambda cols: pl.BlockSpec((tm, cols), lambda i: (i, 0))
    return pl.pallas_call(
        _in_proj_kernel,
        grid=(rows // tm,),
        in_specs=[row_spec(D_MODEL), _resident((1, D_MODEL)), _resident((D_MODEL, IN_COLS_PACKED))],
        out_specs=[row_spec(A_COLS), row_spec(F_COLS), row_spec(LANES), row_spec(C_COLS), row_spec(G_COLS)],
        out_shape=[
            jax.ShapeDtypeStruct((rows, A_COLS), BF16),
            jax.ShapeDtypeStruct((rows, F_COLS), BF16),
            jax.ShapeDtypeStruct((rows, LANES), F32),
            jax.ShapeDtypeStruct((rows, C_COLS), BF16),
            jax.ShapeDtypeStruct((rows, G_COLS), BF16),
        ],
        compiler_params=pltpu.CompilerParams(dimension_semantics=("parallel",), vmem_limit_bytes=VMEM_LIMIT),
        name="in_proj",
    )(h2, g, w)


def _fox_prefix_kernel(fl_ref, fb_ref, ck_ref, cum_ref, *, tp, n_pad):
    lf = _log_sigmoid(fl_ref[0] + fb_ref[...])
    row = lax.broadcasted_iota(jnp.int32, (tp, 1), 0)
    lf = jnp.where(row >= n_pad, lf, 0.0)
    r = lax.broadcasted_iota(jnp.int32, (BLOCK, BLOCK), 0)
    c = lax.broadcasted_iota(jnp.int32, (BLOCK, BLOCK), 1)
    tri = jnp.where(r >= c, 1.0, 0.0).astype(F32)
    carry = jnp.zeros((1, LANES), F32)
    for blk in range(tp // BLOCK):
        sl = slice(blk * BLOCK, (blk + 1) * BLOCK)
        cum = jnp.dot(tri, lf[sl], preferred_element_type=F32, precision=lax.Precision.HIGHEST) + carry
        cum_ref[sl, :] = cum
        carry = cum[BLOCK - 1:BLOCK, :]
    ck_ref[0] = cum_ref[...].T[0:FOX_HEADS, :]


def _fox_prefix(fl3, fb, n_pad):
    b, tp, _ = fl3.shape
    return pl.pallas_call(
        functools.partial(_fox_prefix_kernel, tp=tp, n_pad=n_pad),
        grid=(b,),
        in_specs=[pl.BlockSpec((1, tp, LANES), lambda i: (i, 0, 0)), _resident((1, LANES))],
        out_specs=pl.BlockSpec((1, FOX_HEADS, tp), lambda i: (i, 0, 0)),
        out_shape=jax.ShapeDtypeStruct((b, FOX_HEADS, tp), F32),
        scratch_shapes=[pltpu.VMEM((tp, LANES), F32)],
        compiler_params=pltpu.CompilerParams(dimension_semantics=("parallel",), vmem_limit_bytes=VMEM_LIMIT),
        name="fox_prefix",
    )(fl3, fb)


def _swa_kernel(q_ref, kc_ref, vc_ref, kp_ref, vp_ref, bias_ref, sink_ref, o_ref, *, blocks, n_pad):
    ib = pl.program_id(1)
    lane = lax.broadcasted_iota(jnp.int32, (BLOCK, LANES), 1)
    lo = lane < HEAD_DIM
    kidx = lax.broadcasted_iota(jnp.int32, (1, 2 * BLOCK), 1)
    for g in range(blocks):
        cur = slice(g * BLOCK, (g + 1) * BLOCK)
        if g == 0:
            kprev, vprev = kp_ref[0], vp_ref[0]
        else:
            prev = slice((g - 1) * BLOCK, g * BLOCK)
            kprev, vprev = kc_ref[0, prev, :], vc_ref[0, prev, :]
        kband = jnp.concatenate([kprev, kc_ref[0, cur, :]], axis=0)
        vband = jnp.concatenate([vprev, vc_ref[0, cur, :]], axis=0)
        key_ok = (ib * blocks + g - 1) * BLOCK + kidx >= n_pad
        for j in range(SWA_Q_HEADS // 2):
            cols = slice(j * LANES, (j + 1) * LANES)
            qs = q_ref[0, cur, cols]
            halves = []
            for e in range(2):
                head = j + e * (SWA_Q_HEADS // 2)
                qm = jnp.where(lo if e == 0 else ~lo, qs, jnp.zeros_like(qs))
                s = _dot_nt(qm, kband) + bias_ref[head]
                s = jnp.where(key_ok, s, NEG_INF)
                sink = sink_ref[head:head + 1, 0:1]
                m = jnp.maximum(jnp.max(s, axis=-1, keepdims=True), sink)
                p = jnp.exp(s - m)
                den = jnp.sum(p, axis=-1, keepdims=True) + jnp.exp(sink - m)
                halves.append(_dot(p.astype(BF16), vband) / den)
            o_ref[0, cur, cols] = jnp.where(lo, halves[0], halves[1]).astype(BF16)


def _swa(qkva3, bias, sinks, n_pad):
    b, tp, _ = qkva3.shape
    nb = tp // BLOCK
    blocks = 3 if nb % 3 == 0 else 1
    rows = blocks * BLOCK
    qcols = SWA_Q_HEADS * HEAD_DIM
    kblk = qcols // LANES
    prev_row = lambda i, j: (i, jnp.maximum(j * blocks - 1, 0), kblk)
    prev_row_v = lambda i, j: (i, jnp.maximum(j * blocks - 1, 0), kblk + 1)
    return pl.pallas_call(
        functools.partial(_swa_kernel, blocks=blocks, n_pad=n_pad),
        grid=(b, nb // blocks),
        in_specs=[
            pl.BlockSpec((1, rows, qcols), lambda i, j: (i, j, 0)),
            pl.BlockSpec((1, rows, LANES), lambda i, j: (i, j, kblk)),
            pl.BlockSpec((1, rows, LANES), lambda i, j: (i, j, kblk + 1)),
            pl.BlockSpec((1, BLOCK, LANES), prev_row),
            pl.BlockSpec((1, BLOCK, LANES), prev_row_v),
            _resident((SWA_Q_HEADS, BLOCK, 2 * BLOCK)),
            _resident((SWA_Q_HEADS, LANES)),
        ],
        out_specs=pl.BlockSpec((1, rows, qcols), lambda i, j: (i, j, 0)),
        out_shape=jax.ShapeDtypeStruct((b, tp, qcols), BF16),
        compiler_params=pltpu.CompilerParams(dimension_semantics=("parallel", "parallel"), vmem_limit_bytes=VMEM_LIMIT),
        name="swa",
    )(qkva3, qkva3, qkva3, qkva3, qkva3, bias, sinks)


def _fox_kernel(q_ref, k_ref, v_ref, ck_ref, o_ref, m_ref, l_ref, acc_ref, *, tq, n_pad):
    i = pl.program_id(2)
    q = q_ref[0]
    lane = lax.broadcasted_iota(jnp.int32, (tq, LANES), 1)
    lo = lane < HEAD_DIM
    zero = jnp.zeros_like(q)
    qs = (jnp.where(lo, q, zero), jnp.where(lo, zero, q))
    m_ref[...] = jnp.full(m_ref.shape, NEG_INF, F32)
    l_ref[...] = jnp.zeros(l_ref.shape, F32)
    acc_ref[...] = jnp.zeros(acc_ref.shape, F32)

    def tile(j, masked):
        start = pl.multiple_of(j * tq, BLOCK)
        k = k_ref[0, pl.ds(start, tq), :]
        v = v_ref[0, pl.ds(start, tq), :]
        for e in range(2):
            s = _dot_nt(qs[e], k) - ck_ref[0, 0, e:e + 1, pl.ds(start, tq)]
            if masked:
                qpos = i * tq + lax.broadcasted_iota(jnp.int32, (tq, tq), 0)
                kpos = j * tq + lax.broadcasted_iota(jnp.int32, (tq, tq), 1)
                s = jnp.where((kpos <= qpos) & (kpos >= n_pad), s, NEG_INF)
            m_prev = m_ref[e]
            m_new = jnp.maximum(m_prev, jnp.max(s, axis=-1, keepdims=True))
            alpha = jnp.exp(m_prev - m_new)
            p = jnp.exp(s - m_new)
            l_ref[e] = alpha * l_ref[e] + jnp.sum(p, axis=-1, keepdims=True)
            acc_ref[e] = alpha * acc_ref[e] + _dot(p.astype(BF16), v)
            m_ref[e] = m_new

    @pl.when(i > 0)
    def _():
        tile(0, True)

    def body(j, carry):
        tile(j, False)
        return carry

    lax.fori_loop(1, i, body, 0)
    tile(i, True)
    o_ref[0] = jnp.where(lo, acc_ref[0] / l_ref[0], acc_ref[1] / l_ref[1]).astype(BF16)


def _fox(qkvf3, ck4, n_pad):
    b, tp, _ = qkvf3.shape
    tq = _row_tile(tp)
    pairs = FOX_HEADS // 2
    return pl.pallas_call(
        functools.partial(_fox_kernel, tq=tq, n_pad=n_pad),
        grid=(b, pairs, tp // tq),
        in_specs=[
            pl.BlockSpec((1, tq, LANES), lambda i, p, j: (i, j, p)),
            pl.BlockSpec((1, tp, LANES), lambda i, p, j: (i, 0, pairs + p)),
            pl.BlockSpec((1, tp, LANES), lambda i, p, j: (i, 0, 2 * pairs + p)),
            pl.BlockSpec((1, 1, 2, tp), lambda i, p, j: (i, p, 0, 0)),
        ],
        out_specs=pl.BlockSpec((1, tq, LANES), lambda i, p, j: (i, j, p)),
        out_shape=jax.ShapeDtypeStruct((b, tp, FOX_HEADS * HEAD_DIM), BF16),
        scratch_shapes=[
            pltpu.VMEM((2, tq, 1), F32),
            pltpu.VMEM((2, tq, 1), F32),
            pltpu.VMEM((2, tq, LANES), F32),
        ],
        compiler_params=pltpu.CompilerParams(
            dimension_semantics=("parallel", "parallel", "arbitrary"), vmem_limit_bytes=VMEM_LIMIT),
        name="fox",
    )(qkvf3, qkvf3, qkvf3, ck4)


SUBLANES = 8


def _lru_kernel(xy_ref, cw_ref, cb_ref, wg_ref, bg_ref, lam_ref, o_ref, xbuf, a_s, b_s, h_s, carry, *, tt, n_pad):
    it = pl.program_id(1)

    @pl.when(it == 0)
    def _():
        xbuf[0:SUBLANES, :] = jnp.zeros((SUBLANES, LRU_WIDTH), F32)
        carry[...] = jnp.zeros(carry.shape, F32)

    xbuf[SUBLANES:SUBLANES + tt, :] = xy_ref[0, :, 0:LRU_WIDTH].astype(F32)
    base = SUBLANES - (CONV_WIDTH - 1)
    conv = xbuf[base:base + tt, :] * cw_ref[0:1, :]
    for i in range(1, CONV_WIDTH):
        conv = conv + xbuf[base + i:base + i + tt, :] * cw_ref[i:i + 1, :]
    conv = conv + cb_ref[...]
    xbuf[0:SUBLANES, :] = xbuf[tt:tt + SUBLANES, :]

    z = _dot(conv.astype(BF16), wg_ref[...]) + bg_ref[...]
    r = _sigmoid(z[:, 0:LRU_WIDTH])
    gi = _sigmoid(z[:, LRU_WIDTH:])
    log_a = LRU_C * r * _log_sigmoid(lam_ref[...])
    row = it * tt + lax.broadcasted_iota(jnp.int32, (tt, 1), 0)
    a = jnp.exp(log_a)
    a_s[...] = a
    one_minus_a2 = -jnp.tanh(log_a) * (a * a + 1.0)
    b_s[...] = jnp.where(row >= n_pad, jnp.sqrt(one_minus_a2) * (gi * conv), 0.0)

    sub = lax.broadcasted_iota(jnp.int32, (SUBLANES, LRU_WIDTH), 0)

    def body(c, h_prev):
        r0 = pl.multiple_of(c * SUBLANES, SUBLANES)
        a = a_s[pl.ds(r0, SUBLANES), :]
        bb = b_s[pl.ds(r0, SUBLANES), :]
        for s in (1, 2, 4):
            keep = sub >= s
            bb = jnp.where(keep, a * pltpu.roll(bb, s, 0) + bb, bb)
            a = jnp.where(keep, a * pltpu.roll(a, s, 0), a)
        hc = a * h_prev + bb
        h_s[pl.ds(r0, SUBLANES), :] = hc
        return hc[SUBLANES - 1:SUBLANES, :]

    carry[...] = lax.fori_loop(0, tt // SUBLANES, body, carry[...])
    o_ref[0] = (h_s[...] * xy_ref[0, :, LRU_WIDTH:].astype(F32)).astype(BF16)


def _lru(xy3, cw, cb, wg, bg, lam, n_pad):
    b, tp, _ = xy3.shape
    tt = _row_tile(tp)
    w = LRU_WIDTH
    return pl.pallas_call(
        functools.partial(_lru_kernel, tt=tt, n_pad=n_pad),
        grid=(b, tp // tt),
        in_specs=[
            pl.BlockSpec((1, tt, C_COLS), lambda i, j: (i, j, 0)),
            _resident((CONV_WIDTH, w)), _resident((1, w)), _resident((w, 2 * w)), _resident((1, 2 * w)),
            _resident((1, w)),
        ],
        out_specs=pl.BlockSpec((1, tt, w), lambda i, j: (i, j, 0)),
        out_shape=jax.ShapeDtypeStruct((b, tp, w), BF16),
        scratch_shapes=[
            pltpu.VMEM((tt + SUBLANES, w), F32), pltpu.VMEM((tt, w), F32), pltpu.VMEM((tt, w), F32),
            pltpu.VMEM((tt, w), F32), pltpu.VMEM((1, w), F32),
        ],
        compiler_params=pltpu.CompilerParams(dimension_semantics=("parallel", "arbitrary"), vmem_limit_bytes=VMEM_LIMIT),
        name="lru",
    )(xy3, cw, cb, wg, bg, lam)


FF_CHUNK = 256


def _mix_ffn_kernel(h_ref, oa_ref, of_ref, oc_ref, gt_ref, wb_ref, wo_ref, gn_ref, w1_ref, w2_ref, out_ref, act_s,
                    *, tm, n_pad):
    j = pl.program_id(1)
    d = D_MODEL
    merged = gt_ref[0, :, 0:d].astype(F32) * _dot(oa_ref[0], wb_ref[0])
    merged = merged + gt_ref[0, :, d:2 * d].astype(F32) * _dot(of_ref[0], wb_ref[1])
    merged = merged + gt_ref[0, :, 2 * d:3 * d].astype(F32) * _dot(oc_ref[0], wb_ref[2])
    hm = h_ref[0] + _dot(merged.astype(BF16), wo_ref[...])
    u = _rms_norm(hm, gn_ref[...]).astype(BF16)
    for c in range(0, D_FF, FF_CHUNK):
        gate = _dot(u, w1_ref[:, c:c + FF_CHUNK])
        up = _dot(u, w1_ref[:, D_FF + c:D_FF + c + FF_CHUNK])
        act_s[:, c:c + FF_CHUNK] = (gate * _sigmoid(gate) * up).astype(BF16)
    y = hm + _dot(act_s[...], w2_ref[...])
    row = j * tm + lax.broadcasted_iota(jnp.int32, (tm, 1), 0)
    out_ref[0] = jnp.where(row >= n_pad, y, 0.0)


def _mix_ffn(h3, oa, of, oc, gt3, wb, wo, gn, w1, w2, n_pad):
    b, tp, d = h3.shape
    tm = _row_tile(tp)
    w = LRU_WIDTH
    tile = lambda cols: pl.BlockSpec((1, tm, cols), lambda i, j: (i, j, 0))
    return pl.pallas_call(
        functools.partial(_mix_ffn_kernel, tm=tm, n_pad=n_pad),
        grid=(b, tp // tm),
        in_specs=[
            tile(d), tile(w), tile(w), tile(w), tile(G_COLS),
            _resident((N_BRANCH, w, d)), _resident((d, d)), _resident((1, d)),
            _resident((d, 2 * D_FF)), _resident((D_FF, d)),
        ],
        out_specs=tile(d),
        out_shape=jax.ShapeDtypeStruct((b, tp, d), F32),
        scratch_shapes=[pltpu.VMEM((tm, D_FF), BF16)],
        compiler_params=pltpu.CompilerParams(dimension_semantics=("parallel", "parallel"), vmem_limit_bytes=VMEM_LIMIT),
        name="mix_ffn",
    )(h3, oa, of, oc, gt3, wb, wo, gn, w1, w2)


def _final_norm_kernel(h_ref, g_ref, o_ref):
    o_ref[0] = _rms_norm(h_ref[0], g_ref[...])


def _final_norm(h3, g, skip_blocks, seq):
    b, tp, d = h3.shape
    return pl.pallas_call(
        _final_norm_kernel,
        grid=(b, seq // BLOCK),
        in_specs=[pl.BlockSpec((1, BLOCK, d), lambda i, j: (i, j + skip_blocks, 0)), _resident((1, d))],
        out_specs=pl.BlockSpec((1, BLOCK, d), lambda i, j: (i, j, 0)),
        out_shape=jax.ShapeDtypeStruct((b, seq, d), F32),
        compiler_params=pltpu.CompilerParams(dimension_semantics=("parallel", "parallel"), vmem_limit_bytes=VMEM_LIMIT),
        name="final_norm",
    )(h3, g)


def _t5_bucket(dist):
    max_exact = REL_BUCKETS // 2
    d = jnp.maximum(dist, 0)
    scaled = jnp.log(jnp.maximum(d, 1).astype(F32) / max_exact) / math.log(REL_MAX_DIST / max_exact)
    large = jnp.minimum(max_exact + (scaled * (REL_BUCKETS - max_exact)).astype(jnp.int32), REL_BUCKETS - 1)
    return jnp.where(d < max_exact, d, large)


def _swa_bias(rel_table):
    q_idx = jnp.arange(BLOCK)[:, None]
    k_idx = jnp.arange(2 * BLOCK)[None, :]
    dist = q_idx + BLOCK - k_idx
    bias = rel_table.astype(F32)[_t5_bucket(dist)].transpose(2, 0, 1)
    ok = (dist >= 0) & (dist < SWA_WINDOW)
    return jnp.where(ok[None], bias, NEG_INF)


def _slab_head_order():
    half = SWA_Q_HEADS // 2
    return [h for j in range(half) for h in (j, j + half)]


def _pack_w_in(w_in):
    depth = w_in.shape[0]
    sizes = (512, 128, 128, 512, 512, 512, 8, 512, 512, 3072)
    offs = [0]
    for s in sizes:
        offs.append(offs[-1] + s)
    qa, ka, va, qf, kf, vf, fl, xc, yc, gates = [w_in[:, :, offs[i]:offs[i + 1]] for i in range(len(sizes))]
    scale = HEAD_DIM ** -0.5
    qa = qa.reshape(depth, D_MODEL, SWA_Q_HEADS, HEAD_DIM)[:, :, jnp.array(_slab_head_order())]
    qa = qa.reshape(depth, D_MODEL, SWA_Q_HEADS * HEAD_DIM) * scale
    fl = jnp.pad(fl, ((0, 0), (0, 0), (0, LANES - FOX_HEADS)))
    packed = jnp.concatenate([qa, ka, va, qf * scale, kf, vf, fl, xc, yc, gates], axis=-1)
    return packed.astype(BF16)


def _block_diag(w):
    depth, nb, n, _ = w.shape
    eye = jnp.eye(nb, dtype=w.dtype)
    return jnp.einsum('lhij,hk->lhikj', w, eye).reshape(depth, nb * n, nb * n)


def kernel(x, meta_tokens, rel_bias_table, norm_mix, w_in, swa_sinks, fox_forget_bias, conv_w, conv_b,
           lru_w_r, lru_b_r, lru_w_i, lru_b_i, lru_lambda, w_branch, w_out, norm_ffn, w_ffn_in, w_ffn_out,
           norm_final):
    b, seq, d = x.shape
    depth = w_in.shape[0]
    t = N_META + seq
    n_pad = (-t) % BLOCK
    tp = t + n_pad
    assert d == D_MODEL and (n_pad + N_META) % BLOCK == 0

    meta = jnp.broadcast_to(meta_tokens.astype(x.dtype)[None], (b, N_META, d))
    h = jnp.concatenate([jnp.zeros((b, n_pad, d), x.dtype), meta, x], axis=1)

    w_in_p = _pack_w_in(w_in)
    bias_a = _swa_bias(rel_bias_table)
    order = jnp.array(_slab_head_order())
    sinks = jnp.broadcast_to(swa_sinks.astype(F32)[:, :, None], (depth, SWA_Q_HEADS, LANES))
    fb = jnp.pad(fox_forget_bias.astype(F32), ((0, 0), (0, LANES - FOX_HEADS)))[:, None, :]
    wg = jnp.concatenate([_block_diag(lru_w_r), _block_diag(lru_w_i)], axis=-1).astype(BF16)
    bg = jnp.concatenate([lru_b_r, lru_b_i], axis=-1).astype(F32)[:, None, :]
    wb_a = w_branch[:, 0].reshape(depth, SWA_Q_HEADS, HEAD_DIM, d)[:, order].reshape(depth, LRU_WIDTH, d)
    wb = jnp.stack([wb_a, w_branch[:, 1], w_branch[:, 2]], axis=1).astype(BF16)
    wo = w_out.astype(BF16)
    w1 = w_ffn_in.astype(BF16)
    w2 = w_ffn_out.astype(BF16)

    for l in range(depth):
        qkva, qkvf, fl, xy, gt = _in_proj(h.reshape(b * tp, d), norm_mix[l][None, :], w_in_p[l])
        ck = _fox_prefix(fl.reshape(b, tp, LANES), fb[l], n_pad)
        o_a = _swa(qkva.reshape(b, tp, A_COLS), bias_a, sinks[l], n_pad)
        o_f = _fox(qkvf.reshape(b, tp, F_COLS), ck.reshape(b, FOX_HEADS // 2, 2, tp), n_pad)
        o_c = _lru(xy.reshape(b, tp, C_COLS), conv_w[l], conv_b[l][None, :], wg[l], bg[l], lru_lambda[l][None, :],
                   n_pad)
        h = _mix_ffn(h, o_a, o_f, o_c, gt.reshape(b, tp, G_COLS), wb[l], wo[l], norm_ffn[l][None, :], w1[l], w2[l],
                     n_pad)
    return _final_norm(h, norm_final[None, :], (n_pad + N_META) // BLOCK, seq)
```

```python
import functools
import math

import jax
import jax.numpy as jnp
from jax import lax
from jax.experimental import pallas as pl
from jax.experimental.pallas import tpu as pltpu

F32 = jnp.float32
BF16 = jnp.bfloat16

D_MODEL = 1024
HEAD_DIM = 64
N_META = 16
BLOCK = 128
LANES = 128
NEG_INF = -1e30
SWA_WINDOW = 128
SWA_Q_HEADS = 8
SWA_KV_HEADS = 2
FOX_HEADS = 8
LRU_WIDTH = D_MODEL // 2
LRU_BLOCKS = 8
CONV_WIDTH = 4
LRU_C = 8.0
REL_BUCKETS = 32
REL_MAX_DIST = 128
D_FF = 2816
N_BRANCH = 3
EPS = 1e-6

A_COLS = SWA_Q_HEADS * HEAD_DIM + 2 * SWA_KV_HEADS * HEAD_DIM
FOX_WIDTH = FOX_HEADS * HEAD_DIM
F_COLS = 2 * FOX_WIDTH
C_COLS = 2 * LRU_WIDTH
G_COLS = N_BRANCH * D_MODEL
OFF_A = 0
OFF_F = OFF_A + A_COLS
OFF_FL = OFF_F + F_COLS
OFF_C = OFF_FL + LANES
OFF_G = OFF_C + C_COLS
IN_COLS_PACKED = OFF_G + G_COLS

VMEM_LIMIT = 56 * 1024 * 1024


def _sigmoid(x):
    return 1.0 / (1.0 + jnp.exp(-x))


def _log_sigmoid(x):
    return jnp.minimum(x, 0.0) - jnp.log1p(jnp.exp(-jnp.abs(x)))


def _gelu_tanh(x):
    c = math.sqrt(2.0 / math.pi)
    return 0.5 * x * (1.0 + jnp.tanh(c * (x + 0.044715 * (x * x * x))))


def _rms_norm(x, g):
    ms = jnp.mean(x * x, axis=-1, keepdims=True)
    return x * lax.rsqrt(ms + EPS) * g


def _dot(a, b):
    return jnp.dot(a, b, preferred_element_type=F32)


def _dot_nt(a, b):
    return lax.dot_general(a, b, (((1,), (1,)), ((), ())), preferred_element_type=F32)


def _resident(shape):
    nd = len(shape)
    return pl.BlockSpec(shape, lambda *_: (0,) * nd, pipeline_mode=pl.Buffered(1))


def _row_tile(tp):
    return 384 if tp % 384 == 0 else BLOCK


def _in_proj_kernel(x_ref, g_ref, w_ref, wkt_ref, qa_ref, qf_ref, kt_ref, fl_ref, xy_ref, gt_ref):
    u = _rms_norm(x_ref[...], g_ref[...]).astype(BF16)

    def proj(c0, width):
        return _dot(u, w_ref[:, c0:c0 + width])

    for c in range(0, A_COLS, 256):
        qa_ref[:, c:c + 256] = proj(OFF_A + c, 256).astype(BF16)
    for c in range(0, F_COLS, 512):
        qf_ref[:, c:c + 512] = proj(OFF_F + c, 512).astype(BF16)
    kt_ref[...] = _dot_nt(wkt_ref[...], u).astype(BF16)
    fl_ref[...] = proj(OFF_FL, LANES)
    xy_ref[:, 0:LRU_WIDTH] = proj(OFF_C, LRU_WIDTH).astype(BF16)
    xy_ref[:, LRU_WIDTH:] = _gelu_tanh(proj(OFF_C + LRU_WIDTH, LRU_WIDTH)).astype(BF16)
    for c in range(0, G_COLS, 512):
        gt_ref[:, c:c + 512] = _sigmoid(proj(OFF_G + c, 512)).astype(BF16)


def _in_proj(h2, g, w, wkt):
    rows = h2.shape[0]
    tm = 384 if rows % 384 == 0 else BLOCK
    row_spec = lambda cols: pl.BlockSpec((tm, cols), lambda i: (i, 0))
    return pl.pallas_call(
        _in_proj_kernel,
        grid=(rows // tm,),
        in_specs=[row_spec(D_MODEL), _resident((1, D_MODEL)), _resident((D_MODEL, IN_COLS_PACKED)),
                  _resident((FOX_WIDTH, D_MODEL))],
        out_specs=[row_spec(A_COLS), row_spec(F_COLS), pl.BlockSpec((FOX_WIDTH, tm), lambda i: (0, i)),
                   row_spec(LANES), row_spec(C_COLS), row_spec(G_COLS)],
        out_shape=[
            jax.ShapeDtypeStruct((rows, A_COLS), BF16),
            jax.ShapeDtypeStruct((rows, F_COLS), BF16),
            jax.ShapeDtypeStruct((FOX_WIDTH, rows), BF16),
            jax.ShapeDtypeStruct((rows, LANES), F32),
            jax.ShapeDtypeStruct((rows, C_COLS), BF16),
            jax.ShapeDtypeStruct((rows, G_COLS), BF16),
        ],
        compiler_params=pltpu.CompilerParams(dimension_semantics=("parallel",), vmem_limit_bytes=VMEM_LIMIT),
        name="in_proj",
    )(h2, g, w, wkt)


def _fox_prefix_kernel(fl_ref, fb_ref, ck_ref, cum_ref, *, tp, n_pad):
    lf = _log_sigmoid(fl_ref[0] + fb_ref[...])
    row = lax.broadcasted_iota(jnp.int32, (tp, 1), 0)
    lf = jnp.where(row >= n_pad, lf, 0.0)
    r = lax.broadcasted_iota(jnp.int32, (BLOCK, BLOCK), 0)
    c = lax.broadcasted_iota(jnp.int32, (BLOCK, BLOCK), 1)
    tri = jnp.where(r >= c, 1.0, 0.0).astype(F32)
    carry = jnp.zeros((1, LANES), F32)
    for blk in range(tp // BLOCK):
        sl = slice(blk * BLOCK, (blk + 1) * BLOCK)
        cum = jnp.dot(tri, lf[sl], preferred_element_type=F32, precision=lax.Precision.HIGHEST) + carry
        cum_ref[sl, :] = cum
        carry = cum[BLOCK - 1:BLOCK, :]
    ck_ref[0] = cum_ref[...].T[0:FOX_HEADS, :]


def _fox_prefix(fl3, fb, n_pad):
    b, tp, _ = fl3.shape
    return pl.pallas_call(
        functools.partial(_fox_prefix_kernel, tp=tp, n_pad=n_pad),
        grid=(b,),
        in_specs=[pl.BlockSpec((1, tp, LANES), lambda i: (i, 0, 0)), _resident((1, LANES))],
        out_specs=pl.BlockSpec((1, FOX_HEADS, tp), lambda i: (i, 0, 0)),
        out_shape=jax.ShapeDtypeStruct((b, FOX_HEADS, tp), F32),
        scratch_shapes=[pltpu.VMEM((tp, LANES), F32)],
        compiler_params=pltpu.CompilerParams(dimension_semantics=("parallel",), vmem_limit_bytes=VMEM_LIMIT),
        name="fox_prefix",
    )(fl3, fb)


def _swa_kernel(q_ref, kc_ref, vc_ref, kp_ref, vp_ref, bias_ref, sink_ref, o_ref, *, blocks, n_pad):
    ib = pl.program_id(1)
    lane = lax.broadcasted_iota(jnp.int32, (BLOCK, LANES), 1)
    lo = lane < HEAD_DIM
    kidx = lax.broadcasted_iota(jnp.int32, (1, 2 * BLOCK), 1)
    for g in range(blocks):
        cur = slice(g * BLOCK, (g + 1) * BLOCK)
        if g == 0:
            kprev, vprev = kp_ref[0], vp_ref[0]
        else:
            prev = slice((g - 1) * BLOCK, g * BLOCK)
            kprev, vprev = kc_ref[0, prev, :], vc_ref[0, prev, :]
        kband = jnp.concatenate([kprev, kc_ref[0, cur, :]], axis=0)
        vband = jnp.concatenate([vprev, vc_ref[0, cur, :]], axis=0)
        key_ok = (ib * blocks + g - 1) * BLOCK + kidx >= n_pad
        for j in range(SWA_Q_HEADS // 2):
            cols = slice(j * LANES, (j + 1) * LANES)
            qs = q_ref[0, cur, cols]
            halves = []
            for e in range(2):
                head = j + e * (SWA_Q_HEADS // 2)
                qm = jnp.where(lo if e == 0 else ~lo, qs, jnp.zeros_like(qs))
                s = _dot_nt(qm, kband) + bias_ref[head]
                s = jnp.where(key_ok, s, NEG_INF)
                sink = sink_ref[head:head + 1, 0:1]
                m = jnp.maximum(jnp.max(s, axis=-1, keepdims=True), sink)
                p = jnp.exp(s - m)
                den = jnp.sum(p, axis=-1, keepdims=True) + jnp.exp(sink - m)
                halves.append(_dot(p.astype(BF16), vband) / den)
            o_ref[0, cur, cols] = jnp.where(lo, halves[0], halves[1]).astype(BF16)


def _swa(qkva3, bias, sinks, n_pad):
    b, tp, _ = qkva3.shape
    nb = tp // BLOCK
    blocks = 3 if nb % 3 == 0 else 1
    rows = blocks * BLOCK
    qcols = SWA_Q_HEADS * HEAD_DIM
    kblk = qcols // LANES
    prev_row = lambda i, j: (i, jnp.maximum(j * blocks - 1, 0), kblk)
    prev_row_v = lambda i, j: (i, jnp.maximum(j * blocks - 1, 0), kblk + 1)
    return pl.pallas_call(
        functools.partial(_swa_kernel, blocks=blocks, n_pad=n_pad),
        grid=(b, nb // blocks),
        in_specs=[
            pl.BlockSpec((1, rows, qcols), lambda i, j: (i, j, 0)),
            pl.BlockSpec((1, rows, LANES), lambda i, j: (i, j, kblk)),
            pl.BlockSpec((1, rows, LANES), lambda i, j: (i, j, kblk + 1)),
            pl.BlockSpec((1, BLOCK, LANES), prev_row),
            pl.BlockSpec((1, BLOCK, LANES), prev_row_v),
            _resident((SWA_Q_HEADS, BLOCK, 2 * BLOCK)),
            _resident((SWA_Q_HEADS, LANES)),
        ],
        out_specs=pl.BlockSpec((1, rows, qcols), lambda i, j: (i, j, 0)),
        out_shape=jax.ShapeDtypeStruct((b, tp, qcols), BF16),
        compiler_params=pltpu.CompilerParams(dimension_semantics=("parallel", "parallel"), vmem_limit_bytes=VMEM_LIMIT),
        name="swa",
    )(qkva3, qkva3, qkva3, qkva3, qkva3, bias, sinks)


FOX_TQ = 384
FOX_TK = 512
AUG_ROWS = 16


def _fox_kernel(q_ref, v_ref, kt_ref, ck_ref, o_ref, qa_s, m_s, acc_s, *, tq, tk, first, n_pad):
    i = pl.program_id(1)
    pairs = FOX_HEADS // 2
    lane = lax.broadcasted_iota(jnp.int32, (tq, LANES), 1)
    lo = lane < HEAD_DIM
    ones_pat = jnp.where(lane < 3, 1.0, 0.0).astype(BF16)
    for hp in range(pairs):
        q = q_ref[0, :, hp * LANES:(hp + 1) * LANES]
        zero = jnp.zeros_like(q)
        for e in range(2):
            qa_s[2 * hp + e, :, 0:LANES] = jnp.where(lo, q, zero) if e == 0 else jnp.where(lo, zero, q)
            qa_s[2 * hp + e, :, LANES:] = ones_pat
    m_s[...] = jnp.full(m_s.shape, NEG_INF, F32)
    acc_s[...] = jnp.zeros(acc_s.shape, F32)

    def tile(start, size, masked):
        sub = lax.broadcasted_iota(jnp.int32, (AUG_ROWS, size), 0)
        if masked:
            qpos = i * tq + lax.broadcasted_iota(jnp.int32, (tq, size), 0)
            kpos = start + lax.broadcasted_iota(jnp.int32, (tq, size), 1)
            ok = (kpos <= qpos) & (kpos >= n_pad)
        one = jnp.ones((size, LANES), BF16)
        lo_k = lax.broadcasted_iota(jnp.int32, (size, LANES), 1) < HEAD_DIM
        for hp in range(pairs):
            kt = kt_ref[hp * LANES:(hp + 1) * LANES, pl.ds(start, size)]
            v = v_ref[0, pl.ds(start, size), hp * LANES:(hp + 1) * LANES]
            for e in range(2):
                h = 2 * hp + e
                c = -ck_ref[0, h:h + 1, pl.ds(start, size)]
                c_hi = c.astype(BF16).astype(F32)
                c_mid = (c - c_hi).astype(BF16).astype(F32)
                c_lo = (c - c_hi) - c_mid
                aug = jnp.where(sub == 0, c_hi, jnp.where(sub == 1, c_mid, jnp.where(sub == 2, c_lo, 0.0)))
                kaug = jnp.concatenate(
                    [kt, aug.astype(BF16), jnp.zeros((LANES - AUG_ROWS, size), BF16)], axis=0)
                s = _dot(qa_s[h], kaug)
                if masked:
                    s = jnp.where(ok, s, NEG_INF)
                chunks = [s[:, c0:c0 + LANES] for c0 in range(0, size, LANES)]
                mx = functools.reduce(jnp.maximum, chunks)
                m_prev = m_s[h]
                m_new = jnp.maximum(m_prev, jnp.max(mx, axis=-1, keepdims=True))
                alpha = jnp.exp(m_prev - m_new)
                p = jnp.concatenate([jnp.exp(ch - m_new) for ch in chunks], axis=1).astype(BF16)
                ve = jnp.where(lo_k, v, one) if e == 0 else jnp.where(lo_k, one, v)
                acc_s[h] = alpha * acc_s[h] + _dot(p, ve)
                m_s[h] = m_new

    tile(0, first, True)
    n_full = jnp.maximum(tq * i - (first - 1), 0) // tk
    n_last = (tq * (i + 1) - first - 1) // tk

    def body(masked):
        def f(j, carry):
            tile(pl.multiple_of(first + j * tk, LANES), tk, masked)
            return carry
        return f

    lax.fori_loop(0, n_full, body(False), 0)
    lax.fori_loop(n_full, n_last + 1, body(True), 0)
    for hp in range(pairs):
        a0, a1 = acc_s[2 * hp], acc_s[2 * hp + 1]
        o = jnp.where(lo, a0 / pltpu.roll(a0, HEAD_DIM, 1), a1 / pltpu.roll(a1, HEAD_DIM, 1))
        o_ref[0, :, hp * LANES:(hp + 1) * LANES] = o.astype(BF16)


def _fox(qv3, kt, ck, n_pad):
    b, tp, _ = qv3.shape
    tq, tk, first = FOX_TQ, FOX_TK, n_pad + N_META
    assert tp % tq == 0 and (tp - first) % tk == 0 and first % LANES == 0 and tq > first
    return pl.pallas_call(
        functools.partial(_fox_kernel, tq=tq, tk=tk, first=first, n_pad=n_pad),
        grid=(b, tp // tq),
        in_specs=[
            pl.BlockSpec((1, tq, FOX_WIDTH), lambda i, j: (i, j, 0)),
            pl.BlockSpec((1, tp, FOX_WIDTH), lambda i, j: (i, 0, 1)),
            pl.BlockSpec((FOX_WIDTH, tp), lambda i, j: (0, i)),
            pl.BlockSpec((1, FOX_HEADS, tp), lambda i, j: (i, 0, 0)),
        ],
        out_specs=pl.BlockSpec((1, tq, FOX_WIDTH), lambda i, j: (i, j, 0)),
        out_shape=jax.ShapeDtypeStruct((b, tp, FOX_WIDTH), BF16),
        scratch_shapes=[
            pltpu.VMEM((FOX_HEADS, tq, 2 * LANES), BF16),
            pltpu.VMEM((FOX_HEADS, tq, LANES), F32),
            pltpu.VMEM((FOX_HEADS, tq, LANES), F32),
        ],
        compiler_params=pltpu.CompilerParams(
            dimension_semantics=("parallel", "arbitrary"), vmem_limit_bytes=VMEM_LIMIT),
        name="fox",
    )(qv3, qv3, kt, ck)


SUBLANES = 8


def _lru_kernel(xy_ref, cw_ref, cb_ref, wg_ref, bg_ref, lam_ref, o_ref, xbuf, a_s, b_s, h_s, carry, *, tt, n_pad):
    it = pl.program_id(1)

    @pl.when(it == 0)
    def _():
        xbuf[0:SUBLANES, :] = jnp.zeros((SUBLANES, LRU_WIDTH), F32)
        carry[...] = jnp.zeros(carry.shape, F32)

    xbuf[SUBLANES:SUBLANES + tt, :] = xy_ref[0, :, 0:LRU_WIDTH].astype(F32)
    base = SUBLANES - (CONV_WIDTH - 1)
    conv = xbuf[base:base + tt, :] * cw_ref[0:1, :]
    for i in range(1, CONV_WIDTH):
        conv = conv + xbuf[base + i:base + i + tt, :] * cw_ref[i:i + 1, :]
    conv = conv + cb_ref[...]
    xbuf[0:SUBLANES, :] = xbuf[tt:tt + SUBLANES, :]

    z = _dot(conv.astype(BF16), wg_ref[...]) + bg_ref[...]
    r = _sigmoid(z[:, 0:LRU_WIDTH])
    gi = _sigmoid(z[:, LRU_WIDTH:])
    log_a = LRU_C * r * _log_sigmoid(lam_ref[...])
    row = it * tt + lax.broadcasted_iota(jnp.int32, (tt, 1), 0)
    a = jnp.exp(log_a)
    a_s[...] = a
    one_minus_a2 = -jnp.tanh(log_a) * (a * a + 1.0)
    b_s[...] = jnp.where(row >= n_pad, jnp.sqrt(one_minus_a2) * (gi * conv), 0.0)

    sub = lax.broadcasted_iota(jnp.int32, (SUBLANES, LRU_WIDTH), 0)

    def body(c, h_prev):
        r0 = pl.multiple_of(c * SUBLANES, SUBLANES)
        a = a_s[pl.ds(r0, SUBLANES), :]
        bb = b_s[pl.ds(r0, SUBLANES), :]
        for s in (1, 2, 4):
            keep = sub >= s
            bb = jnp.where(keep, a * pltpu.roll(bb, s, 0) + bb, bb)
            a = jnp.where(keep, a * pltpu.roll(a, s, 0), a)
        hc = a * h_prev + bb
        h_s[pl.ds(r0, SUBLANES), :] = hc
        return hc[SUBLANES - 1:SUBLANES, :]

    carry[...] = lax.fori_loop(0, tt // SUBLANES, body, carry[...])
    o_ref[0] = (h_s[...] * xy_ref[0, :, LRU_WIDTH:].astype(F32)).astype(BF16)


def _lru(xy3, cw, cb, wg, bg, lam, n_pad):
    b, tp, _ = xy3.shape
    tt = _row_tile(tp)
    w = LRU_WIDTH
    return pl.pallas_call(
        functools.partial(_lru_kernel, tt=tt, n_pad=n_pad),
        grid=(b, tp // tt),
        in_specs=[
            pl.BlockSpec((1, tt, C_COLS), lambda i, j: (i, j, 0)),
            _resident((CONV_WIDTH, w)), _resident((1, w)), _resident((w, 2 * w)), _resident((1, 2 * w)),
            _resident((1, w)),
        ],
        out_specs=pl.BlockSpec((1, tt, w), lambda i, j: (i, j, 0)),
        out_shape=jax.ShapeDtypeStruct((b, tp, w), BF16),
        scratch_shapes=[
            pltpu.VMEM((tt + SUBLANES, w), F32), pltpu.VMEM((tt, w), F32), pltpu.VMEM((tt, w), F32),
            pltpu.VMEM((tt, w), F32), pltpu.VMEM((1, w), F32),
        ],
        compiler_params=pltpu.CompilerParams(dimension_semantics=("parallel", "arbitrary"), vmem_limit_bytes=VMEM_LIMIT),
        name="lru",
    )(xy3, cw, cb, wg, bg, lam)


FF_CHUNK = 256


def _mix_ffn_kernel(h_ref, oa_ref, of_ref, oc_ref, gt_ref, wb_ref, wo_ref, gn_ref, w1_ref, w2_ref, out_ref, act_s,
                    *, tm, n_pad):
    j = pl.program_id(1)
    d = D_MODEL
    merged = gt_ref[0, :, 0:d].astype(F32) * _dot(oa_ref[0], wb_ref[0])
    merged = merged + gt_ref[0, :, d:2 * d].astype(F32) * _dot(of_ref[0], wb_ref[1])
    merged = merged + gt_ref[0, :, 2 * d:3 * d].astype(F32) * _dot(oc_ref[0], wb_ref[2])
    hm = h_ref[0] + _dot(merged.astype(BF16), wo_ref[...])
    u = _rms_norm(hm, gn_ref[...]).astype(BF16)
    for c in range(0, D_FF, FF_CHUNK):
        gate = _dot(u, w1_ref[:, c:c + FF_CHUNK])
        up = _dot(u, w1_ref[:, D_FF + c:D_FF + c + FF_CHUNK])
        act_s[:, c:c + FF_CHUNK] = (gate * _sigmoid(gate) * up).astype(BF16)
    y = hm + _dot(act_s[...], w2_ref[...])
    row = j * tm + lax.broadcasted_iota(jnp.int32, (tm, 1), 0)
    out_ref[0] = jnp.where(row >= n_pad, y, 0.0)


def _mix_ffn(h3, oa, of, oc, gt3, wb, wo, gn, w1, w2, n_pad):
    b, tp, d = h3.shape
    tm = _row_tile(tp)
    w = LRU_WIDTH
    tile = lambda cols: pl.BlockSpec((1, tm, cols), lambda i, j: (i, j, 0))
    return pl.pallas_call(
        functools.partial(_mix_ffn_kernel, tm=tm, n_pad=n_pad),
        grid=(b, tp // tm),
        in_specs=[
            tile(d), tile(w), tile(w), tile(w), tile(G_COLS),
            _resident((N_BRANCH, w, d)), _resident((d, d)), _resident((1, d)),
            _resident((d, 2 * D_FF)), _resident((D_FF, d)),
        ],
        out_specs=tile(d),
        out_shape=jax.ShapeDtypeStruct((b, tp, d), F32),
        scratch_shapes=[pltpu.VMEM((tm, D_FF), BF16)],
        compiler_params=pltpu.CompilerParams(dimension_semantics=("parallel", "parallel"), vmem_limit_bytes=VMEM_LIMIT),
        name="mix_ffn",
    )(h3, oa, of, oc, gt3, wb, wo, gn, w1, w2)


def _final_norm_kernel(h_ref, g_ref, o_ref):
    o_ref[0] = _rms_norm(h_ref[0], g_ref[...])


def _final_norm(h3, g, skip_blocks, seq):
    b, tp, d = h3.shape
    return pl.pallas_call(
        _final_norm_kernel,
        grid=(b, seq // BLOCK),
        in_specs=[pl.BlockSpec((1, BLOCK, d), lambda i, j: (i, j + skip_blocks, 0)), _resident((1, d))],
        out_specs=pl.BlockSpec((1, BLOCK, d), lambda i, j: (i, j, 0)),
        out_shape=jax.ShapeDtypeStruct((b, seq, d), F32),
        compiler_params=pltpu.CompilerParams(dimension_semantics=("parallel", "parallel"), vmem_limit_bytes=VMEM_LIMIT),
        name="final_norm",
    )(h3, g)


def _t5_bucket(dist):
    max_exact = REL_BUCKETS // 2
    d = jnp.maximum(dist, 0)
    scaled = jnp.log(jnp.maximum(d, 1).astype(F32) / max_exact) / math.log(REL_MAX_DIST / max_exact)
    large = jnp.minimum(max_exact + (scaled * (REL_BUCKETS - max_exact)).astype(jnp.int32), REL_BUCKETS - 1)
    return jnp.where(d < max_exact, d, large)


def _swa_bias(rel_table):
    q_idx = jnp.arange(BLOCK)[:, None]
    k_idx = jnp.arange(2 * BLOCK)[None, :]
    dist = q_idx + BLOCK - k_idx
    bias = rel_table.astype(F32)[_t5_bucket(dist)].transpose(2, 0, 1)
    ok = (dist >= 0) & (dist < SWA_WINDOW)
    return jnp.where(ok[None], bias, NEG_INF)


def _slab_head_order():
    half = SWA_Q_HEADS // 2
    return [h for j in range(half) for h in (j, j + half)]


def _pack_w_in(w_in):
    depth = w_in.shape[0]
    sizes = (512, 128, 128, 512, 512, 512, 8, 512, 512, 3072)
    offs = [0]
    for s in sizes:
        offs.append(offs[-1] + s)
    qa, ka, va, qf, kf, vf, fl, xc, yc, gates = [w_in[:, :, offs[i]:offs[i + 1]] for i in range(len(sizes))]
    scale = HEAD_DIM ** -0.5
    qa = qa.reshape(depth, D_MODEL, SWA_Q_HEADS, HEAD_DIM)[:, :, jnp.array(_slab_head_order())]
    qa = qa.reshape(depth, D_MODEL, SWA_Q_HEADS * HEAD_DIM) * scale
    fl = jnp.pad(fl, ((0, 0), (0, 0), (0, LANES - FOX_HEADS)))
    packed = jnp.concatenate([qa, ka, va, qf * scale, vf, fl, xc, yc, gates], axis=-1)
    return packed.astype(BF16), jnp.swapaxes(kf, 1, 2).astype(BF16)


def _block_diag(w):
    depth, nb, n, _ = w.shape
    eye = jnp.eye(nb, dtype=w.dtype)
    return jnp.einsum('lhij,hk->lhikj', w, eye).reshape(depth, nb * n, nb * n)


def kernel(x, meta_tokens, rel_bias_table, norm_mix, w_in, swa_sinks, fox_forget_bias, conv_w, conv_b,
           lru_w_r, lru_b_r, lru_w_i, lru_b_i, lru_lambda, w_branch, w_out, norm_ffn, w_ffn_in, w_ffn_out,
           norm_final):
    b, seq, d = x.shape
    depth = w_in.shape[0]
    t = N_META + seq
    n_pad = (-t) % BLOCK
    tp = t + n_pad
    assert d == D_MODEL and (n_pad + N_META) % BLOCK == 0

    meta = jnp.broadcast_to(meta_tokens.astype(x.dtype)[None], (b, N_META, d))
    h = jnp.concatenate([jnp.zeros((b, n_pad, d), x.dtype), meta, x], axis=1)

    w_in_p, wkt = _pack_w_in(w_in)
    bias_a = _swa_bias(rel_bias_table)
    order = jnp.array(_slab_head_order())
    sinks = jnp.broadcast_to(swa_sinks.astype(F32)[:, :, None], (depth, SWA_Q_HEADS, LANES))
    fb = jnp.pad(fox_forget_bias.astype(F32), ((0, 0), (0, LANES - FOX_HEADS)))[:, None, :]
    wg = jnp.concatenate([_block_diag(lru_w_r), _block_diag(lru_w_i)], axis=-1).astype(BF16)
    bg = jnp.concatenate([lru_b_r, lru_b_i], axis=-1).astype(F32)[:, None, :]
    wb_a = w_branch[:, 0].reshape(depth, SWA_Q_HEADS, HEAD_DIM, d)[:, order].reshape(depth, LRU_WIDTH, d)
    wb = jnp.stack([wb_a, w_branch[:, 1], w_branch[:, 2]], axis=1).astype(BF16)
    wo = w_out.astype(BF16)
    w1 = w_ffn_in.astype(BF16)
    w2 = w_ffn_out.astype(BF16)

    for l in range(depth):
        qkva, qvf, kt, fl, xy, gt = _in_proj(h.reshape(b * tp, d), norm_mix[l][None, :], w_in_p[l], wkt[l])
        ck = _fox_prefix(fl.reshape(b, tp, LANES), fb[l], n_pad)
        o_a = _swa(qkva.reshape(b, tp, A_COLS), bias_a, sinks[l], n_pad)
        o_f = _fox(qvf.reshape(b, tp, F_COLS), kt, ck, n_pad)
        o_c = _lru(xy.reshape(b, tp, C_COLS), conv_w[l], conv_b[l][None, :], wg[l], bg[l], lru_lambda[l][None, :],
                   n_pad)
        h = _mix_ffn(h, o_a, o_f, o_c, gt.reshape(b, tp, G_COLS), wb[l], wo[l], norm_ffn[l][None, :], w1[l], w2[l],
                     n_pad)
    return _final_norm(h, norm_final[None, :], (n_pad + N_META) // BLOCK, seq)
```

```python
import functools
import math

import jax
import jax.numpy as jnp
from jax import lax
from jax.experimental import pallas as pl
from jax.experimental.pallas import tpu as pltpu

F32 = jnp.float32
BF16 = jnp.bfloat16

D_MODEL = 1024
HEAD_DIM = 64
N_META = 16
BLOCK = 128
LANES = 128
NEG_INF = -1e30
LOG2E = math.log2(math.e)
SWA_WINDOW = 128
SWA_Q_HEADS = 8
SWA_KV_HEADS = 2
FOX_HEADS = 8
LRU_WIDTH = D_MODEL // 2
LRU_BLOCKS = 8
CONV_WIDTH = 4
LRU_C = 8.0
REL_BUCKETS = 32
REL_MAX_DIST = 128
D_FF = 2816
N_BRANCH = 3
EPS = 1e-6

SWA_QCOLS = SWA_Q_HEADS * HEAD_DIM
A_COLS = SWA_QCOLS + SWA_KV_HEADS * HEAD_DIM
FOX_WIDTH = FOX_HEADS * HEAD_DIM
F_COLS = 2 * FOX_WIDTH
KT_ROWS = FOX_WIDTH + SWA_KV_HEADS * HEAD_DIM
C_COLS = 2 * LRU_WIDTH
G_COLS = N_BRANCH * D_MODEL
OFF_A = 0
OFF_F = OFF_A + A_COLS
OFF_FL = OFF_F + F_COLS
OFF_C = OFF_FL + LANES
OFF_G = OFF_C + C_COLS
IN_COLS_PACKED = OFF_G + G_COLS

VMEM_LIMIT = 56 * 1024 * 1024


def _sigmoid(x):
    return 1.0 / (1.0 + jnp.exp(-x))


def _log_sigmoid(x):
    return jnp.minimum(x, 0.0) - jnp.log1p(jnp.exp(-jnp.abs(x)))


def _gelu_tanh(x):
    c = math.sqrt(2.0 / math.pi)
    return 0.5 * x * (1.0 + jnp.tanh(c * (x + 0.044715 * (x * x * x))))


def _rms_norm(x, g):
    ms = jnp.mean(x * x, axis=-1, keepdims=True)
    return x * lax.rsqrt(ms + EPS) * g


def _dot(a, b):
    return jnp.dot(a, b, preferred_element_type=F32)


def _dot_nt(a, b):
    return lax.dot_general(a, b, (((1,), (1,)), ((), ())), preferred_element_type=F32)


def _resident(shape):
    nd = len(shape)
    return pl.BlockSpec(shape, lambda *_: (0,) * nd, pipeline_mode=pl.Buffered(1))


def _row_tile(tp):
    return 384 if tp % 384 == 0 else BLOCK


def _pv_operand(v, lo_k):
    zero = jnp.zeros_like(v)
    ind_lo = jnp.where(lo_k, 1.0, 0.0).astype(BF16)
    ind_hi = jnp.where(lo_k, 0.0, 1.0).astype(BF16)
    top = jnp.concatenate([jnp.where(lo_k, v, zero), ind_lo], axis=1)
    bot = jnp.concatenate([jnp.where(lo_k, zero, v), ind_hi], axis=1)
    return jnp.concatenate([top, bot], axis=0)


def _in_proj_kernel(x_ref, g_ref, w_ref, wkt_ref, qa_ref, qf_ref, kt_ref, fl_ref, xy_ref, gt_ref):
    u = _rms_norm(x_ref[...], g_ref[...]).astype(BF16)

    def proj(c0, width):
        return _dot(u, w_ref[:, c0:c0 + width])

    qa_ref[:, 0:SWA_QCOLS] = proj(OFF_A, SWA_QCOLS).astype(BF16)
    qa_ref[:, SWA_QCOLS:] = proj(OFF_A + SWA_QCOLS, A_COLS - SWA_QCOLS).astype(BF16)
    for c in range(0, F_COLS, 512):
        qf_ref[:, c:c + 512] = proj(OFF_F + c, 512).astype(BF16)
    kt_ref[...] = _dot_nt(wkt_ref[...], u).astype(BF16)
    fl_ref[...] = proj(OFF_FL, LANES)
    xy_ref[:, 0:LRU_WIDTH] = proj(OFF_C, LRU_WIDTH).astype(BF16)
    xy_ref[:, LRU_WIDTH:] = _gelu_tanh(proj(OFF_C + LRU_WIDTH, LRU_WIDTH)).astype(BF16)
    for c in range(0, G_COLS, 512):
        gt_ref[:, c:c + 512] = _sigmoid(proj(OFF_G + c, 512)).astype(BF16)


def _in_proj(h2, g, w, wkt):
    rows = h2.shape[0]
    tm = 384 if rows % 384 == 0 else BLOCK
    row_spec = lambda cols: pl.BlockSpec((tm, cols), lambda i: (i, 0))
    return pl.pallas_call(
        _in_proj_kernel,
        grid=(rows // tm,),
        in_specs=[row_spec(D_MODEL), _resident((1, D_MODEL)), _resident((D_MODEL, IN_COLS_PACKED)),
                  _resident((KT_ROWS, D_MODEL))],
        out_specs=[row_spec(A_COLS), row_spec(F_COLS), pl.BlockSpec((KT_ROWS, tm), lambda i: (0, i)),
                   row_spec(LANES), row_spec(C_COLS), row_spec(G_COLS)],
        out_shape=[
            jax.ShapeDtypeStruct((rows, A_COLS), BF16),
            jax.ShapeDtypeStruct((rows, F_COLS), BF16),
            jax.ShapeDtypeStruct((KT_ROWS, rows), BF16),
            jax.ShapeDtypeStruct((rows, LANES), F32),
            jax.ShapeDtypeStruct((rows, C_COLS), BF16),
            jax.ShapeDtypeStruct((rows, G_COLS), BF16),
        ],
        compiler_params=pltpu.CompilerParams(dimension_semantics=("parallel",), vmem_limit_bytes=VMEM_LIMIT),
        name="in_proj",
    )(h2, g, w, wkt)


def _fox_prefix_kernel(fl_ref, fb_ref, ck_ref, cum_ref, *, tp, n_pad):
    lf = _log_sigmoid(fl_ref[0] + fb_ref[...])
    row = lax.broadcasted_iota(jnp.int32, (tp, 1), 0)
    lf = jnp.where(row >= n_pad, lf, 0.0)
    r = lax.broadcasted_iota(jnp.int32, (BLOCK, BLOCK), 0)
    c = lax.broadcasted_iota(jnp.int32, (BLOCK, BLOCK), 1)
    tri = jnp.where(r >= c, 1.0, 0.0).astype(F32)
    carry = jnp.zeros((1, LANES), F32)
    for blk in range(tp // BLOCK):
        sl = slice(blk * BLOCK, (blk + 1) * BLOCK)
        cum = jnp.dot(tri, lf[sl], preferred_element_type=F32, precision=lax.Precision.HIGHEST) + carry
        cum_ref[sl, :] = cum
        carry = cum[BLOCK - 1:BLOCK, :]
    ck_ref[0] = cum_ref[...].T[0:FOX_HEADS, :]


def _fox_prefix(fl3, fb, n_pad):
    b, tp, _ = fl3.shape
    return pl.pallas_call(
        functools.partial(_fox_prefix_kernel, tp=tp, n_pad=n_pad),
        grid=(b,),
        in_specs=[pl.BlockSpec((1, tp, LANES), lambda i: (i, 0, 0)), _resident((1, LANES))],
        out_specs=pl.BlockSpec((1, FOX_HEADS, tp), lambda i: (i, 0, 0)),
        out_shape=jax.ShapeDtypeStruct((b, FOX_HEADS, tp), F32),
        scratch_shapes=[pltpu.VMEM((tp, LANES), F32)],
        compiler_params=pltpu.CompilerParams(dimension_semantics=("parallel",), vmem_limit_bytes=VMEM_LIMIT),
        name="fox_prefix",
    )(fl3, fb)


def _swa_kernel(q_ref, v_ref, vp_ref, kt_ref, ktp_ref, bias_ref, sink_ref, o_ref, *, blocks, n_pad):
    ib = pl.program_id(1)
    half = SWA_Q_HEADS // 2
    rows = half * BLOCK
    lo = lax.broadcasted_iota(jnp.int32, (BLOCK, LANES), 1) < HEAD_DIM
    lo_k = lax.broadcasted_iota(jnp.int32, (2 * BLOCK, LANES), 1) < HEAD_DIM
    lo_o = lax.broadcasted_iota(jnp.int32, (rows, LANES), 1) < HEAD_DIM
    kidx = lax.broadcasted_iota(jnp.int32, (1, BLOCK), 1)
    sink = sink_ref[...]
    for g in range(blocks):
        cur = slice(g * BLOCK, (g + 1) * BLOCK)
        if g == 0:
            kt_prev, v_prev = ktp_ref[...], vp_ref[0]
        else:
            prev = slice((g - 1) * BLOCK, g * BLOCK)
            kt_prev, v_prev = kt_ref[:, prev], v_ref[0, prev, :]
        kt_band = jnp.concatenate([kt_prev, kt_ref[:, cur]], axis=1)
        v_band = jnp.concatenate([v_prev, v_ref[0, cur, :]], axis=0)
        slabs = [q_ref[0, cur, j * LANES:(j + 1) * LANES] for j in range(half)]
        zero = jnp.zeros_like(slabs[0])
        q_all = jnp.concatenate([jnp.where(lo, s, zero) for s in slabs] + [jnp.where(lo, zero, s) for s in slabs],
                                axis=0)
        s = _dot(q_all, kt_band) + bias_ref[...]
        key0 = (ib * blocks + g - 1) * BLOCK + kidx
        s_prev = jnp.where(key0 >= n_pad, s[:, 0:BLOCK], NEG_INF)
        s_cur = jnp.where(key0 + BLOCK >= n_pad, s[:, BLOCK:], NEG_INF)
        m = jnp.maximum(jnp.max(jnp.maximum(s_prev, s_cur), axis=-1, keepdims=True), sink)
        p = jnp.concatenate([jnp.exp2(s_prev - m), jnp.exp2(s_cur - m)], axis=1).astype(BF16)
        pv = _dot(jnp.concatenate([p[0:rows], p[rows:]], axis=1), _pv_operand(v_band, lo_k))
        esink = jnp.exp2(sink - m)
        den = pv[:, LANES:] + jnp.where(lo_o, esink[0:rows], esink[rows:])
        o = (pv[:, 0:LANES] / den).astype(BF16)
        for j in range(half):
            o_ref[0, cur, j * LANES:(j + 1) * LANES] = o[j * BLOCK:(j + 1) * BLOCK]


def _swa(qv3, kt, bias, sinks, n_pad):
    b, tp, _ = qv3.shape
    nb = tp // BLOCK
    blocks = 3 if nb % 3 == 0 else 1
    rows = blocks * BLOCK
    steps = nb // blocks
    vblk = SWA_QCOLS // LANES
    krow = FOX_WIDTH // LANES
    return pl.pallas_call(
        functools.partial(_swa_kernel, blocks=blocks, n_pad=n_pad),
        grid=(b, steps),
        in_specs=[
            pl.BlockSpec((1, rows, SWA_QCOLS), lambda i, j: (i, j, 0)),
            pl.BlockSpec((1, rows, LANES), lambda i, j: (i, j, vblk)),
            pl.BlockSpec((1, BLOCK, LANES), lambda i, j: (i, jnp.maximum(j * blocks - 1, 0), vblk)),
            pl.BlockSpec((LANES, rows), lambda i, j: (krow, i * steps + j)),
            pl.BlockSpec((LANES, BLOCK), lambda i, j: (krow, jnp.maximum((i * steps + j) * blocks - 1, 0))),
            _resident((SWA_Q_HEADS * BLOCK, 2 * BLOCK)),
            _resident((SWA_Q_HEADS * BLOCK, LANES)),
        ],
        out_specs=pl.BlockSpec((1, rows, SWA_QCOLS), lambda i, j: (i, j, 0)),
        out_shape=jax.ShapeDtypeStruct((b, tp, SWA_QCOLS), BF16),
        compiler_params=pltpu.CompilerParams(dimension_semantics=("parallel", "parallel"), vmem_limit_bytes=VMEM_LIMIT),
        name="swa",
    )(qv3, qv3, qv3, kt, kt, bias, sinks)


FOX_TQ = 384
FOX_TK = 512
AUG_ROWS = 16


def _split3(c):
    hi = c.astype(BF16).astype(F32)
    mid = (c - hi).astype(BF16).astype(F32)
    return hi, mid, (c - hi) - mid


def _fox_kernel(q_ref, v_ref, kt_ref, ck_ref, o_ref, qa_s, m_s, acc_s, *, tq, tk, first, n_pad):
    i = pl.program_id(1)
    pairs = FOX_HEADS // 2
    lane = lax.broadcasted_iota(jnp.int32, (tq, LANES), 1)
    lo = lane < HEAD_DIM
    pats = [jnp.where((lane >= 3 * e) & (lane < 3 * e + 3), 1.0, 0.0).astype(BF16) for e in range(2)]
    for hp in range(pairs):
        q = q_ref[0, :, hp * LANES:(hp + 1) * LANES]
        zero = jnp.zeros_like(q)
        qa_s[hp, 0:tq, 0:LANES] = jnp.where(lo, q, zero)
        qa_s[hp, tq:, 0:LANES] = jnp.where(lo, zero, q)
        qa_s[hp, 0:tq, LANES:] = pats[0]
        qa_s[hp, tq:, LANES:] = pats[1]
    m_s[...] = jnp.full(m_s.shape, NEG_INF, F32)
    acc_s[...] = jnp.zeros(acc_s.shape, F32)

    def tile(start, size, masked):
        sub = lax.broadcasted_iota(jnp.int32, (AUG_ROWS, size), 0)
        lo_k = lax.broadcasted_iota(jnp.int32, (size, LANES), 1) < HEAD_DIM
        if masked:
            qpos = i * tq + lax.broadcasted_iota(jnp.int32, (tq, size), 0)
            kpos = start + lax.broadcasted_iota(jnp.int32, (tq, size), 1)
            ok = (kpos <= qpos) & (kpos >= n_pad)
        for hp in range(pairs):
            kt = kt_ref[hp * LANES:(hp + 1) * LANES, pl.ds(start, size)]
            v = v_ref[0, pl.ds(start, size), hp * LANES:(hp + 1) * LANES]
            aug = jnp.zeros((AUG_ROWS, size), F32)
            for e in range(2):
                pieces = _split3(ck_ref[0, 2 * hp + e:2 * hp + e + 1, pl.ds(start, size)] * (-LOG2E))
                for r, piece in enumerate(pieces):
                    aug = jnp.where(sub == 3 * e + r, piece, aug)
            kaug = jnp.concatenate([kt, aug.astype(BF16), jnp.zeros((LANES - AUG_ROWS, size), BF16)], axis=0)
            s2 = _dot(qa_s[hp], kaug)
            ps, alphas = [], []
            for e in range(2):
                s = s2[e * tq:(e + 1) * tq]
                if masked:
                    s = jnp.where(ok, s, NEG_INF)
                chunks = [s[:, c0:c0 + LANES] for c0 in range(0, size, LANES)]
                m_prev = m_s[2 * hp + e]
                m_new = jnp.maximum(m_prev, jnp.max(functools.reduce(jnp.maximum, chunks), axis=-1, keepdims=True))
                alphas.append(jnp.exp2(m_prev - m_new))
                ps.append(jnp.concatenate([jnp.exp2(ch - m_new) for ch in chunks], axis=1).astype(BF16))
                m_s[2 * hp + e] = m_new
            alpha = jnp.where(lo, alphas[0], alphas[1])
            alpha = jnp.concatenate([alpha, alpha], axis=1)
            acc_s[hp] = alpha * acc_s[hp] + _dot(jnp.concatenate(ps, axis=1), _pv_operand(v, lo_k))

    tile(0, first, True)
    n_full = jnp.maximum(tq * i - (first - 1), 0) // tk
    n_last = (tq * (i + 1) - first - 1) // tk

    def body(masked):
        def f(j, carry):
            tile(pl.multiple_of(first + j * tk, LANES), tk, masked)
            return carry
        return f

    lax.fori_loop(0, n_full, body(False), 0)
    lax.fori_loop(n_full, n_last + 1, body(True), 0)
    for hp in range(pairs):
        acc = acc_s[hp]
        o_ref[0, :, hp * LANES:(hp + 1) * LANES] = (acc[:, 0:LANES] / acc[:, LANES:]).astype(BF16)


def _fox(qv3, kt, ck, n_pad):
    b, tp, _ = qv3.shape
    tq, tk, first = FOX_TQ, FOX_TK, n_pad + N_META
    assert tp % tq == 0 and (tp - first) % tk == 0 and first % LANES == 0 and tq > first
    return pl.pallas_call(
        functools.partial(_fox_kernel, tq=tq, tk=tk, first=first, n_pad=n_pad),
        grid=(b, tp // tq),
        in_specs=[
            pl.BlockSpec((1, tq, FOX_WIDTH), lambda i, j: (i, j, 0)),
            pl.BlockSpec((1, tp, FOX_WIDTH), lambda i, j: (i, 0, 1)),
            pl.BlockSpec((FOX_WIDTH, tp), lambda i, j: (0, i)),
            pl.BlockSpec((1, FOX_HEADS, tp), lambda i, j: (i, 0, 0)),
        ],
        out_specs=pl.BlockSpec((1, tq, FOX_WIDTH), lambda i, j: (i, j, 0)),
        out_shape=jax.ShapeDtypeStruct((b, tp, FOX_WIDTH), BF16),
        scratch_shapes=[
            pltpu.VMEM((FOX_HEADS // 2, 2 * tq, 2 * LANES), BF16),
            pltpu.VMEM((FOX_HEADS, tq, LANES), F32),
            pltpu.VMEM((FOX_HEADS // 2, tq, 2 * LANES), F32),
        ],
        compiler_params=pltpu.CompilerParams(
            dimension_semantics=("parallel", "arbitrary"), vmem_limit_bytes=VMEM_LIMIT),
        name="fox",
    )(qv3, qv3, kt, ck)


SUBLANES = 8


def _lru_kernel(xy_ref, cw_ref, cb_ref, wg_ref, bg_ref, lam_ref, o_ref, xbuf, a_s, b_s, h_s, carry, *, tt, n_pad):
    it = pl.program_id(1)

    @pl.when(it == 0)
    def _():
        xbuf[0:SUBLANES, :] = jnp.zeros((SUBLANES, LRU_WIDTH), F32)
        carry[...] = jnp.zeros(carry.shape, F32)

    xbuf[SUBLANES:SUBLANES + tt, :] = xy_ref[0, :, 0:LRU_WIDTH].astype(F32)
    base = SUBLANES - (CONV_WIDTH - 1)
    conv = xbuf[base:base + tt, :] * cw_ref[0:1, :]
    for i in range(1, CONV_WIDTH):
        conv = conv + xbuf[base + i:base + i + tt, :] * cw_ref[i:i + 1, :]
    conv = conv + cb_ref[...]
    xbuf[0:SUBLANES, :] = xbuf[tt:tt + SUBLANES, :]

    z = _dot(conv.astype(BF16), wg_ref[...]) + bg_ref[...]
    r = _sigmoid(z[:, 0:LRU_WIDTH])
    gi = _sigmoid(z[:, LRU_WIDTH:])
    log_a = LRU_C * r * _log_sigmoid(lam_ref[...])
    row = it * tt + lax.broadcasted_iota(jnp.int32, (tt, 1), 0)
    a = jnp.exp(log_a)
    a_s[...] = a
    one_minus_a2 = -jnp.tanh(log_a) * (a * a + 1.0)
    b_s[...] = jnp.where(row >= n_pad, jnp.sqrt(one_minus_a2) * (gi * conv), 0.0)

    sub = lax.broadcasted_iota(jnp.int32, (SUBLANES, LRU_WIDTH), 0)

    def body(c, h_prev):
        r0 = pl.multiple_of(c * SUBLANES, SUBLANES)
        a = a_s[pl.ds(r0, SUBLANES), :]
        bb = b_s[pl.ds(r0, SUBLANES), :]
        for s in (1, 2, 4):
            keep = sub >= s
            bb = jnp.where(keep, a * pltpu.roll(bb, s, 0) + bb, bb)
            a = jnp.where(keep, a * pltpu.roll(a, s, 0), a)
        hc = a * h_prev + bb
        h_s[pl.ds(r0, SUBLANES), :] = hc
        return hc[SUBLANES - 1:SUBLANES, :]

    carry[...] = lax.fori_loop(0, tt // SUBLANES, body, carry[...])
    o_ref[0] = (h_s[...] * xy_ref[0, :, LRU_WIDTH:].astype(F32)).astype(BF16)


def _lru(xy3, cw, cb, wg, bg, lam, n_pad):
    b, tp, _ = xy3.shape
    tt = _row_tile(tp)
    w = LRU_WIDTH
    return pl.pallas_call(
        functools.partial(_lru_kernel, tt=tt, n_pad=n_pad),
        grid=(b, tp // tt),
        in_specs=[
            pl.BlockSpec((1, tt, C_COLS), lambda i, j: (i, j, 0)),
            _resident((CONV_WIDTH, w)), _resident((1, w)), _resident((w, 2 * w)), _resident((1, 2 * w)),
            _resident((1, w)),
        ],
        out_specs=pl.BlockSpec((1, tt, w), lambda i, j: (i, j, 0)),
        out_shape=jax.ShapeDtypeStruct((b, tp, w), BF16),
        scratch_shapes=[
            pltpu.VMEM((tt + SUBLANES, w), F32), pltpu.VMEM((tt, w), F32), pltpu.VMEM((tt, w), F32),
            pltpu.VMEM((tt, w), F32), pltpu.VMEM((1, w), F32),
        ],
        compiler_params=pltpu.CompilerParams(dimension_semantics=("parallel", "arbitrary"), vmem_limit_bytes=VMEM_LIMIT),
        name="lru",
    )(xy3, cw, cb, wg, bg, lam)


FF_CHUNK = 256


def _mix_ffn_kernel(h_ref, oa_ref, of_ref, oc_ref, gt_ref, wb_ref, wo_ref, gn_ref, w1_ref, w2_ref, out_ref, act_s,
                    *, tm, n_pad):
    j = pl.program_id(1)
    d = D_MODEL
    merged = gt_ref[0, :, 0:d].astype(F32) * _dot(oa_ref[0], wb_ref[0])
    merged = merged + gt_ref[0, :, d:2 * d].astype(F32) * _dot(of_ref[0], wb_ref[1])
    merged = merged + gt_ref[0, :, 2 * d:3 * d].astype(F32) * _dot(oc_ref[0], wb_ref[2])
    hm = h_ref[0] + _dot(merged.astype(BF16), wo_ref[...])
    u = _rms_norm(hm, gn_ref[...]).astype(BF16)
    for c in range(0, D_FF, FF_CHUNK):
        gate = _dot(u, w1_ref[:, c:c + FF_CHUNK])
        up = _dot(u, w1_ref[:, D_FF + c:D_FF + c + FF_CHUNK])
        act_s[:, c:c + FF_CHUNK] = (gate * _sigmoid(gate) * up).astype(BF16)
    y = hm + _dot(act_s[...], w2_ref[...])
    row = j * tm + lax.broadcasted_iota(jnp.int32, (tm, 1), 0)
    out_ref[0] = jnp.where(row >= n_pad, y, 0.0)


def _mix_ffn(h3, oa, of, oc, gt3, wb, wo, gn, w1, w2, n_pad):
    b, tp, d = h3.shape
    tm = _row_tile(tp)
    w = LRU_WIDTH
    tile = lambda cols: pl.BlockSpec((1, tm, cols), lambda i, j: (i, j, 0))
    return pl.pallas_call(
        functools.partial(_mix_ffn_kernel, tm=tm, n_pad=n_pad),
        grid=(b, tp // tm),
        in_specs=[
            tile(d), tile(w), tile(w), tile(w), tile(G_COLS),
            _resident((N_BRANCH, w, d)), _resident((d, d)), _resident((1, d)),
            _resident((d, 2 * D_FF)), _resident((D_FF, d)),
        ],
        out_specs=tile(d),
        out_shape=jax.ShapeDtypeStruct((b, tp, d), F32),
        scratch_shapes=[pltpu.VMEM((tm, D_FF), BF16)],
        compiler_params=pltpu.CompilerParams(dimension_semantics=("parallel", "parallel"), vmem_limit_bytes=VMEM_LIMIT),
        name="mix_ffn",
    )(h3, oa, of, oc, gt3, wb, wo, gn, w1, w2)


def _final_norm_kernel(h_ref, g_ref, o_ref):
    o_ref[0] = _rms_norm(h_ref[0], g_ref[...])


def _final_norm(h3, g, skip_blocks, seq):
    b, tp, d = h3.shape
    return pl.pallas_call(
        _final_norm_kernel,
        grid=(b, seq // BLOCK),
        in_specs=[pl.BlockSpec((1, BLOCK, d), lambda i, j: (i, j + skip_blocks, 0)), _resident((1, d))],
        out_specs=pl.BlockSpec((1, BLOCK, d), lambda i, j: (i, j, 0)),
        out_shape=jax.ShapeDtypeStruct((b, seq, d), F32),
        compiler_params=pltpu.CompilerParams(dimension_semantics=("parallel", "parallel"), vmem_limit_bytes=VMEM_LIMIT),
        name="final_norm",
    )(h3, g)


def _t5_bucket(dist):
    max_exact = REL_BUCKETS // 2
    d = jnp.maximum(dist, 0)
    scaled = jnp.log(jnp.maximum(d, 1).astype(F32) / max_exact) / math.log(REL_MAX_DIST / max_exact)
    large = jnp.minimum(max_exact + (scaled * (REL_BUCKETS - max_exact)).astype(jnp.int32), REL_BUCKETS - 1)
    return jnp.where(d < max_exact, d, large)


def _swa_bias(rel_table):
    q_idx = jnp.arange(BLOCK)[:, None]
    k_idx = jnp.arange(2 * BLOCK)[None, :]
    dist = q_idx + BLOCK - k_idx
    bias = rel_table.astype(F32)[_t5_bucket(dist)].transpose(2, 0, 1) * LOG2E
    ok = (dist >= 0) & (dist < SWA_WINDOW)
    return jnp.where(ok[None], bias, NEG_INF).reshape(SWA_Q_HEADS * BLOCK, 2 * BLOCK)


def _slab_head_order():
    half = SWA_Q_HEADS // 2
    return [h for j in range(half) for h in (j, j + half)]


def _pack_w_in(w_in):
    depth = w_in.shape[0]
    sizes = (512, 128, 128, 512, 512, 512, 8, 512, 512, 3072)
    offs = [0]
    for s in sizes:
        offs.append(offs[-1] + s)
    qa, ka, va, qf, kf, vf, fl, xc, yc, gates = [w_in[:, :, offs[i]:offs[i + 1]] for i in range(len(sizes))]
    scale = HEAD_DIM ** -0.5 * LOG2E
    qa = qa.reshape(depth, D_MODEL, SWA_Q_HEADS, HEAD_DIM)[:, :, jnp.array(_slab_head_order())]
    qa = qa.reshape(depth, D_MODEL, SWA_QCOLS) * scale
    fl = jnp.pad(fl, ((0, 0), (0, 0), (0, LANES - FOX_HEADS)))
    packed = jnp.concatenate([qa, va, qf * scale, vf, fl, xc, yc, gates], axis=-1)
    keys_t = jnp.swapaxes(jnp.concatenate([kf, ka], axis=-1), 1, 2)
    return packed.astype(BF16), keys_t.astype(BF16)


def _block_diag(w):
    depth, nb, n, _ = w.shape
    eye = jnp.eye(nb, dtype=w.dtype)
    return jnp.einsum('lhij,hk->lhikj', w, eye).reshape(depth, nb * n, nb * n)


def kernel(x, meta_tokens, rel_bias_table, norm_mix, w_in, swa_sinks, fox_forget_bias, conv_w, conv_b,
           lru_w_r, lru_b_r, lru_w_i, lru_b_i, lru_lambda, w_branch, w_out, norm_ffn, w_ffn_in, w_ffn_out,
           norm_final):
    b, seq, d = x.shape
    depth = w_in.shape[0]
    t = N_META + seq
    n_pad = (-t) % BLOCK
    tp = t + n_pad
    assert d == D_MODEL and (n_pad + N_META) % BLOCK == 0

    meta = jnp.broadcast_to(meta_tokens.astype(x.dtype)[None], (b, N_META, d))
    h = jnp.concatenate([jnp.zeros((b, n_pad, d), x.dtype), meta, x], axis=1)

    w_in_p, wkt = _pack_w_in(w_in)
    bias_a = _swa_bias(rel_bias_table)
    order = jnp.array(_slab_head_order())
    sinks = jnp.broadcast_to((swa_sinks.astype(F32) * LOG2E)[:, :, None, None], (depth, SWA_Q_HEADS, BLOCK, LANES))
    sinks = sinks.reshape(depth, SWA_Q_HEADS * BLOCK, LANES)
    fb = jnp.pad(fox_forget_bias.astype(F32), ((0, 0), (0, LANES - FOX_HEADS)))[:, None, :]
    wg = jnp.concatenate([_block_diag(lru_w_r), _block_diag(lru_w_i)], axis=-1).astype(BF16)
    bg = jnp.concatenate([lru_b_r, lru_b_i], axis=-1).astype(F32)[:, None, :]
    wb_a = w_branch[:, 0].reshape(depth, SWA_Q_HEADS, HEAD_DIM, d)[:, order].reshape(depth, LRU_WIDTH, d)
    wb = jnp.stack([wb_a, w_branch[:, 1], w_branch[:, 2]], axis=1).astype(BF16)
    wo = w_out.astype(BF16)
    w1 = w_ffn_in.astype(BF16)
    w2 = w_ffn_out.astype(BF16)

    for l in range(depth):
        qva, qvf, kt, fl, xy, gt = _in_proj(h.reshape(b * tp, d), norm_mix[l][None, :], w_in_p[l], wkt[l])
        ck = _fox_prefix(fl.reshape(b, tp, LANES), fb[l], n_pad)
        o_a = _swa(qva.reshape(b, tp, A_COLS), kt, bias_a, sinks[l], n_pad)
        o_f = _fox(qvf.reshape(b, tp, F_COLS), kt, ck, n_pad)
        o_c = _lru(xy.reshape(b, tp, C_COLS), conv_w[l], conv_b[l][None, :], wg[l], bg[l], lru_lambda[l][None, :],
                   n_pad)
        h = _mix_ffn(h, o_a, o_f, o_c, gt.reshape(b, tp, G_COLS), wb[l], wo[l], norm_ffn[l][None, :], w1[l], w2[l],
                     n_pad)
    return _final_norm(h, norm_final[None, :], (n_pad + N_META) // BLOCK, seq)
```

```python
import functools
import math

import jax
import jax.numpy as jnp
from jax import lax
from jax.experimental import pallas as pl
from jax.experimental.pallas import tpu as pltpu

F32 = jnp.float32
BF16 = jnp.bfloat16

D_MODEL = 1024
HEAD_DIM = 64
N_META = 16
BLOCK = 128
LANES = 128
NEG_INF = -1e30
LOG2E = math.log2(math.e)
SWA_WINDOW = 128
SWA_Q_HEADS = 8
SWA_KV_HEADS = 2
FOX_HEADS = 8
LRU_WIDTH = D_MODEL // 2
LRU_BLOCKS = 8
CONV_WIDTH = 4
LRU_C = 8.0
REL_BUCKETS = 32
REL_MAX_DIST = 128
D_FF = 2816
N_BRANCH = 3
EPS = 1e-6

SWA_QCOLS = SWA_Q_HEADS * HEAD_DIM
A_COLS = SWA_QCOLS + SWA_KV_HEADS * HEAD_DIM
FOX_WIDTH = FOX_HEADS * HEAD_DIM
F_COLS = 2 * FOX_WIDTH
KT_ROWS = FOX_WIDTH + SWA_KV_HEADS * HEAD_DIM
C_COLS = 2 * LRU_WIDTH
G_COLS = N_BRANCH * D_MODEL
OFF_A = 0
OFF_F = OFF_A + A_COLS
OFF_FL = OFF_F + F_COLS
OFF_C = OFF_FL + LANES
OFF_G = OFF_C + C_COLS
IN_COLS_PACKED = OFF_G + G_COLS

VMEM_LIMIT = 56 * 1024 * 1024


def _sigmoid(x):
    return 0.5 * jnp.tanh(0.5 * x) + 0.5


def _log_sigmoid(x):
    return jnp.minimum(x, 0.0) - jnp.log1p(jnp.exp(-jnp.abs(x)))


def _gelu_tanh(x):
    c = math.sqrt(2.0 / math.pi)
    return 0.5 * x * (1.0 + jnp.tanh(c * (x + 0.044715 * (x * x * x))))


def _rms_norm(x, g):
    ms = jnp.mean(x * x, axis=-1, keepdims=True)
    return x * lax.rsqrt(ms + EPS) * g


def _dot(a, b):
    return jnp.dot(a, b, preferred_element_type=F32)


def _dot_nt(a, b):
    return lax.dot_general(a, b, (((1,), (1,)), ((), ())), preferred_element_type=F32)


def _resident(shape):
    nd = len(shape)
    return pl.BlockSpec(shape, lambda *_: (0,) * nd, pipeline_mode=pl.Buffered(1))


def _row_tile(tp):
    return 384 if tp % 384 == 0 else BLOCK


def _pv_operand(v, lo_k):
    zero = jnp.zeros_like(v)
    ind_lo = jnp.where(lo_k, 1.0, 0.0).astype(BF16)
    ind_hi = jnp.where(lo_k, 0.0, 1.0).astype(BF16)
    top = jnp.concatenate([jnp.where(lo_k, v, zero), ind_lo], axis=1)
    bot = jnp.concatenate([jnp.where(lo_k, zero, v), ind_hi], axis=1)
    return jnp.concatenate([top, bot], axis=0)


def _in_proj_kernel(x_ref, g_ref, w_ref, wkt_ref, qa_ref, qf_ref, kt_ref, fl_ref, xy_ref, gt_ref):
    u = _rms_norm(x_ref[...], g_ref[...]).astype(BF16)

    def proj(c0, width):
        return _dot(u, w_ref[:, c0:c0 + width])

    qa_ref[:, 0:SWA_QCOLS] = proj(OFF_A, SWA_QCOLS).astype(BF16)
    qa_ref[:, SWA_QCOLS:] = proj(OFF_A + SWA_QCOLS, A_COLS - SWA_QCOLS).astype(BF16)
    for c in range(0, F_COLS, 512):
        qf_ref[:, c:c + 512] = proj(OFF_F + c, 512).astype(BF16)
    kt_ref[...] = _dot_nt(wkt_ref[...], u).astype(BF16)
    fl_ref[...] = proj(OFF_FL, LANES)
    xy_ref[:, 0:LRU_WIDTH] = proj(OFF_C, LRU_WIDTH).astype(BF16)
    xy_ref[:, LRU_WIDTH:] = _gelu_tanh(proj(OFF_C + LRU_WIDTH, LRU_WIDTH)).astype(BF16)
    for c in range(0, G_COLS, 512):
        gt_ref[:, c:c + 512] = _sigmoid(proj(OFF_G + c, 512)).astype(BF16)


def _in_proj(h2, g, w, wkt):
    rows = h2.shape[0]
    tm = 384 if rows % 384 == 0 else BLOCK
    row_spec = lambda cols: pl.BlockSpec((tm, cols), lambda i: (i, 0))
    return pl.pallas_call(
        _in_proj_kernel,
        grid=(rows // tm,),
        in_specs=[row_spec(D_MODEL), _resident((1, D_MODEL)), _resident((D_MODEL, IN_COLS_PACKED)),
                  _resident((KT_ROWS, D_MODEL))],
        out_specs=[row_spec(A_COLS), row_spec(F_COLS), pl.BlockSpec((KT_ROWS, tm), lambda i: (0, i)),
                   row_spec(LANES), row_spec(C_COLS), row_spec(G_COLS)],
        out_shape=[
            jax.ShapeDtypeStruct((rows, A_COLS), BF16),
            jax.ShapeDtypeStruct((rows, F_COLS), BF16),
            jax.ShapeDtypeStruct((KT_ROWS, rows), BF16),
            jax.ShapeDtypeStruct((rows, LANES), F32),
            jax.ShapeDtypeStruct((rows, C_COLS), BF16),
            jax.ShapeDtypeStruct((rows, G_COLS), BF16),
        ],
        compiler_params=pltpu.CompilerParams(dimension_semantics=("parallel",), vmem_limit_bytes=VMEM_LIMIT),
        name="in_proj",
    )(h2, g, w, wkt)


def _fox_prefix_kernel(fl_ref, fb_ref, ck_ref, cum_ref, *, tp, n_pad):
    lf = _log_sigmoid(fl_ref[0] + fb_ref[...])
    row = lax.broadcasted_iota(jnp.int32, (tp, 1), 0)
    lf = jnp.where(row >= n_pad, lf, 0.0)
    r = lax.broadcasted_iota(jnp.int32, (BLOCK, BLOCK), 0)
    c = lax.broadcasted_iota(jnp.int32, (BLOCK, BLOCK), 1)
    tri = jnp.where(r >= c, 1.0, 0.0).astype(F32)
    carry = jnp.zeros((1, LANES), F32)
    for blk in range(tp // BLOCK):
        sl = slice(blk * BLOCK, (blk + 1) * BLOCK)
        cum = jnp.dot(tri, lf[sl], preferred_element_type=F32, precision=lax.Precision.HIGHEST) + carry
        cum_ref[sl, :] = cum
        carry = cum[BLOCK - 1:BLOCK, :]
    ck_ref[0] = cum_ref[...].T[0:FOX_HEADS, :]


def _fox_prefix(fl3, fb, n_pad):
    b, tp, _ = fl3.shape
    return pl.pallas_call(
        functools.partial(_fox_prefix_kernel, tp=tp, n_pad=n_pad),
        grid=(b,),
        in_specs=[pl.BlockSpec((1, tp, LANES), lambda i: (i, 0, 0)), _resident((1, LANES))],
        out_specs=pl.BlockSpec((1, FOX_HEADS, tp), lambda i: (i, 0, 0)),
        out_shape=jax.ShapeDtypeStruct((b, FOX_HEADS, tp), F32),
        scratch_shapes=[pltpu.VMEM((tp, LANES), F32)],
        compiler_params=pltpu.CompilerParams(dimension_semantics=("parallel",), vmem_limit_bytes=VMEM_LIMIT),
        name="fox_prefix",
    )(fl3, fb)


def _swa_kernel(q_ref, v_ref, vp_ref, kt_ref, ktp_ref, bias_ref, sink_ref, o_ref, *, blocks, n_pad):
    ib = pl.program_id(1)
    half = SWA_Q_HEADS // 2
    rows = half * BLOCK
    lo = lax.broadcasted_iota(jnp.int32, (BLOCK, LANES), 1) < HEAD_DIM
    lo_k = lax.broadcasted_iota(jnp.int32, (2 * BLOCK, LANES), 1) < HEAD_DIM
    lo_o = lax.broadcasted_iota(jnp.int32, (rows, LANES), 1) < HEAD_DIM
    kidx = lax.broadcasted_iota(jnp.int32, (1, BLOCK), 1)
    sink = sink_ref[...]
    for g in range(blocks):
        cur = slice(g * BLOCK, (g + 1) * BLOCK)
        if g == 0:
            kt_prev, v_prev = ktp_ref[...], vp_ref[0]
        else:
            prev = slice((g - 1) * BLOCK, g * BLOCK)
            kt_prev, v_prev = kt_ref[:, prev], v_ref[0, prev, :]
        kt_band = jnp.concatenate([kt_prev, kt_ref[:, cur]], axis=1)
        v_band = jnp.concatenate([v_prev, v_ref[0, cur, :]], axis=0)
        slabs = [q_ref[0, cur, j * LANES:(j + 1) * LANES] for j in range(half)]
        zero = jnp.zeros_like(slabs[0])
        q_all = jnp.concatenate([jnp.where(lo, s, zero) for s in slabs] + [jnp.where(lo, zero, s) for s in slabs],
                                axis=0)
        s = _dot(q_all, kt_band) + bias_ref[...]
        key0 = (ib * blocks + g - 1) * BLOCK + kidx
        s_prev = jnp.where(key0 >= n_pad, s[:, 0:BLOCK], NEG_INF)
        s_cur = jnp.where(key0 + BLOCK >= n_pad, s[:, BLOCK:], NEG_INF)
        m = jnp.maximum(jnp.max(jnp.maximum(s_prev, s_cur), axis=-1, keepdims=True), sink)
        p = jnp.concatenate([jnp.exp2(s_prev - m), jnp.exp2(s_cur - m)], axis=1).astype(BF16)
        pv = _dot(jnp.concatenate([p[0:rows], p[rows:]], axis=1), _pv_operand(v_band, lo_k))
        esink = jnp.exp2(sink - m)
        den = pv[:, LANES:] + jnp.where(lo_o, esink[0:rows], esink[rows:])
        o = (pv[:, 0:LANES] / den).astype(BF16)
        for j in range(half):
            o_ref[0, cur, j * LANES:(j + 1) * LANES] = o[j * BLOCK:(j + 1) * BLOCK]


def _swa(qv3, kt, bias, sinks, n_pad):
    b, tp, _ = qv3.shape
    nb = tp // BLOCK
    blocks = 3 if nb % 3 == 0 else 1
    rows = blocks * BLOCK
    steps = nb // blocks
    vblk = SWA_QCOLS // LANES
    krow = FOX_WIDTH // LANES
    return pl.pallas_call(
        functools.partial(_swa_kernel, blocks=blocks, n_pad=n_pad),
        grid=(b, steps),
        in_specs=[
            pl.BlockSpec((1, rows, SWA_QCOLS), lambda i, j: (i, j, 0)),
            pl.BlockSpec((1, rows, LANES), lambda i, j: (i, j, vblk)),
            pl.BlockSpec((1, BLOCK, LANES), lambda i, j: (i, jnp.maximum(j * blocks - 1, 0), vblk)),
            pl.BlockSpec((LANES, rows), lambda i, j: (krow, i * steps + j)),
            pl.BlockSpec((LANES, BLOCK), lambda i, j: (krow, jnp.maximum((i * steps + j) * blocks - 1, 0))),
            _resident((SWA_Q_HEADS * BLOCK, 2 * BLOCK)),
            _resident((SWA_Q_HEADS * BLOCK, LANES)),
        ],
        out_specs=pl.BlockSpec((1, rows, SWA_QCOLS), lambda i, j: (i, j, 0)),
        out_shape=jax.ShapeDtypeStruct((b, tp, SWA_QCOLS), BF16),
        compiler_params=pltpu.CompilerParams(dimension_semantics=("parallel", "parallel"), vmem_limit_bytes=VMEM_LIMIT),
        name="swa",
    )(qv3, qv3, qv3, kt, kt, bias, sinks)


FOX_TQ = 384
FOX_TK = 512
AUG_ROWS = 16


def _split3(c):
    hi = c.astype(BF16).astype(F32)
    mid = (c - hi).astype(BF16).astype(F32)
    return hi, mid, (c - hi) - mid


def _fox_kernel(q_ref, v_ref, kt_ref, ck_ref, o_ref, qa_s, m_s, acc_s, *, tq, tk, n_tiles, tail, n_pad):
    i = pl.program_id(1)
    pairs = FOX_HEADS // 2
    lane = lax.broadcasted_iota(jnp.int32, (tq, LANES), 1)
    lo = lane < HEAD_DIM
    pats = [jnp.where((lane >= 3 * e) & (lane < 3 * e + 3), 1.0, 0.0).astype(BF16) for e in range(2)]
    for hp in range(pairs):
        q = q_ref[0, :, hp * LANES:(hp + 1) * LANES]
        zero = jnp.zeros_like(q)
        qa_s[hp, 0:tq, 0:LANES] = jnp.where(lo, q, zero)
        qa_s[hp, tq:, 0:LANES] = jnp.where(lo, zero, q)
        qa_s[hp, 0:tq, LANES:] = pats[0]
        qa_s[hp, tq:, LANES:] = pats[1]
    m_s[...] = jnp.full(m_s.shape, NEG_INF, F32)
    acc_s[...] = jnp.zeros(acc_s.shape, F32)

    def tile(start, size, masked):
        sub = lax.broadcasted_iota(jnp.int32, (AUG_ROWS, size), 0)
        key_ok = start + lax.broadcasted_iota(jnp.int32, (1, size), 1) >= n_pad
        lo_k = lax.broadcasted_iota(jnp.int32, (size, LANES), 1) < HEAD_DIM
        if masked:
            rel = lax.broadcasted_iota(jnp.int32, (tq, size), 1) - lax.broadcasted_iota(jnp.int32, (tq, size), 0)
            ok = rel <= i * tq - start
        for hp in range(pairs):
            kt = kt_ref[hp * LANES:(hp + 1) * LANES, pl.ds(start, size)]
            v = v_ref[0, pl.ds(start, size), hp * LANES:(hp + 1) * LANES]
            aug = jnp.zeros((AUG_ROWS, size), F32)
            for e in range(2):
                c = jnp.where(key_ok, ck_ref[0, 2 * hp + e:2 * hp + e + 1, pl.ds(start, size)] * (-LOG2E), NEG_INF)
                for r, piece in enumerate(_split3(c)):
                    aug = jnp.where(sub == 3 * e + r, piece, aug)
            kaug = jnp.concatenate([kt, aug.astype(BF16), jnp.zeros((LANES - AUG_ROWS, size), BF16)], axis=0)
            s2 = _dot(qa_s[hp], kaug)
            ps, alphas = [], []
            for e in range(2):
                s = s2[e * tq:(e + 1) * tq]
                if masked:
                    s = jnp.where(ok, s, NEG_INF)
                chunks = [s[:, c0:c0 + LANES] for c0 in range(0, size, LANES)]
                m_prev = m_s[2 * hp + e]
                m_new = jnp.maximum(m_prev, jnp.max(functools.reduce(jnp.maximum, chunks), axis=-1, keepdims=True))
                alphas.append(jnp.exp2(m_prev - m_new))
                ps.append(jnp.concatenate([jnp.exp2(ch - m_new) for ch in chunks], axis=1).astype(BF16))
                m_s[2 * hp + e] = m_new
            alpha = jnp.where(lo, alphas[0], alphas[1])
            alpha = jnp.concatenate([alpha, alpha], axis=1)
            acc_s[hp] = alpha * acc_s[hp] + _dot(jnp.concatenate(ps, axis=1), _pv_operand(v, lo_k))

    n_last = jnp.minimum((tq * (i + 1) - 1) // tk, n_tiles - 1)
    n_free = (tq * i + 1) // tk

    def full_tile(j, masked):
        tile(pl.multiple_of(j * tk, tk), tk, masked)

    def two_free(j2, carry):
        full_tile(2 * j2, False)
        full_tile(2 * j2 + 1, False)
        return carry

    def one_masked(j, carry):
        full_tile(j, True)
        return carry

    lax.fori_loop(0, n_free // 2, two_free, 0)
    pl.when(n_free % 2 == 1)(lambda: full_tile(n_free - 1, False))
    lax.fori_loop(n_free, n_last + 1, one_masked, 0)
    if tail:
        pl.when(tq * (i + 1) > n_tiles * tk)(lambda: tile(n_tiles * tk, tail, True))

    for hp in range(pairs):
        acc = acc_s[hp]
        o_ref[0, :, hp * LANES:(hp + 1) * LANES] = (acc[:, 0:LANES] / acc[:, LANES:]).astype(BF16)


def _fox(qv3, kt, ck, n_pad):
    b, tp, _ = qv3.shape
    tq, tk = FOX_TQ, FOX_TK
    n_tiles, tail = tp // tk, tp % tk
    assert tp % tq == 0 and tail % LANES == 0 and tail <= tq and n_tiles >= 1
    pairs = FOX_HEADS // 2
    return pl.pallas_call(
        functools.partial(_fox_kernel, tq=tq, tk=tk, n_tiles=n_tiles, tail=tail, n_pad=n_pad),
        grid=(b, tp // tq),
        in_specs=[
            pl.BlockSpec((1, tq, FOX_WIDTH), lambda i, j: (i, j, 0)),
            pl.BlockSpec((1, tp, FOX_WIDTH), lambda i, j: (i, 0, 1)),
            pl.BlockSpec((FOX_WIDTH, tp), lambda i, j: (0, i)),
            pl.BlockSpec((1, FOX_HEADS, tp), lambda i, j: (i, 0, 0)),
        ],
        out_specs=pl.BlockSpec((1, tq, FOX_WIDTH), lambda i, j: (i, j, 0)),
        out_shape=jax.ShapeDtypeStruct((b, tp, FOX_WIDTH), BF16),
        scratch_shapes=[
            pltpu.VMEM((pairs, 2 * tq, 2 * LANES), BF16),
            pltpu.VMEM((FOX_HEADS, tq, LANES), F32),
            pltpu.VMEM((pairs, tq, 2 * LANES), F32),
        ],
        compiler_params=pltpu.CompilerParams(
            dimension_semantics=("parallel", "arbitrary"), vmem_limit_bytes=VMEM_LIMIT),
        name="fox",
    )(qv3, qv3, kt, ck)


SUBLANES = 8


def _lru_kernel(xy_ref, cw_ref, cb_ref, wg_ref, bg_ref, lam_ref, o_ref, xbuf, a_s, b_s, h_s, carry, *, tt, n_pad):
    it = pl.program_id(1)

    @pl.when(it == 0)
    def _():
        xbuf[0:SUBLANES, :] = jnp.zeros((SUBLANES, LRU_WIDTH), F32)
        carry[...] = jnp.zeros(carry.shape, F32)

    xbuf[SUBLANES:SUBLANES + tt, :] = xy_ref[0, :, 0:LRU_WIDTH].astype(F32)
    base = SUBLANES - (CONV_WIDTH - 1)
    conv = xbuf[base:base + tt, :] * cw_ref[0:1, :]
    for i in range(1, CONV_WIDTH):
        conv = conv + xbuf[base + i:base + i + tt, :] * cw_ref[i:i + 1, :]
    conv = conv + cb_ref[...]
    xbuf[0:SUBLANES, :] = xbuf[tt:tt + SUBLANES, :]

    z = _dot(conv.astype(BF16), wg_ref[...]) + bg_ref[...]
    r = _sigmoid(z[:, 0:LRU_WIDTH])
    gi = _sigmoid(z[:, LRU_WIDTH:])
    log_a = LRU_C * r * _log_sigmoid(lam_ref[...])
    row = it * tt + lax.broadcasted_iota(jnp.int32, (tt, 1), 0)
    a = jnp.exp(log_a)
    a_s[...] = a
    one_minus_a2 = -jnp.tanh(log_a) * (a * a + 1.0)
    b_s[...] = jnp.where(row >= n_pad, jnp.sqrt(one_minus_a2) * (gi * conv), 0.0)

    sub = lax.broadcasted_iota(jnp.int32, (SUBLANES, LRU_WIDTH), 0)

    def body(c, h_prev):
        r0 = pl.multiple_of(c * SUBLANES, SUBLANES)
        a = a_s[pl.ds(r0, SUBLANES), :]
        bb = b_s[pl.ds(r0, SUBLANES), :]
        for s in (1, 2, 4):
            keep = sub >= s
            bb = jnp.where(keep, a * pltpu.roll(bb, s, 0) + bb, bb)
            a = jnp.where(keep, a * pltpu.roll(a, s, 0), a)
        hc = a * h_prev + bb
        h_s[pl.ds(r0, SUBLANES), :] = hc
        return hc[SUBLANES - 1:SUBLANES, :]

    carry[...] = lax.fori_loop(0, tt // SUBLANES, body, carry[...])
    o_ref[0] = (h_s[...] * xy_ref[0, :, LRU_WIDTH:].astype(F32)).astype(BF16)


def _lru(xy3, cw, cb, wg, bg, lam, n_pad):
    b, tp, _ = xy3.shape
    tt = _row_tile(tp)
    w = LRU_WIDTH
    return pl.pallas_call(
        functools.partial(_lru_kernel, tt=tt, n_pad=n_pad),
        grid=(b, tp // tt),
        in_specs=[
            pl.BlockSpec((1, tt, C_COLS), lambda i, j: (i, j, 0)),
            _resident((CONV_WIDTH, w)), _resident((1, w)), _resident((w, 2 * w)), _resident((1, 2 * w)),
            _resident((1, w)),
        ],
        out_specs=pl.BlockSpec((1, tt, w), lambda i, j: (i, j, 0)),
        out_shape=jax.ShapeDtypeStruct((b, tp, w), BF16),
        scratch_shapes=[
            pltpu.VMEM((tt + SUBLANES, w), F32), pltpu.VMEM((tt, w), F32), pltpu.VMEM((tt, w), F32),
            pltpu.VMEM((tt, w), F32), pltpu.VMEM((1, w), F32),
        ],
        compiler_params=pltpu.CompilerParams(dimension_semantics=("parallel", "arbitrary"), vmem_limit_bytes=VMEM_LIMIT),
        name="lru",
    )(xy3, cw, cb, wg, bg, lam)


FF_CHUNK = 256


def _mix_ffn_kernel(h_ref, oa_ref, of_ref, oc_ref, gt_ref, wb_ref, wo_ref, gn_ref, w1_ref, w2_ref, out_ref, act_s,
                    *, tm, n_pad):
    j = pl.program_id(1)
    d = D_MODEL
    merged = gt_ref[0, :, 0:d].astype(F32) * _dot(oa_ref[0], wb_ref[0])
    merged = merged + gt_ref[0, :, d:2 * d].astype(F32) * _dot(of_ref[0], wb_ref[1])
    merged = merged + gt_ref[0, :, 2 * d:3 * d].astype(F32) * _dot(oc_ref[0], wb_ref[2])
    hm = h_ref[0] + _dot(merged.astype(BF16), wo_ref[...])
    u = _rms_norm(hm, gn_ref[...]).astype(BF16)
    for c in range(0, D_FF, FF_CHUNK):
        gate = _dot(u, w1_ref[:, c:c + FF_CHUNK])
        up = _dot(u, w1_ref[:, D_FF + c:D_FF + c + FF_CHUNK])
        act_s[:, c:c + FF_CHUNK] = (gate * _sigmoid(gate) * up).astype(BF16)
    y = hm + _dot(act_s[...], w2_ref[...])
    row = j * tm + lax.broadcasted_iota(jnp.int32, (tm, 1), 0)
    out_ref[0] = jnp.where(row >= n_pad, y, 0.0)


def _mix_ffn(h3, oa, of, oc, gt3, wb, wo, gn, w1, w2, n_pad):
    b, tp, d = h3.shape
    tm = _row_tile(tp)
    w = LRU_WIDTH
    tile = lambda cols: pl.BlockSpec((1, tm, cols), lambda i, j: (i, j, 0))
    return pl.pallas_call(
        functools.partial(_mix_ffn_kernel, tm=tm, n_pad=n_pad),
        grid=(b, tp // tm),
        in_specs=[
            tile(d), tile(w), tile(w), tile(w), tile(G_COLS),
            _resident((N_BRANCH, w, d)), _resident((d, d)), _resident((1, d)),
            _resident((d, 2 * D_FF)), _resident((D_FF, d)),
        ],
        out_specs=tile(d),
        out_shape=jax.ShapeDtypeStruct((b, tp, d), F32),
        scratch_shapes=[pltpu.VMEM((tm, D_FF), BF16)],
        compiler_params=pltpu.CompilerParams(dimension_semantics=("parallel", "parallel"), vmem_limit_bytes=VMEM_LIMIT),
        name="mix_ffn",
    )(h3, oa, of, oc, gt3, wb, wo, gn, w1, w2)


def _final_norm_kernel(h_ref, g_ref, o_ref):
    o_ref[0] = _rms_norm(h_ref[...], g_ref[...])


def _final_norm(h3, g, skip_rows, seq):
    b, tp, d = h3.shape
    tr = 512 if seq % 512 == 0 else BLOCK
    return pl.pallas_call(
        _final_norm_kernel,
        grid=(b, seq // tr),
        in_specs=[pl.BlockSpec((pl.Element(tr), pl.Element(d)),
                               lambda i, j: (pl.multiple_of(i * tp + skip_rows + j * tr, BLOCK), 0)),
                  _resident((1, d))],
        out_specs=pl.BlockSpec((1, tr, d), lambda i, j: (i, j, 0)),
        out_shape=jax.ShapeDtypeStruct((b, seq, d), F32),
        compiler_params=pltpu.CompilerParams(dimension_semantics=("parallel", "parallel"), vmem_limit_bytes=VMEM_LIMIT),
        name="final_norm",
    )(h3.reshape(b * tp, d), g)


def _t5_bucket(dist):
    max_exact = REL_BUCKETS // 2
    d = jnp.maximum(dist, 0)
    scaled = jnp.log(jnp.maximum(d, 1).astype(F32) / max_exact) / math.log(REL_MAX_DIST / max_exact)
    large = jnp.minimum(max_exact + (scaled * (REL_BUCKETS - max_exact)).astype(jnp.int32), REL_BUCKETS - 1)
    return jnp.where(d < max_exact, d, large)


def _swa_bias(rel_table):
    q_idx = jnp.arange(BLOCK)[:, None]
    k_idx = jnp.arange(2 * BLOCK)[None, :]
    dist = q_idx + BLOCK - k_idx
    bias = rel_table.astype(F32)[_t5_bucket(dist)].transpose(2, 0, 1) * LOG2E
    ok = (dist >= 0) & (dist < SWA_WINDOW)
    return jnp.where(ok[None], bias, NEG_INF).reshape(SWA_Q_HEADS * BLOCK, 2 * BLOCK)


def _slab_head_order():
    half = SWA_Q_HEADS // 2
    return [h for j in range(half) for h in (j, j + half)]


def _pack_w_in(w_in):
    depth = w_in.shape[0]
    sizes = (512, 128, 128, 512, 512, 512, 8, 512, 512, 3072)
    offs = [0]
    for s in sizes:
        offs.append(offs[-1] + s)
    qa, ka, va, qf, kf, vf, fl, xc, yc, gates = [w_in[:, :, offs[i]:offs[i + 1]] for i in range(len(sizes))]
    scale = HEAD_DIM ** -0.5 * LOG2E
    qa = jnp.concatenate([qa[:, :, h * HEAD_DIM:(h + 1) * HEAD_DIM] for h in _slab_head_order()], axis=-1) * scale
    fl = jnp.pad(fl, ((0, 0), (0, 0), (0, LANES - FOX_HEADS)))
    packed = jnp.concatenate([qa, va, qf * scale, vf, fl, xc, yc, gates], axis=-1)
    keys_t = jnp.swapaxes(jnp.concatenate([kf, ka], axis=-1), 1, 2)
    return packed.astype(BF16), keys_t.astype(BF16)


def _block_diag(w):
    depth, nb, n, _ = w.shape
    eye = jnp.eye(nb, dtype=w.dtype)
    return jnp.einsum('lhij,hk->lhikj', w, eye).reshape(depth, nb * n, nb * n)


def kernel(x, meta_tokens, rel_bias_table, norm_mix, w_in, swa_sinks, fox_forget_bias, conv_w, conv_b,
           lru_w_r, lru_b_r, lru_w_i, lru_b_i, lru_lambda, w_branch, w_out, norm_ffn, w_ffn_in, w_ffn_out,
           norm_final):
    b, seq, d = x.shape
    depth = w_in.shape[0]
    t = N_META + seq
    n_pad = (-t) % BLOCK
    tp = t + n_pad
    assert d == D_MODEL and (n_pad + N_META) % BLOCK == 0

    meta = jnp.broadcast_to(meta_tokens.astype(x.dtype)[None], (b, N_META, d))
    h = jnp.concatenate([jnp.zeros((b, n_pad, d), x.dtype), meta, x], axis=1)

    w_in_p, wkt = _pack_w_in(w_in)
    bias_a = _swa_bias(rel_bias_table)
    sinks =jnp.broadcast_to((swa_sinks.astype(F32) * LOG2E)[:, :, None, None], (depth, SWA_Q_HEADS, BLOCK, LANES))
    sinks = sinks.reshape(depth, SWA_Q_HEADS * BLOCK, LANES)
    fb = jnp.pad(fox_forget_bias.astype(F32), ((0, 0), (0, LANES - FOX_HEADS)))[:, None, :]
    wg = jnp.concatenate([_block_diag(lru_w_r), _block_diag(lru_w_i)], axis=-1).astype(BF16)
    bg = jnp.concatenate([lru_b_r, lru_b_i], axis=-1).astype(F32)[:, None, :]
    wb_a = jnp.concatenate([w_branch[:, 0, h * HEAD_DIM:(h + 1) * HEAD_DIM] for h in _slab_head_order()], axis=1)
    wb = jnp.stack([wb_a, w_branch[:, 1], w_branch[:, 2]], axis=1).astype(BF16)
    wo = w_out.astype(BF16)
    w1 = w_ffn_in.astype(BF16)
    w2 = w_ffn_out.astype(BF16)

    for l in range(depth):
        qva, qvf, kt, fl, xy, gt = _in_proj(h.reshape(b * tp, d), norm_mix[l][None, :], w_in_p[l], wkt[l])
        ck = _fox_prefix(fl.reshape(b, tp, LANES), fb[l], n_pad)
        o_a = _swa(qva.reshape(b, tp, A_COLS), kt, bias_a, sinks[l], n_pad)
        o_f = _fox(qvf.reshape(b, tp, F_COLS), kt, ck, n_pad)
        o_c = _lru(xy.reshape(b, tp, C_COLS), conv_w[l], conv_b[l][None, :], wg[l], bg[l], lru_lambda[l][None, :],
                   n_pad)
        h = _mix_ffn(h, o_a, o_f, o_c, gt.reshape(b, tp, G_COLS), wb[l], wo[l], norm_ffn[l][None, :], w1[l], w2[l],
                     n_pad)
    return _final_norm(h, norm_final[None, :], n_pad + N_META, seq)
```

```python
import functools
import math

import jax
import jax.numpy as jnp
from jax import lax
from jax.experimental import pallas as pl
from jax.experimental.pallas import tpu as pltpu

F32 = jnp.float32
BF16 = jnp.bfloat16

D_MODEL = 1024
HEAD_DIM = 64
N_META = 16
BLOCK = 128
LANES = 128
NEG_INF = -1e30
LOG2E = math.log2(math.e)
SWA_WINDOW = 128
SWA_Q_HEADS = 8
SWA_KV_HEADS = 2
FOX_HEADS = 8
LRU_WIDTH = D_MODEL // 2
LRU_BLOCKS = 8
CONV_WIDTH = 4
LRU_C = 8.0
REL_BUCKETS = 32
REL_MAX_DIST = 128
D_FF = 2816
N_BRANCH = 3
EPS = 1e-6

SWA_QCOLS = SWA_Q_HEADS * HEAD_DIM
A_COLS = SWA_QCOLS + SWA_KV_HEADS * HEAD_DIM
FOX_WIDTH = FOX_HEADS * HEAD_DIM
F_COLS = 2 * FOX_WIDTH
KT_ROWS = FOX_WIDTH + SWA_KV_HEADS * HEAD_DIM
C_COLS = 2 * LRU_WIDTH
G_COLS = N_BRANCH * D_MODEL
OFF_A = 0
OFF_F = OFF_A + A_COLS
OFF_FL = OFF_F + F_COLS
OFF_C = OFF_FL + LANES
OFF_G = OFF_C + C_COLS
IN_COLS_PACKED = OFF_G + G_COLS

VMEM_LIMIT = 56 * 1024 * 1024


def _sigmoid(x):
    return 0.5 * jnp.tanh(0.5 * x) + 0.5


def _log_sigmoid(x):
    return jnp.minimum(x, 0.0) - jnp.log1p(jnp.exp(-jnp.abs(x)))


def _gelu_tanh(x):
    c = math.sqrt(2.0 / math.pi)
    return 0.5 * x * (1.0 + jnp.tanh(c * (x + 0.044715 * (x * x * x))))


def _rms_norm(x, g):
    ms = jnp.mean(x * x, axis=-1, keepdims=True)
    return x * lax.rsqrt(ms + EPS) * g


def _dot(a, b):
    return jnp.dot(a, b, preferred_element_type=F32)


def _dot_nt(a, b):
    return lax.dot_general(a, b, (((1,), (1,)), ((), ())), preferred_element_type=F32)


def _resident(shape):
    nd = len(shape)
    return pl.BlockSpec(shape, lambda *_: (0,) * nd, pipeline_mode=pl.Buffered(1))


def _row_tile(tp):
    return 384 if tp % 384 == 0 else BLOCK


def _pv_operand(v, lo_k):
    zero = jnp.zeros_like(v)
    ind_lo = jnp.where(lo_k, 1.0, 0.0).astype(BF16)
    ind_hi = jnp.where(lo_k, 0.0, 1.0).astype(BF16)
    top = jnp.concatenate([jnp.where(lo_k, v, zero), ind_lo], axis=1)
    bot = jnp.concatenate([jnp.where(lo_k, zero, v), ind_hi], axis=1)
    return jnp.concatenate([top, bot], axis=0)


def _in_proj_kernel(x_ref, g_ref, w_ref, wkt_ref, qa_ref, qf_ref, kt_ref, fl_ref, xy_ref, gt_ref):
    u = _rms_norm(x_ref[...], g_ref[...]).astype(BF16)

    def proj(c0, width):
        return _dot(u, w_ref[:, c0:c0 + width])

    qa_ref[:, 0:SWA_QCOLS] = proj(OFF_A, SWA_QCOLS).astype(BF16)
    qa_ref[:, SWA_QCOLS:] = proj(OFF_A + SWA_QCOLS, A_COLS - SWA_QCOLS).astype(BF16)
    for c in range(0, F_COLS, 512):
        qf_ref[:, c:c + 512] = proj(OFF_F + c, 512).astype(BF16)
    kt_ref[...] = _dot_nt(wkt_ref[...], u).astype(BF16)
    fl_ref[...] = proj(OFF_FL, LANES)
    xy_ref[:, 0:LRU_WIDTH] = proj(OFF_C, LRU_WIDTH).astype(BF16)
    xy_ref[:, LRU_WIDTH:] = _gelu_tanh(proj(OFF_C + LRU_WIDTH, LRU_WIDTH)).astype(BF16)
    for c in range(0, G_COLS, 512):
        gt_ref[:, c:c + 512] = _sigmoid(proj(OFF_G + c, 512)).astype(BF16)


def _in_proj(h2, g, w, wkt):
    rows = h2.shape[0]
    tm = next(t for t in (768, 384, BLOCK) if rows % t == 0)
    row_spec = lambda cols: pl.BlockSpec((tm, cols), lambda i: (i, 0))
    return pl.pallas_call(
        _in_proj_kernel,
        grid=(rows // tm,),
        in_specs=[row_spec(D_MODEL), _resident((1, D_MODEL)), _resident((D_MODEL, IN_COLS_PACKED)),
                  _resident((KT_ROWS, D_MODEL))],
        out_specs=[row_spec(A_COLS), row_spec(F_COLS), pl.BlockSpec((KT_ROWS, tm), lambda i: (0, i)),
                   row_spec(LANES), row_spec(C_COLS), row_spec(G_COLS)],
        out_shape=[
            jax.ShapeDtypeStruct((rows, A_COLS), BF16),
            jax.ShapeDtypeStruct((rows, F_COLS), BF16),
            jax.ShapeDtypeStruct((KT_ROWS, rows), BF16),
            jax.ShapeDtypeStruct((rows, LANES), F32),
            jax.ShapeDtypeStruct((rows, C_COLS), BF16),
            jax.ShapeDtypeStruct((rows, G_COLS), BF16),
        ],
        compiler_params=pltpu.CompilerParams(dimension_semantics=("parallel",), vmem_limit_bytes=VMEM_LIMIT),
        name="in_proj",
    )(h2, g, w, wkt)


def _fox_prefix_kernel(fl_ref, fb_ref, ck_ref, cum_ref, *, tp, n_pad):
    lf = _log_sigmoid(fl_ref[0] + fb_ref[...])
    row = lax.broadcasted_iota(jnp.int32, (tp, 1), 0)
    lf = jnp.where(row >= n_pad, lf, 0.0)
    r = lax.broadcasted_iota(jnp.int32, (BLOCK, BLOCK), 0)
    c = lax.broadcasted_iota(jnp.int32, (BLOCK, BLOCK), 1)
    tri = jnp.where(r >= c, 1.0, 0.0).astype(F32)
    carry = jnp.zeros((1, LANES), F32)
    for blk in range(tp // BLOCK):
        sl = slice(blk * BLOCK, (blk + 1) * BLOCK)
        cum = jnp.dot(tri, lf[sl], preferred_element_type=F32, precision=lax.Precision.HIGHEST) + carry
        cum_ref[sl, :] = cum
        carry = cum[BLOCK - 1:BLOCK, :]
    ck_ref[0] = cum_ref[...].T[0:FOX_HEADS, :]


def _fox_prefix(fl3, fb, n_pad):
    b, tp, _ = fl3.shape
    return pl.pallas_call(
        functools.partial(_fox_prefix_kernel, tp=tp, n_pad=n_pad),
        grid=(b,),
        in_specs=[pl.BlockSpec((1, tp, LANES), lambda i: (i, 0, 0)), _resident((1, LANES))],
        out_specs=pl.BlockSpec((1, FOX_HEADS, tp), lambda i: (i, 0, 0)),
        out_shape=jax.ShapeDtypeStruct((b, FOX_HEADS, tp), F32),
        scratch_shapes=[pltpu.VMEM((tp, LANES), F32)],
        compiler_params=pltpu.CompilerParams(dimension_semantics=("parallel",), vmem_limit_bytes=VMEM_LIMIT),
        name="fox_prefix",
    )(fl3, fb)


def _swa_kernel(q_ref, v_ref, vp_ref, kt_ref, ktp_ref, bias_ref, sink_ref, o_ref, *, blocks, n_pad):
    ib = pl.program_id(1)
    half = SWA_Q_HEADS // 2
    rows = half * BLOCK
    lo = lax.broadcasted_iota(jnp.int32, (BLOCK, LANES), 1) < HEAD_DIM
    lo_k = lax.broadcasted_iota(jnp.int32, (2 * BLOCK, LANES), 1) < HEAD_DIM
    lo_o = lax.broadcasted_iota(jnp.int32, (rows, LANES), 1) < HEAD_DIM
    kidx = lax.broadcasted_iota(jnp.int32, (1, BLOCK), 1)
    sink = sink_ref[...]
    for g in range(blocks):
        cur = slice(g * BLOCK, (g + 1) * BLOCK)
        if g == 0:
            kt_prev, v_prev = ktp_ref[...], vp_ref[0]
        else:
            prev = slice((g - 1) * BLOCK, g * BLOCK)
            kt_prev, v_prev = kt_ref[:, prev], v_ref[0, prev, :]
        kt_band = jnp.concatenate([kt_prev, kt_ref[:, cur]], axis=1)
        v_band = jnp.concatenate([v_prev, v_ref[0, cur, :]], axis=0)
        slabs = [q_ref[0, cur, j * LANES:(j + 1) * LANES] for j in range(half)]
        zero = jnp.zeros_like(slabs[0])
        q_all = jnp.concatenate([jnp.where(lo, s, zero) for s in slabs] + [jnp.where(lo, zero, s) for s in slabs],
                                axis=0)
        s = _dot(q_all, kt_band) + bias_ref[...]
        key0 = (ib * blocks + g - 1) * BLOCK + kidx
        s_prev = jnp.where(key0 >= n_pad, s[:, 0:BLOCK], NEG_INF)
        s_cur = jnp.where(key0 + BLOCK >= n_pad, s[:, BLOCK:], NEG_INF)
        m = jnp.maximum(jnp.max(jnp.maximum(s_prev, s_cur), axis=-1, keepdims=True), sink)
        p = jnp.concatenate([jnp.exp2(s_prev - m), jnp.exp2(s_cur - m)], axis=1).astype(BF16)
        pv = _dot(jnp.concatenate([p[0:rows], p[rows:]], axis=1), _pv_operand(v_band, lo_k))
        esink = jnp.exp2(sink - m)
        den = pv[:, LANES:] + jnp.where(lo_o, esink[0:rows], esink[rows:])
        o = (pv[:, 0:LANES] / den).astype(BF16)
        for j in range(half):
            o_ref[0, cur, j * LANES:(j + 1) * LANES] = o[j * BLOCK:(j + 1) * BLOCK]


def _swa(qv3, kt, bias, sinks, n_pad):
    b, tp, _ = qv3.shape
    nb = tp // BLOCK
    blocks = 3 if nb % 3 == 0 else 1
    rows = blocks * BLOCK
    steps = nb // blocks
    vblk = SWA_QCOLS // LANES
    krow = FOX_WIDTH // LANES
    return pl.pallas_call(
        functools.partial(_swa_kernel, blocks=blocks, n_pad=n_pad),
        grid=(b, steps),
        in_specs=[
            pl.BlockSpec((1, rows, SWA_QCOLS), lambda i, j: (i, j, 0)),
            pl.BlockSpec((1, rows, LANES), lambda i, j: (i, j, vblk)),
            pl.BlockSpec((1, BLOCK, LANES), lambda i, j: (i, jnp.maximum(j * blocks - 1, 0), vblk)),
            pl.BlockSpec((LANES, rows), lambda i, j: (krow, i * steps + j)),
            pl.BlockSpec((LANES, BLOCK), lambda i, j: (krow, jnp.maximum((i * steps + j) * blocks - 1, 0))),
            _resident((SWA_Q_HEADS * BLOCK, 2 * BLOCK)),
            _resident((SWA_Q_HEADS * BLOCK, LANES)),
        ],
        out_specs=pl.BlockSpec((1, rows, SWA_QCOLS), lambda i, j: (i, j, 0)),
        out_shape=jax.ShapeDtypeStruct((b, tp, SWA_QCOLS), BF16),
        compiler_params=pltpu.CompilerParams(dimension_semantics=("parallel", "parallel"), vmem_limit_bytes=VMEM_LIMIT),
        name="swa",
    )(qv3, qv3, qv3, kt, kt, bias, sinks)


FOX_TQ = 384
FOX_TK = 512
AUG_ROWS = 16


def _split3(c):
    hi = c.astype(BF16).astype(F32)
    mid = (c - hi).astype(BF16).astype(F32)
    return hi, mid, (c - hi) - mid


def _fox_kernel(q_ref, v_ref, kt_ref, ck_ref, o_ref, qa_s, m_s, acc_s, *, tq, tk, n_tiles, tail, n_pad):
    i = pl.program_id(1)
    pairs = FOX_HEADS // 2
    lane = lax.broadcasted_iota(jnp.int32, (tq, LANES), 1)
    lo = lane < HEAD_DIM
    pats = [jnp.where((lane >= 3 * e) & (lane < 3 * e + 3), 1.0, 0.0).astype(BF16) for e in range(2)]
    for hp in range(pairs):
        q = q_ref[0, :, hp * LANES:(hp + 1) * LANES]
        zero = jnp.zeros_like(q)
        qa_s[hp, 0:tq, 0:LANES] = jnp.where(lo, q, zero)
        qa_s[hp, tq:, 0:LANES] = jnp.where(lo, zero, q)
        qa_s[hp, 0:tq, LANES:] = pats[0]
        qa_s[hp, tq:, LANES:] = pats[1]
    m_s[...] = jnp.full(m_s.shape, NEG_INF, F32)
    acc_s[...] = jnp.zeros(acc_s.shape, F32)

    def tile(start, size, masked):
        sub = lax.broadcasted_iota(jnp.int32, (AUG_ROWS, size), 0)
        key_ok = start + lax.broadcasted_iota(jnp.int32, (1, size), 1) >= n_pad
        lo_k = lax.broadcasted_iota(jnp.int32, (size, LANES), 1) < HEAD_DIM
        if masked:
            rel = lax.broadcasted_iota(jnp.int32, (tq, size), 1) - lax.broadcasted_iota(jnp.int32, (tq, size), 0)
            ok = rel <= i * tq - start
        for hp in range(pairs):
            kt = kt_ref[hp * LANES:(hp + 1) * LANES, pl.ds(start, size)]
            v = v_ref[0, pl.ds(start, size), hp * LANES:(hp + 1) * LANES]
            aug = jnp.zeros((AUG_ROWS, size), F32)
            for e in range(2):
                c = jnp.where(key_ok, ck_ref[0, 2 * hp + e:2 * hp + e + 1, pl.ds(start, size)] * (-LOG2E), NEG_INF)
                for r, piece in enumerate(_split3(c)):
                    aug = jnp.where(sub == 3 * e + r, piece, aug)
            kaug = jnp.concatenate([kt, aug.astype(BF16), jnp.zeros((LANES - AUG_ROWS, size), BF16)], axis=0)
            s2 = _dot(qa_s[hp], kaug)
            ps, alphas = [], []
            for e in range(2):
                s = s2[e * tq:(e + 1) * tq]
                if masked:
                    s = jnp.where(ok, s, NEG_INF)
                chunks = [s[:, c0:c0 + LANES] for c0 in range(0, size, LANES)]
                m_prev = m_s[2 * hp + e]
                m_new = jnp.maximum(m_prev, jnp.max(functools.reduce(jnp.maximum, chunks), axis=-1, keepdims=True))
                alphas.append(jnp.exp2(m_prev - m_new))
                ps.append(jnp.concatenate([jnp.exp2(ch - m_new) for ch in chunks], axis=1).astype(BF16))
                m_s[2 * hp + e] = m_new
            alpha = jnp.where(lo, alphas[0], alphas[1])
            alpha = jnp.concatenate([alpha, alpha], axis=1)
            acc_s[hp] = alpha * acc_s[hp] + _dot(jnp.concatenate(ps, axis=1), _pv_operand(v, lo_k))

    n_last = jnp.minimum((tq * (i + 1) - 1) // tk, n_tiles - 1)
    n_free = (tq * i + 1) // tk

    def full_tile(j, masked):
        tile(pl.multiple_of(j * tk, tk), tk, masked)

    def two_free(j2, carry):
        full_tile(2 * j2, False)
        full_tile(2 * j2 + 1, False)
        return carry

    def one_masked(j, carry):
        full_tile(j, True)
        return carry

    lax.fori_loop(0, n_free // 2, two_free, 0)
    pl.when(n_free % 2 == 1)(lambda: full_tile(n_free - 1, False))
    lax.fori_loop(n_free, n_last + 1, one_masked, 0)
    if tail:
        pl.when(tq * (i + 1) > n_tiles * tk)(lambda: tile(n_tiles * tk, tail, True))

    for hp in range(pairs):
        acc = acc_s[hp]
        o_ref[0, :, hp * LANES:(hp + 1) * LANES] = (acc[:, 0:LANES] / acc[:, LANES:]).astype(BF16)


def _fox(qv3, kt, ck, n_pad):
    b, tp, _ = qv3.shape
    tq, tk = FOX_TQ, FOX_TK
    n_tiles, tail = tp // tk, tp % tk
    assert tp % tq == 0 and tail % LANES == 0 and tail <= tq and n_tiles >= 1
    pairs = FOX_HEADS // 2
    return pl.pallas_call(
        functools.partial(_fox_kernel, tq=tq, tk=tk, n_tiles=n_tiles, tail=tail, n_pad=n_pad),
        grid=(b, tp // tq),
        in_specs=[
            pl.BlockSpec((1, tq, FOX_WIDTH), lambda i, j: (i, j, 0)),
            pl.BlockSpec((1, tp, FOX_WIDTH), lambda i, j: (i, 0, 1)),
            pl.BlockSpec((FOX_WIDTH, tp), lambda i, j: (0, i)),
            pl.BlockSpec((1, FOX_HEADS, tp), lambda i, j: (i, 0, 0)),
        ],
        out_specs=pl.BlockSpec((1, tq, FOX_WIDTH), lambda i, j: (i, j, 0)),
        out_shape=jax.ShapeDtypeStruct((b, tp, FOX_WIDTH), BF16),
        scratch_shapes=[
            pltpu.VMEM((pairs, 2 * tq, 2 * LANES), BF16),
            pltpu.VMEM((FOX_HEADS, tq, LANES), F32),
            pltpu.VMEM((pairs, tq, 2 * LANES), F32),
        ],
        compiler_params=pltpu.CompilerParams(
            dimension_semantics=("parallel", "arbitrary"), vmem_limit_bytes=VMEM_LIMIT),
        name="fox",
    )(qv3, qv3, kt, ck)


SUBLANES = 8


LRU_SEQS = 2


def _lru_kernel(xy_ref, perm_ref, unperm_ref, cw_ref, cb_ref, wg_ref, bg_ref, lam_ref, o_ref,
                xe_s, a_s, h_s, hist_s, carry, *, seqs, tt, n_pad):
    for q in range(seqs):
        _lru_one(xy_ref.at[q], perm_ref, unperm_ref, cw_ref, cb_ref, wg_ref, bg_ref, lam_ref, o_ref.at[q],
                 xe_s.at[q], a_s.at[q], h_s.at[q], hist_s.at[q], carry.at[q], tt=tt, n_pad=n_pad)


def _lru_one(xy_ref, perm_ref, unperm_ref, cw_ref, cb_ref, wg_ref, bg_ref, lam_ref, o_ref,
             xe_s, a_s, h_s, hist_s, carry, *, tt, n_pad):
    it = pl.program_id(1)
    seg = tt // SUBLANES
    taps = CONV_WIDTH - 1
    w = LRU_WIDTH
    sub = lax.broadcasted_iota(jnp.int32, (SUBLANES, w), 0)

    @pl.when(it == 0)
    def _():
        hist_s[...] = jnp.zeros(hist_s.shape, F32)
        carry[...] = jnp.zeros(carry.shape, F32)

    xyp = _dot(perm_ref[...], xy_ref[...])
    xp, gate = xyp[:, 0:w], xyp[:, w:]

    for v in range(taps):
        src = slice((seg - taps + v) * SUBLANES, (seg - taps + v + 1) * SUBLANES)
        xe_s[v * SUBLANES:(v + 1) * SUBLANES, :] = jnp.where(
            sub == 0, pltpu.roll(hist_s[v], 1, 0), pltpu.roll(xp[src], 1, 0))
        hist_s[v] = xp[src]
    xe_s[taps * SUBLANES:, :] = xp
    conv = xe_s[0:tt, :] * cw_ref[0:1, :]
    for i in range(1, CONV_WIDTH):
        conv = conv + xe_s[i * SUBLANES:i * SUBLANES + tt, :] * cw_ref[i:i + 1, :]
    conv = conv + cb_ref[...]

    z = _dot(conv.astype(BF16), wg_ref[...]) + bg_ref[...]
    r = _sigmoid(z[:, 0:w])
    gi = _sigmoid(z[:, w:])
    log_a = LRU_C * r * _log_sigmoid(lam_ref[...])
    p_idx = lax.broadcasted_iota(jnp.int32, (tt, 1), 0)
    time = it * tt + (p_idx & (SUBLANES - 1)) * seg + (p_idx >> 3)
    a = jnp.exp(log_a)
    one_minus_a2 = -jnp.tanh(log_a) * (a * a + 1.0)
    b = jnp.where(time >= n_pad, jnp.sqrt(one_minus_a2) * (gi * conv), 0.0)

    h_loc = jnp.zeros((SUBLANES, w), F32)
    prod = jnp.ones((SUBLANES, w), F32)
    for v in range(seg):
        rows = slice(v * SUBLANES, (v + 1) * SUBLANES)
        h_loc = a[rows] * h_loc + b[rows]
        prod = a[rows] * prod
        h_s[rows, :] = h_loc
        a_s[rows, :] = prod
    aa, bb = prod, h_loc
    for s in (1, 2, 4):
        keep = sub >= s
        bb = jnp.where(keep, aa * pltpu.roll(bb, s, 0) + bb, bb)
        aa = jnp.where(keep, aa * pltpu.roll(aa, s, 0), aa)
    h_end = aa * carry[...] + bb
    h_in = jnp.where(sub == 0, carry[...], pltpu.roll(h_end, 1, 0))
    carry[...] = h_end[SUBLANES - 1:SUBLANES, :]

    h_in_all = jnp.concatenate([h_in] * seg, axis=0)
    out = ((h_s[...] + a_s[...] * h_in_all) * gate).astype(BF16)
    o_ref[...] = _dot(unperm_ref[...], out).astype(BF16)


def _lru(xy3, cw, cb, wg, bg, lam, n_pad):
    b, tp, _ = xy3.shape
    tt = _row_tile(tp)
    w = LRU_WIDTH
    seg = tt // SUBLANES
    p_idx = jnp.arange(tt)
    perm = (jnp.arange(tt)[None, :] == ((p_idx % SUBLANES) * seg + p_idx // SUBLANES)[:, None]).astype(BF16)
    seqs = LRU_SEQS if b % LRU_SEQS == 0 else 1
    return pl.pallas_call(
        functools.partial(_lru_kernel, seqs=seqs, tt=tt, n_pad=n_pad),
        grid=(b // seqs, tp // tt),
        in_specs=[
            pl.BlockSpec((seqs, tt, C_COLS), lambda i, j: (i, j, 0)),
            _resident((tt, tt)), _resident((tt, tt)),
            _resident((CONV_WIDTH, w)), _resident((1, w)), _resident((w, 2 * w)), _resident((1, 2 * w)),
            _resident((1, w)),
        ],
        out_specs=pl.BlockSpec((seqs, tt, w), lambda i, j: (i, j, 0)),
        out_shape=jax.ShapeDtypeStruct((b, tp, w), BF16),
        scratch_shapes=[
            pltpu.VMEM((seqs, tt + (CONV_WIDTH - 1) * SUBLANES, w), F32), pltpu.VMEM((seqs, tt, w), F32),
            pltpu.VMEM((seqs, tt, w), F32), pltpu.VMEM((seqs, CONV_WIDTH - 1, SUBLANES, w), F32),
            pltpu.VMEM((seqs, 1, w), F32),
        ],
        compiler_params=pltpu.CompilerParams(dimension_semantics=("parallel", "arbitrary"), vmem_limit_bytes=VMEM_LIMIT),
        name="lru",
    )(xy3, perm, perm.T, cw, cb, wg, bg, lam)


FF_CHUNK = 256


def _mix_ffn_kernel(h_ref, oa_ref, of_ref, oc_ref, gt_ref, wb_ref, wo_ref, gn_ref, w1_ref, w2_ref, out_ref, act_s,
                    *, tm, n_pad):
    j = pl.program_id(1)
    d = D_MODEL
    merged = gt_ref[0, :, 0:d].astype(F32) * _dot(oa_ref[0], wb_ref[0])
    merged = merged + gt_ref[0, :, d:2 * d].astype(F32) * _dot(of_ref[0], wb_ref[1])
    merged = merged + gt_ref[0, :, 2 * d:3 * d].astype(F32) * _dot(oc_ref[0], wb_ref[2])
    hm = h_ref[0] + _dot(merged.astype(BF16), wo_ref[...])
    u = _rms_norm(hm, gn_ref[...]).astype(BF16)
    for c in range(0, D_FF, FF_CHUNK):
        gate = _dot(u, w1_ref[:, c:c + FF_CHUNK])
        up = _dot(u, w1_ref[:, D_FF + c:D_FF + c + FF_CHUNK])
        act_s[:, c:c + FF_CHUNK] = (gate * _sigmoid(gate) * up).astype(BF16)
    y = hm + _dot(act_s[...], w2_ref[...])
    row = j * tm + lax.broadcasted_iota(jnp.int32, (tm, 1), 0)
    out_ref[0] = jnp.where(row >= n_pad, y, 0.0)


def _mix_ffn(h3, oa, of, oc, gt3, wb, wo, gn, w1, w2, n_pad):
    b, tp, d = h3.shape
    tm = _row_tile(tp)
    w = LRU_WIDTH
    tile = lambda cols: pl.BlockSpec((1, tm, cols), lambda i, j: (i, j, 0))
    return pl.pallas_call(
        functools.partial(_mix_ffn_kernel, tm=tm, n_pad=n_pad),
        grid=(b, tp // tm),
        in_specs=[
            tile(d), tile(w), tile(w), tile(w), tile(G_COLS),
            _resident((N_BRANCH, w, d)), _resident((d, d)), _resident((1, d)),
            _resident((d, 2 * D_FF)), _resident((D_FF, d)),
        ],
        out_specs=tile(d),
        out_shape=jax.ShapeDtypeStruct((b, tp, d), F32),
        scratch_shapes=[pltpu.VMEM((tm, D_FF), BF16)],
        compiler_params=pltpu.CompilerParams(dimension_semantics=("parallel", "parallel"), vmem_limit_bytes=VMEM_LIMIT),
        name="mix_ffn",
    )(h3, oa, of, oc, gt3, wb, wo, gn, w1, w2)


def _final_norm_kernel(h_ref, g_ref, o_ref):
    o_ref[0] = _rms_norm(h_ref[...], g_ref[...])


def _final_norm(h3, g, skip_rows, seq):
    b, tp, d = h3.shape
    tr = 512 if seq % 512 == 0 else BLOCK
    return pl.pallas_call(
        _final_norm_kernel,
        grid=(b, seq // tr),
        in_specs=[pl.BlockSpec((pl.Element(tr), pl.Element(d)),
                               lambda i, j: (pl.multiple_of(i * tp + skip_rows + j * tr, BLOCK), 0)),
                  _resident((1, d))],
        out_specs=pl.BlockSpec((1, tr, d), lambda i, j: (i, j, 0)),
        out_shape=jax.ShapeDtypeStruct((b, seq, d), F32),
        compiler_params=pltpu.CompilerParams(dimension_semantics=("parallel", "parallel"), vmem_limit_bytes=VMEM_LIMIT),
        name="final_norm",
    )(h3.reshape(b * tp, d), g)


def _t5_bucket(dist):
    max_exact = REL_BUCKETS // 2
    d = jnp.maximum(dist, 0)
    scaled = jnp.log(jnp.maximum(d, 1).astype(F32) / max_exact) / math.log(REL_MAX_DIST / max_exact)
    large = jnp.minimum(max_exact + (scaled * (REL_BUCKETS - max_exact)).astype(jnp.int32), REL_BUCKETS - 1)
    return jnp.where(d < max_exact, d, large)


def _swa_bias_kernel(bucket_ref, table_ref, o_ref):
    bucket = bucket_ref[...]
    for h in range(SWA_Q_HEADS):
        acc = jnp.full(bucket.shape, NEG_INF, F32)
        for c in range(REL_BUCKETS):
            acc = jnp.where(bucket == c, table_ref[c, h] * LOG2E, acc)
        o_ref[h * BLOCK:(h + 1) * BLOCK, :] = acc


def _swa_bias(rel_table):
    q_idx = jnp.arange(BLOCK)[:, None]
    k_idx = jnp.arange(2 * BLOCK)[None, :]
    dist = q_idx + BLOCK - k_idx
    bucket = jnp.where((dist >= 0) & (dist < SWA_WINDOW), _t5_bucket(dist), -1).astype(jnp.int32)
    return pl.pallas_call(
        _swa_bias_kernel,
        in_specs=[pl.BlockSpec(memory_space=pltpu.VMEM), pl.BlockSpec(memory_space=pltpu.SMEM)],
        out_specs=pl.BlockSpec(memory_space=pltpu.VMEM),
        out_shape=jax.ShapeDtypeStruct((SWA_Q_HEADS * BLOCK, 2 * BLOCK), F32),
        name="swa_bias",
    )(bucket, rel_table.astype(F32))


def _slab_head_order():
    half = SWA_Q_HEADS // 2
    return [h for j in range(half) for h in (j, j + half)]


def _pack_w_in(w_in):
    depth = w_in.shape[0]
    sizes = (512, 128, 128, 512, 512, 512, 8, 512, 512, 3072)
    offs = [0]
    for s in sizes:
        offs.append(offs[-1] + s)
    qa, ka, va, qf, kf, vf, fl, xc, yc, gates = [w_in[:, :, offs[i]:offs[i + 1]] for i in range(len(sizes))]
    scale = HEAD_DIM ** -0.5 * LOG2E
    qa = jnp.concatenate([qa[:, :, h * HEAD_DIM:(h + 1) * HEAD_DIM] for h in _slab_head_order()], axis=-1) * scale
    fl = jnp.pad(fl, ((0, 0), (0, 0), (0, LANES - FOX_HEADS)))
    packed = jnp.concatenate([qa, va, qf * scale, vf, fl, xc, yc, gates], axis=-1)
    keys_t = jnp.swapaxes(jnp.concatenate([kf, ka], axis=-1), 1, 2)
    return packed.astype(BF16), keys_t.astype(BF16)


def _block_diag(w):
    depth, nb, n, _ = w.shape
    eye = jnp.eye(nb, dtype=w.dtype)
    return jnp.einsum('lhij,hk->lhikj', w, eye).reshape(depth, nb * n, nb * n)


def kernel(x, meta_tokens, rel_bias_table, norm_mix, w_in, swa_sinks, fox_forget_bias, conv_w, conv_b,
           lru_w_r, lru_b_r, lru_w_i, lru_b_i, lru_lambda, w_branch, w_out, norm_ffn, w_ffn_in, w_ffn_out,
           norm_final):
    b, seq, d = x.shape
    depth = w_in.shape[0]
    t = N_META + seq
    n_pad = (-t) % BLOCK
    tp = t + n_pad
    assert d == D_MODEL and (n_pad + N_META) % BLOCK == 0

    meta = jnp.broadcast_to(meta_tokens.astype(x.dtype)[None], (b, N_META, d))
    h = jnp.concatenate([jnp.zeros((b, n_pad, d), x.dtype), meta, x], axis=1)

    w_in_p, wkt = _pack_w_in(w_in)
    bias_a = _swa_bias(rel_bias_table)
    sinks =jnp.broadcast_to((swa_sinks.astype(F32) * LOG2E)[:, :, None, None], (depth, SWA_Q_HEADS, BLOCK, LANES))
    sinks = sinks.reshape(depth, SWA_Q_HEADS * BLOCK, LANES)
    fb = jnp.pad(fox_forget_bias.astype(F32), ((0, 0), (0, LANES - FOX_HEADS)))[:, None, :]
    wg = jnp.concatenate([_block_diag(lru_w_r), _block_diag(lru_w_i)], axis=-1).astype(BF16)
    bg = jnp.concatenate([lru_b_r, lru_b_i], axis=-1).astype(F32)[:, None, :]
    wb_a = jnp.concatenate([w_branch[:, 0, h * HEAD_DIM:(h + 1) * HEAD_DIM] for h in _slab_head_order()], axis=1)
    wb = jnp.stack([wb_a, w_branch[:, 1], w_branch[:, 2]], axis=1).astype(BF16)
    wo = w_out.astype(BF16)
    w1 = w_ffn_in.astype(BF16)
    w2 = w_ffn_out.astype(BF16)

    for l in range(depth):
        qva, qvf, kt, fl, xy, gt = _in_proj(h.reshape(b * tp, d), norm_mix[l][None, :], w_in_p[l], wkt[l])
        ck = _fox_prefix(fl.reshape(b, tp, LANES), fb[l], n_pad)
        o_a = _swa(qva.reshape(b, tp, A_COLS), kt, bias_a, sinks[l], n_pad)
        o_f = _fox(qvf.reshape(b, tp, F_COLS), kt, ck, n_pad)
        o_c = _lru(xy.reshape(b, tp, C_COLS), conv_w[l], conv_b[l][None, :], wg[l], bg[l], lru_lambda[l][None, :],
                   n_pad)
        h = _mix_ffn(h, o_a, o_f, o_c, gt.reshape(b, tp, G_COLS), wb[l], wo[l], norm_ffn[l][None, :], w1[l], w2[l],
                     n_pad)
    return _final_norm(h, norm_final[None, :], n_pad + N_META, seq)
```

```python
import functools
import math

import jax
import jax.numpy as jnp
from jax import lax
from jax.experimental import pallas as pl
from jax.experimental.pallas import tpu as pltpu

F32 = jnp.float32
BF16 = jnp.bfloat16

D_MODEL = 1024
HEAD_DIM = 64
N_META = 16
BLOCK = 128
LANES = 128
NEG_INF = -1e30
LOG2E = math.log2(math.e)
SWA_WINDOW = 128
SWA_Q_HEADS = 8
SWA_KV_HEADS = 2
FOX_HEADS = 8
LRU_WIDTH = D_MODEL // 2
LRU_BLOCKS = 8
CONV_WIDTH = 4
LRU_C = 8.0
REL_BUCKETS = 32
REL_MAX_DIST = 128
D_FF = 2816
N_BRANCH = 3
EPS = 1e-6

SWA_QCOLS = SWA_Q_HEADS * HEAD_DIM
A_COLS = SWA_QCOLS + SWA_KV_HEADS * HEAD_DIM
FOX_WIDTH = FOX_HEADS * HEAD_DIM
F_COLS = 2 * FOX_WIDTH
KT_ROWS = FOX_WIDTH + SWA_KV_HEADS * HEAD_DIM
C_COLS = 2 * LRU_WIDTH
G_COLS = N_BRANCH * D_MODEL
OFF_A = 0
OFF_F = OFF_A + A_COLS
OFF_FL = OFF_F + F_COLS
OFF_C = OFF_FL + LANES
OFF_G = OFF_C + C_COLS
IN_COLS_PACKED = OFF_G + G_COLS

VMEM_LIMIT = 56 * 1024 * 1024


def _sigmoid(x):
    return 0.5 * jnp.tanh(0.5 * x) + 0.5


def _log_sigmoid(x):
    return jnp.minimum(x, 0.0) - jnp.log1p(jnp.exp(-jnp.abs(x)))


def _gelu_tanh(x):
    c = math.sqrt(2.0 / math.pi)
    return 0.5 * x * (1.0 + jnp.tanh(c * (x + 0.044715 * (x * x * x))))


def _rms_norm(x, g):
    ms = jnp.mean(x * x, axis=-1, keepdims=True)
    return x * lax.rsqrt(ms + EPS) * g


def _dot(a, b):
    return jnp.dot(a, b, preferred_element_type=F32)


def _dot_nt(a, b):
    return lax.dot_general(a, b, (((1,), (1,)), ((), ())), preferred_element_type=F32)


def _resident(shape):
    nd = len(shape)
    return pl.BlockSpec(shape, lambda *_: (0,) * nd, pipeline_mode=pl.Buffered(1))


def _row_tile(tp):
    return 384 if tp % 384 == 0 else BLOCK


def _pv_operand(v, lo_k):
    zero = jnp.zeros_like(v)
    ind_lo = jnp.where(lo_k, 1.0, 0.0).astype(BF16)
    ind_hi = jnp.where(lo_k, 0.0, 1.0).astype(BF16)
    top = jnp.concatenate([jnp.where(lo_k, v, zero), ind_lo], axis=1)
    bot = jnp.concatenate([jnp.where(lo_k, zero, v), ind_hi], axis=1)
    return jnp.concatenate([top, bot], axis=0)


def _in_proj_kernel(x_ref, g_ref, w_ref, wkt_ref, qa_ref, qf_ref, kt_ref, fl_ref, xy_ref, gt_ref):
    u = _rms_norm(x_ref[...], g_ref[...]).astype(BF16)

    def proj(c0, width):
        return _dot(u, w_ref[:, c0:c0 + width])

    qa_ref[:, 0:SWA_QCOLS] = proj(OFF_A, SWA_QCOLS).astype(BF16)
    qa_ref[:, SWA_QCOLS:] = proj(OFF_A + SWA_QCOLS, A_COLS - SWA_QCOLS).astype(BF16)
    for c in range(0, F_COLS, 512):
        qf_ref[:, c:c + 512] = proj(OFF_F + c, 512).astype(BF16)
    kt_ref[...] = _dot_nt(wkt_ref[...], u).astype(BF16)
    fl_ref[...] = proj(OFF_FL, LANES)
    xy_ref[:, 0:LRU_WIDTH] = proj(OFF_C, LRU_WIDTH).astype(BF16)
    xy_ref[:, LRU_WIDTH:] = _gelu_tanh(proj(OFF_C + LRU_WIDTH, LRU_WIDTH)).astype(BF16)
    for c in range(0, G_COLS, 512):
        gt_ref[:, c:c + 512] = _sigmoid(proj(OFF_G + c, 512)).astype(BF16)


def _in_proj(h2, g, w, wkt):
    rows = h2.shape[0]
    tm = next(t for t in (768, 384, BLOCK) if rows % t == 0)
    row_spec = lambda cols: pl.BlockSpec((tm, cols), lambda i: (i, 0))
    return pl.pallas_call(
        _in_proj_kernel,
        grid=(rows // tm,),
        in_specs=[row_spec(D_MODEL), _resident((1, D_MODEL)), _resident((D_MODEL, IN_COLS_PACKED)),
                  _resident((KT_ROWS, D_MODEL))],
        out_specs=[row_spec(A_COLS), row_spec(F_COLS), pl.BlockSpec((KT_ROWS, tm), lambda i: (0, i)),
                   row_spec(LANES), row_spec(C_COLS), row_spec(G_COLS)],
        out_shape=[
            jax.ShapeDtypeStruct((rows, A_COLS), BF16),
            jax.ShapeDtypeStruct((rows, F_COLS), BF16),
            jax.ShapeDtypeStruct((KT_ROWS, rows), BF16),
            jax.ShapeDtypeStruct((rows, LANES), F32),
            jax.ShapeDtypeStruct((rows, C_COLS), BF16),
            jax.ShapeDtypeStruct((rows, G_COLS), BF16),
        ],
        compiler_params=pltpu.CompilerParams(dimension_semantics=("parallel",), vmem_limit_bytes=VMEM_LIMIT),
        name="in_proj",
    )(h2, g, w, wkt)


def _fox_prefix_kernel(fl_ref, fb_ref, ck_ref, cum_ref, *, tp, n_pad):
    lf = _log_sigmoid(fl_ref[0] + fb_ref[...])
    row = lax.broadcasted_iota(jnp.int32, (tp, 1), 0)
    lf = jnp.where(row >= n_pad, lf, 0.0)
    r = lax.broadcasted_iota(jnp.int32, (BLOCK, BLOCK), 0)
    c = lax.broadcasted_iota(jnp.int32, (BLOCK, BLOCK), 1)
    tri = jnp.where(r >= c, 1.0, 0.0).astype(F32)
    carry = jnp.zeros((1, LANES), F32)
    for blk in range(tp // BLOCK):
        sl = slice(blk * BLOCK, (blk + 1) * BLOCK)
        cum = jnp.dot(tri, lf[sl], preferred_element_type=F32, precision=lax.Precision.HIGHEST) + carry
        cum_ref[sl, :] = cum
        carry = cum[BLOCK - 1:BLOCK, :]
    ck_ref[0] = cum_ref[...].T[0:FOX_HEADS, :]


def _fox_prefix(fl3, fb, n_pad):
    b, tp, _ = fl3.shape
    return pl.pallas_call(
        functools.partial(_fox_prefix_kernel, tp=tp, n_pad=n_pad),
        grid=(b,),
        in_specs=[pl.BlockSpec((1, tp, LANES), lambda i: (i, 0, 0)), _resident((1, LANES))],
        out_specs=pl.BlockSpec((1, FOX_HEADS, tp), lambda i: (i, 0, 0)),
        out_shape=jax.ShapeDtypeStruct((b, FOX_HEADS, tp), F32),
        scratch_shapes=[pltpu.VMEM((tp, LANES), F32)],
        compiler_params=pltpu.CompilerParams(dimension_semantics=("parallel",), vmem_limit_bytes=VMEM_LIMIT),
        name="fox_prefix",
    )(fl3, fb)


def _swa_kernel(q_ref, v_ref, vp_ref, kt_ref, ktp_ref, bias_ref, sink_ref, o_ref, *, blocks, n_pad):
    ib = pl.program_id(1)
    half = SWA_Q_HEADS // 2
    rows = half * BLOCK
    lo = lax.broadcasted_iota(jnp.int32, (BLOCK, LANES), 1) < HEAD_DIM
    lo_k = lax.broadcasted_iota(jnp.int32, (2 * BLOCK, LANES), 1) < HEAD_DIM
    lo_o = lax.broadcasted_iota(jnp.int32, (rows, LANES), 1) < HEAD_DIM
    kidx = lax.broadcasted_iota(jnp.int32, (1, BLOCK), 1)
    sink = sink_ref[...]
    for g in range(blocks):
        cur = slice(g * BLOCK, (g + 1) * BLOCK)
        if g == 0:
            kt_prev, v_prev = ktp_ref[...], vp_ref[0]
        else:
            prev = slice((g - 1) * BLOCK, g * BLOCK)
            kt_prev, v_prev = kt_ref[:, prev], v_ref[0, prev, :]
        kt_band = jnp.concatenate([kt_prev, kt_ref[:, cur]], axis=1)
        v_band = jnp.concatenate([v_prev, v_ref[0, cur, :]], axis=0)
        slabs = [q_ref[0, cur, j * LANES:(j + 1) * LANES] for j in range(half)]
        zero = jnp.zeros_like(slabs[0])
        q_all = jnp.concatenate([jnp.where(lo, s, zero) for s in slabs] + [jnp.where(lo, zero, s) for s in slabs],
                                axis=0)
        s = _dot(q_all, kt_band) + bias_ref[...]
        key0 = (ib * blocks + g - 1) * BLOCK + kidx
        s_prev = jnp.where(key0 >= n_pad, s[:, 0:BLOCK], NEG_INF)
        s_cur = jnp.where(key0 + BLOCK >= n_pad, s[:, BLOCK:], NEG_INF)
        m = jnp.maximum(jnp.max(jnp.maximum(s_prev, s_cur), axis=-1, keepdims=True), sink)
        p = jnp.concatenate([jnp.exp2(s_prev - m), jnp.exp2(s_cur - m)], axis=1).astype(BF16)
        pv = _dot(jnp.concatenate([p[0:rows], p[rows:]], axis=1), _pv_operand(v_band, lo_k))
        esink = jnp.exp2(sink - m)
        den = pv[:, LANES:] + jnp.where(lo_o, esink[0:rows], esink[rows:])
        o = (pv[:, 0:LANES] / den).astype(BF16)
        for j in range(half):
            o_ref[0, cur, j * LANES:(j + 1) * LANES] = o[j * BLOCK:(j + 1) * BLOCK]


def _swa(qv3, kt, bias, sinks, n_pad):
    b, tp, _ = qv3.shape
    nb = tp // BLOCK
    blocks = 3 if nb % 3 == 0 else 1
    rows = blocks * BLOCK
    steps = nb // blocks
    vblk = SWA_QCOLS // LANES
    krow = FOX_WIDTH // LANES
    return pl.pallas_call(
        functools.partial(_swa_kernel, blocks=blocks, n_pad=n_pad),
        grid=(b, steps),
        in_specs=[
            pl.BlockSpec((1, rows, SWA_QCOLS), lambda i, j: (i, j, 0)),
            pl.BlockSpec((1, rows, LANES), lambda i, j: (i, j, vblk)),
            pl.BlockSpec((1, BLOCK, LANES), lambda i, j: (i, jnp.maximum(j * blocks - 1, 0), vblk)),
            pl.BlockSpec((LANES, rows), lambda i, j: (krow, i * steps + j)),
            pl.BlockSpec((LANES, BLOCK), lambda i, j: (krow, jnp.maximum((i * steps + j) * blocks - 1, 0))),
            _resident((SWA_Q_HEADS * BLOCK, 2 * BLOCK)),
            _resident((SWA_Q_HEADS * BLOCK, LANES)),
        ],
        out_specs=pl.BlockSpec((1, rows, SWA_QCOLS), lambda i, j: (i, j, 0)),
        out_shape=jax.ShapeDtypeStruct((b, tp, SWA_QCOLS), BF16),
        compiler_params=pltpu.CompilerParams(dimension_semantics=("parallel", "parallel"), vmem_limit_bytes=VMEM_LIMIT),
        name="swa",
    )(qv3, qv3, qv3, kt, kt, bias, sinks)


FOX_TQ = 384
FOX_TK = 512
AUG_ROWS = 16


def _split3(c):
    hi = c.astype(BF16).astype(F32)
    mid = (c - hi).astype(BF16).astype(F32)
    return hi, mid, (c - hi) - mid


def _fox_kernel(q_ref, v_ref, kt_ref, ck_ref, o_ref, qa_s, m_s, acc_s, *, tq, tk, n_tiles, tail, n_pad):
    i = pl.program_id(1)
    pairs = FOX_HEADS // 2
    lane = lax.broadcasted_iota(jnp.int32, (tq, LANES), 1)
    lo = lane < HEAD_DIM
    pats = [jnp.where((lane >= 3 * e) & (lane < 3 * e + 3), 1.0, 0.0).astype(BF16) for e in range(2)]
    for hp in range(pairs):
        q = q_ref[0, :, hp * LANES:(hp + 1) * LANES]
        zero = jnp.zeros_like(q)
        qa_s[hp, 0:tq, 0:LANES] = jnp.where(lo, q, zero)
        qa_s[hp, tq:, 0:LANES] = jnp.where(lo, zero, q)
        qa_s[hp, 0:tq, LANES:] = pats[0]
        qa_s[hp, tq:, LANES:] = pats[1]
    m_s[...] = jnp.full(m_s.shape, NEG_INF, F32)
    acc_s[...] = jnp.zeros(acc_s.shape, F32)

    def tile(start, size, masked):
        sub = lax.broadcasted_iota(jnp.int32, (AUG_ROWS, size), 0)
        key_ok = start + lax.broadcasted_iota(jnp.int32, (1, size), 1) >= n_pad
        lo_k = lax.broadcasted_iota(jnp.int32, (size, LANES), 1) < HEAD_DIM
        if masked:
            rel = lax.broadcasted_iota(jnp.int32, (tq, size), 1) - lax.broadcasted_iota(jnp.int32, (tq, size), 0)
            ok = rel <= i * tq - start
        for hp in range(pairs):
            kt = kt_ref[hp * LANES:(hp + 1) * LANES, pl.ds(start, size)]
            v = v_ref[0, pl.ds(start, size), hp * LANES:(hp + 1) * LANES]
            aug = jnp.zeros((AUG_ROWS, size), F32)
            for e in range(2):
                c = jnp.where(key_ok, ck_ref[0, 2 * hp + e:2 * hp + e + 1, pl.ds(start, size)] * (-LOG2E), NEG_INF)
                for r, piece in enumerate(_split3(c)):
                    aug = jnp.where(sub == 3 * e + r, piece, aug)
            kaug = jnp.concatenate([kt, aug.astype(BF16), jnp.zeros((LANES - AUG_ROWS, size), BF16)], axis=0)
            s2 = _dot(qa_s[hp], kaug)
            ps, alphas = [], []
            for e in range(2):
                s = s2[e * tq:(e + 1) * tq]
                if masked:
                    s = jnp.where(ok, s, NEG_INF)
                chunks = [s[:, c0:c0 + LANES] for c0 in range(0, size, LANES)]
                m_prev = m_s[2 * hp + e]
                m_new = jnp.maximum(m_prev, jnp.max(functools.reduce(jnp.maximum, chunks), axis=-1, keepdims=True))
                alphas.append(jnp.exp2(m_prev - m_new))
                ps.append(jnp.concatenate([jnp.exp2(ch - m_new) for ch in chunks], axis=1).astype(BF16))
                m_s[2 * hp + e] = m_new
            alpha = jnp.where(lo, alphas[0], alphas[1])
            alpha = jnp.concatenate([alpha, alpha], axis=1)
            acc_s[hp] = alpha * acc_s[hp] + _dot(jnp.concatenate(ps, axis=1), _pv_operand(v, lo_k))

    n_last = jnp.minimum((tq * (i + 1) - 1) // tk, n_tiles - 1)
    n_free = (tq * i + 1) // tk

    def full_tile(j, masked):
        tile(pl.multiple_of(j * tk, tk), tk, masked)

    def two_free(j2, carry):
        full_tile(2 * j2, False)
        full_tile(2 * j2 + 1, False)
        return carry

    def one_masked(j, carry):
        full_tile(j, True)
        return carry

    lax.fori_loop(0, n_free // 2, two_free, 0)
    pl.when(n_free % 2 == 1)(lambda: full_tile(n_free - 1, False))
    lax.fori_loop(n_free, n_last, one_masked, 0)
    last_start = pl.multiple_of(n_last * tk, tk)
    reach = jnp.minimum(tq * (i + 1) - n_last * tk, tk)
    for size in range(LANES, tk + 1, LANES):
        pl.when(reach == size)(functools.partial(tile, last_start, size, True))
    if tail:
        pl.when(tq * (i + 1) > n_tiles * tk)(lambda: tile(n_tiles * tk, tail, True))

    for hp in range(pairs):
        acc = acc_s[hp]
        o_ref[0, :, hp * LANES:(hp + 1) * LANES] = (acc[:, 0:LANES] / acc[:, LANES:]).astype(BF16)


def _fox(qv3, kt, ck, n_pad):
    b, tp, _ = qv3.shape
    tq, tk = FOX_TQ, FOX_TK
    n_tiles, tail = tp // tk, tp % tk
    assert tp % tq == 0 and tail % LANES == 0 and tail <= tq and n_tiles >= 1
    pairs = FOX_HEADS // 2
    return pl.pallas_call(
        functools.partial(_fox_kernel, tq=tq, tk=tk, n_tiles=n_tiles, tail=tail, n_pad=n_pad),
        grid=(b, tp // tq),
        in_specs=[
            pl.BlockSpec((1, tq, FOX_WIDTH), lambda i, j: (i, j, 0)),
            pl.BlockSpec((1, tp, FOX_WIDTH), lambda i, j: (i, 0, 1)),
            pl.BlockSpec((FOX_WIDTH, tp), lambda i, j: (0, i)),
            pl.BlockSpec((1, FOX_HEADS, tp), lambda i, j: (i, 0, 0)),
        ],
        out_specs=pl.BlockSpec((1, tq, FOX_WIDTH), lambda i, j: (i, j, 0)),
        out_shape=jax.ShapeDtypeStruct((b, tp, FOX_WIDTH), BF16),
        scratch_shapes=[
            pltpu.VMEM((pairs, 2 * tq, 2 * LANES), BF16),
            pltpu.VMEM((FOX_HEADS, tq, LANES), F32),
            pltpu.VMEM((pairs, tq, 2 * LANES), F32),
        ],
        compiler_params=pltpu.CompilerParams(
            dimension_semantics=("parallel", "arbitrary"), vmem_limit_bytes=VMEM_LIMIT),
        name="fox",
    )(qv3, qv3, kt, ck)


SUBLANES = 8


LRU_SEQS = 2


def _lru_kernel(xy_ref, perm_ref, unperm_ref, cw_ref, cb_ref, wg_ref, bg_ref, lam_ref, o_ref,
                xe_s, a_s, h_s, hist_s, carry, *, seqs, tt, n_pad):
    for q in range(seqs):
        _lru_one(xy_ref.at[q], perm_ref, unperm_ref, cw_ref, cb_ref, wg_ref, bg_ref, lam_ref, o_ref.at[q],
                 xe_s.at[q], a_s.at[q], h_s.at[q], hist_s.at[q], carry.at[q], tt=tt, n_pad=n_pad)


def _lru_one(xy_ref, perm_ref, unperm_ref, cw_ref, cb_ref, wg_ref, bg_ref, lam_ref, o_ref,
             xe_s, a_s, h_s, hist_s, carry, *, tt, n_pad):
    it = pl.program_id(1)
    seg = tt // SUBLANES
    taps = CONV_WIDTH - 1
    w = LRU_WIDTH
    sub = lax.broadcasted_iota(jnp.int32, (SUBLANES, w), 0)

    @pl.when(it == 0)
    def _():
        hist_s[...] = jnp.zeros(hist_s.shape, F32)
        carry[...] = jnp.zeros(carry.shape, F32)

    xyp = _dot(perm_ref[...], xy_ref[...])
    xp, gate = xyp[:, 0:w], xyp[:, w:]

    for v in range(taps):
        src = slice((seg - taps + v) * SUBLANES, (seg - taps + v + 1) * SUBLANES)
        xe_s[v * SUBLANES:(v + 1) * SUBLANES, :] = jnp.where(
            sub == 0, pltpu.roll(hist_s[v], 1, 0), pltpu.roll(xp[src], 1, 0))
        hist_s[v] = xp[src]
    xe_s[taps * SUBLANES:, :] = xp
    conv = xe_s[0:tt, :] * cw_ref[0:1, :]
    for i in range(1, CONV_WIDTH):
        conv = conv + xe_s[i * SUBLANES:i * SUBLANES + tt, :] * cw_ref[i:i + 1, :]
    conv = conv + cb_ref[...]

    z = _dot(conv.astype(BF16), wg_ref[...]) + bg_ref[...]
    r = _sigmoid(z[:, 0:w])
    gi = _sigmoid(z[:, w:])
    log_a = LRU_C * r * _log_sigmoid(lam_ref[...])
    p_idx = lax.broadcasted_iota(jnp.int32, (tt, 1), 0)
    time = it * tt + (p_idx & (SUBLANES - 1)) * seg + (p_idx >> 3)
    a = jnp.exp(log_a)
    one_minus_a2 = -jnp.tanh(log_a) * (a * a + 1.0)
    b = jnp.where(time >= n_pad, jnp.sqrt(one_minus_a2) * (gi * conv), 0.0)

    h_loc = jnp.zeros((SUBLANES, w), F32)
    prod = jnp.ones((SUBLANES, w), F32)
    for v in range(seg):
        rows = slice(v * SUBLANES, (v + 1) * SUBLANES)
        h_loc = a[rows] * h_loc + b[rows]
        prod = a[rows] * prod
        h_s[rows, :] = h_loc
        a_s[rows, :] = prod
    aa, bb = prod, h_loc
    for s in (1, 2, 4):
        keep = sub >= s
        bb = jnp.where(keep, aa * pltpu.roll(bb, s, 0) + bb, bb)
        aa = jnp.where(keep, aa * pltpu.roll(aa, s, 0), aa)
    h_end = aa * carry[...] + bb
    h_in = jnp.where(sub == 0, carry[...], pltpu.roll(h_end, 1, 0))
    carry[...] = h_end[SUBLANES - 1:SUBLANES, :]

    h_in_all = jnp.concatenate([h_in] * seg, axis=0)
    out = ((h_s[...] + a_s[...] * h_in_all) * gate).astype(BF16)
    o_ref[...] = _dot(unperm_ref[...], out).astype(BF16)


def _lru(xy3, cw, cb, wg, bg, lam, n_pad):
    b, tp, _ = xy3.shape
    tt = _row_tile(tp)
    w = LRU_WIDTH
    seg = tt // SUBLANES
    p_idx = jnp.arange(tt)
    perm = (jnp.arange(tt)[None, :] == ((p_idx % SUBLANES) * seg + p_idx // SUBLANES)[:, None]).astype(BF16)
    seqs = LRU_SEQS if b % LRU_SEQS == 0 else 1
    return pl.pallas_call(
        functools.partial(_lru_kernel, seqs=seqs, tt=tt, n_pad=n_pad),
        grid=(b // seqs, tp // tt),
        in_specs=[
            pl.BlockSpec((seqs, tt, C_COLS), lambda i, j: (i, j, 0)),
            _resident((tt, tt)), _resident((tt, tt)),
            _resident((CONV_WIDTH, w)), _resident((1, w)), _resident((w, 2 * w)), _resident((1, 2 * w)),
            _resident((1, w)),
        ],
        out_specs=pl.BlockSpec((seqs, tt, w), lambda i, j: (i, j, 0)),
        out_shape=jax.ShapeDtypeStruct((b, tp, w), BF16),
        scratch_shapes=[
            pltpu.VMEM((seqs, tt + (CONV_WIDTH - 1) * SUBLANES, w), F32), pltpu.VMEM((seqs, tt, w), F32),
            pltpu.VMEM((seqs, tt, w), F32), pltpu.VMEM((seqs, CONV_WIDTH - 1, SUBLANES, w), F32),
            pltpu.VMEM((seqs, 1, w), F32),
        ],
        compiler_params=pltpu.CompilerParams(dimension_semantics=("parallel", "arbitrary"), vmem_limit_bytes=VMEM_LIMIT),
        name="lru",
    )(xy3, perm, perm.T, cw, cb, wg, bg, lam)


FF_CHUNK = 256


def _mix_ffn_rows(h, oa, of, oc, gt_ref, wb_ref, wo_ref, gn_ref, w1_ref, w2_ref, act_s):
    d = D_MODEL
    merged = gt_ref[:, 0:d].astype(F32) * _dot(oa, wb_ref[0])
    merged = merged + gt_ref[:, d:2 * d].astype(F32) * _dot(of, wb_ref[1])
    merged = merged + gt_ref[:, 2 * d:3 * d].astype(F32) * _dot(oc, wb_ref[2])
    hm = h + _dot(merged.astype(BF16), wo_ref[...])
    u = _rms_norm(hm, gn_ref[...]).astype(BF16)
    for c in range(0, D_FF, FF_CHUNK):
        gate = _dot(u, w1_ref[:, c:c + FF_CHUNK])
        up = _dot(u, w1_ref[:, D_FF + c:D_FF + c + FF_CHUNK])
        act_s[:, c:c + FF_CHUNK] = (gate * _sigmoid(gate) * up).astype(BF16)
    return hm + _dot(act_s[...], w2_ref[...])


def _mix_ffn_kernel(h_ref, oa_ref, of_ref, oc_ref, gt_ref, wb_ref, wo_ref, gn_ref, w1_ref, w2_ref, out_ref, act_s,
                    *, tm, n_pad):
    j = pl.program_id(1)
    y = _mix_ffn_rows(h_ref[0], oa_ref[0], of_ref[0], oc_ref[0], gt_ref.at[0], wb_ref, wo_ref, gn_ref, w1_ref, w2_ref,
                      act_s)
    row = j * tm + lax.broadcasted_iota(jnp.int32, (tm, 1), 0)
    out_ref[0] = jnp.where(row >= n_pad, y, 0.0)


def _mix_ffn_last_kernel(h_ref, oa_ref, of_ref, oc_ref, gt_ref, wb_ref, wo_ref, gn_ref, w1_ref, w2_ref, gf_ref,
                         out_ref, act_s):
    y = _mix_ffn_rows(h_ref[...], oa_ref[...], of_ref[...], oc_ref[...], gt_ref, wb_ref, wo_ref, gn_ref, w1_ref,
                      w2_ref, act_s)
    out_ref[0] = _rms_norm(y, gf_ref[...])


def _mix_ffn_last(h3, oa, of, oc, gt3, wb, wo, gn, w1, w2, gf, skip_rows, seq):
    b, tp, d = h3.shape
    tm = 512 if seq % 512 == 0 else BLOCK
    w = LRU_WIDTH

    def rows(cols):
        return pl.BlockSpec((pl.Element(tm), pl.Element(cols)),
                            lambda i, j: (pl.multiple_of(i * tp + skip_rows + j * tm, BLOCK), 0))

    flat = lambda a: a.reshape(b * tp, a.shape[-1])
    return pl.pallas_call(
        _mix_ffn_last_kernel,
        grid=(b, seq // tm),
        in_specs=[
            rows(d), rows(w), rows(w), rows(w), rows(G_COLS),
            _resident((N_BRANCH, w, d)), _resident((d, d)), _resident((1, d)),
            _resident((d, 2 * D_FF)), _resident((D_FF, d)), _resident((1, d)),
        ],
        out_specs=pl.BlockSpec((1, tm, d), lambda i, j: (i, j, 0)),
        out_shape=jax.ShapeDtypeStruct((b, seq, d), F32),
        scratch_shapes=[pltpu.VMEM((tm, D_FF), BF16)],
        compiler_params=pltpu.CompilerParams(dimension_semantics=("parallel", "parallel"), vmem_limit_bytes=VMEM_LIMIT),
        name="mix_ffn_last",
    )(flat(h3), flat(oa), flat(of), flat(oc), flat(gt3), wb, wo, gn, w1, w2, gf)


def _mix_ffn(h3, oa, of, oc, gt3, wb, wo, gn, w1, w2, n_pad):
    b, tp, d = h3.shape
    tm = _row_tile(tp)
    w = LRU_WIDTH
    tile = lambda cols: pl.BlockSpec((1, tm, cols), lambda i, j: (i, j, 0))
    return pl.pallas_call(
        functools.partial(_mix_ffn_kernel, tm=tm, n_pad=n_pad),
        grid=(b, tp // tm),
        in_specs=[
            tile(d), tile(w), tile(w), tile(w), tile(G_COLS),
            _resident((N_BRANCH, w, d)), _resident((d, d)), _resident((1, d)),
            _resident((d, 2 * D_FF)), _resident((D_FF, d)),
        ],
        out_specs=tile(d),
        out_shape=jax.ShapeDtypeStruct((b, tp, d), F32),
        scratch_shapes=[pltpu.VMEM((tm, D_FF), BF16)],
        compiler_params=pltpu.CompilerParams(dimension_semantics=("parallel", "parallel"), vmem_limit_bytes=VMEM_LIMIT),
        name="mix_ffn",
    )(h3, oa, of, oc, gt3, wb, wo, gn, w1, w2)


def _t5_bucket(dist):
    max_exact = REL_BUCKETS // 2
    d = jnp.maximum(dist, 0)
    scaled = jnp.log(jnp.maximum(d, 1).astype(F32) / max_exact) / math.log(REL_MAX_DIST / max_exact)
    large = jnp.minimum(max_exact + (scaled * (REL_BUCKETS - max_exact)).astype(jnp.int32), REL_BUCKETS - 1)
    return jnp.where(d < max_exact, d, large)


def _swa_bias_kernel(bucket_ref, table_ref, o_ref):
    bucket = bucket_ref[...]
    for h in range(SWA_Q_HEADS):
        acc = jnp.full(bucket.shape, NEG_INF, F32)
        for c in range(REL_BUCKETS):
            acc = jnp.where(bucket == c, table_ref[c, h] * LOG2E, acc)
        o_ref[h * BLOCK:(h + 1) * BLOCK, :] = acc


def _swa_bias(rel_table):
    q_idx = jnp.arange(BLOCK)[:, None]
    k_idx = jnp.arange(2 * BLOCK)[None, :]
    dist = q_idx + BLOCK - k_idx
    bucket = jnp.where((dist >= 0) & (dist < SWA_WINDOW), _t5_bucket(dist), -1).astype(jnp.int32)
    return pl.pallas_call(
        _swa_bias_kernel,
        in_specs=[pl.BlockSpec(memory_space=pltpu.VMEM), pl.BlockSpec(memory_space=pltpu.SMEM)],
        out_specs=pl.BlockSpec(memory_space=pltpu.VMEM),
        out_shape=jax.ShapeDtypeStruct((SWA_Q_HEADS * BLOCK, 2 * BLOCK), F32),
        name="swa_bias",
    )(bucket, rel_table.astype(F32))


def _slab_head_order():
    half = SWA_Q_HEADS // 2
    return [h for j in range(half) for h in (j, j + half)]


def _pack_w_in(w_in):
    depth = w_in.shape[0]
    sizes = (512, 128, 128, 512, 512, 512, 8, 512, 512, 3072)
    offs = [0]
    for s in sizes:
        offs.append(offs[-1] + s)
    qa, ka, va, qf, kf, vf, fl, xc, yc, gates = [w_in[:, :, offs[i]:offs[i + 1]] for i in range(len(sizes))]
    scale = HEAD_DIM ** -0.5 * LOG2E
    qa = jnp.concatenate([qa[:, :, h * HEAD_DIM:(h + 1) * HEAD_DIM] for h in _slab_head_order()], axis=-1) * scale
    fl = jnp.pad(fl, ((0, 0), (0, 0), (0, LANES - FOX_HEADS)))
    packed = jnp.concatenate([qa, va, qf * scale, vf, fl, xc, yc, gates], axis=-1)
    keys_t = jnp.swapaxes(jnp.concatenate([kf, ka], axis=-1), 1, 2)
    return packed.astype(BF16), keys_t.astype(BF16)


def _block_diag(w):
    depth, nb, n, _ = w.shape
    eye = jnp.eye(nb, dtype=w.dtype)
    return jnp.einsum('lhij,hk->lhikj', w, eye).reshape(depth, nb * n, nb * n)


def kernel(x, meta_tokens, rel_bias_table, norm_mix, w_in, swa_sinks, fox_forget_bias, conv_w, conv_b,
           lru_w_r, lru_b_r, lru_w_i, lru_b_i, lru_lambda, w_branch, w_out, norm_ffn, w_ffn_in, w_ffn_out,
           norm_final):
    b, seq, d = x.shape
    depth = w_in.shape[0]
    t = N_META + seq
    n_pad = (-t) % BLOCK
    tp = t + n_pad
    assert d == D_MODEL and (n_pad + N_META) % BLOCK == 0

    meta = jnp.broadcast_to(meta_tokens.astype(x.dtype)[None], (b, N_META, d))
    h = jnp.concatenate([jnp.zeros((b, n_pad, d), x.dtype), meta, x], axis=1)

    w_in_p, wkt = _pack_w_in(w_in)
    bias_a = _swa_bias(rel_bias_table)
    sinks =jnp.broadcast_to((swa_sinks.astype(F32) * LOG2E)[:, :, None, None], (depth, SWA_Q_HEADS, BLOCK, LANES))
    sinks = sinks.reshape(depth, SWA_Q_HEADS * BLOCK, LANES)
    fb = jnp.pad(fox_forget_bias.astype(F32), ((0, 0), (0, LANES - FOX_HEADS)))[:, None, :]
    wg = jnp.concatenate([_block_diag(lru_w_r), _block_diag(lru_w_i)], axis=-1).astype(BF16)
    bg = jnp.concatenate([lru_b_r, lru_b_i], axis=-1).astype(F32)[:, None, :]
    wb_a = jnp.concatenate([w_branch[:, 0, h * HEAD_DIM:(h + 1) * HEAD_DIM] for h in _slab_head_order()], axis=1)
    wb = jnp.stack([wb_a, w_branch[:, 1], w_branch[:, 2]], axis=1).astype(BF16)
    wo = w_out.astype(BF16)
    w1 = w_ffn_in.astype(BF16)
    w2 = w_ffn_out.astype(BF16)

    for l in range(depth):
        qva, qvf, kt, fl, xy, gt = _in_proj(h.reshape(b * tp, d), norm_mix[l][None, :], w_in_p[l], wkt[l])
        ck = _fox_prefix(fl.reshape(b, tp, LANES), fb[l], n_pad)
        o_a = _swa(qva.reshape(b, tp, A_COLS), kt, bias_a, sinks[l], n_pad)
        o_f = _fox(qvf.reshape(b, tp, F_COLS), kt, ck, n_pad)
        o_c = _lru(xy.reshape(b, tp, C_COLS), conv_w[l], conv_b[l][None, :], wg[l], bg[l], lru_lambda[l][None, :],
                   n_pad)
        mix_args = (h, o_a, o_f, o_c, gt.reshape(b, tp, G_COLS), wb[l], wo[l], norm_ffn[l][None, :], w1[l], w2[l])
        if l + 1 < depth:
            h = _mix_ffn(*mix_args, n_pad)
    return _mix_ffn_last(*mix_args, norm_final[None, :], n_pad + N_META, seq)
```

```python
import functools
import math

import jax
import jax.numpy as jnp
from jax import lax
from jax.experimental import pallas as pl
from jax.experimental.pallas import tpu as pltpu

F32 = jnp.float32
BF16 = jnp.bfloat16

D_MODEL = 1024
HEAD_DIM = 64
N_META = 16
BLOCK = 128
LANES = 128
NEG_INF = -1e30
LOG2E = math.log2(math.e)
SWA_WINDOW = 128
SWA_Q_HEADS = 8
SWA_KV_HEADS = 2
FOX_HEADS = 8
LRU_WIDTH = D_MODEL // 2
LRU_BLOCKS = 8
CONV_WIDTH = 4
LRU_C = 8.0
REL_BUCKETS = 32
REL_MAX_DIST = 128
D_FF = 2816
N_BRANCH = 3
EPS = 1e-6

SWA_QCOLS = SWA_Q_HEADS * HEAD_DIM
A_COLS = SWA_QCOLS + SWA_KV_HEADS * HEAD_DIM
FOX_WIDTH = FOX_HEADS * HEAD_DIM
F_COLS = 2 * FOX_WIDTH
KT_ROWS = FOX_WIDTH + SWA_KV_HEADS * HEAD_DIM
C_COLS = 2 * LRU_WIDTH
G_COLS = N_BRANCH * D_MODEL
OFF_A = 0
OFF_FL = OFF_A + A_COLS
OFF_F = OFF_FL + LANES
OFF_C = OFF_F + F_COLS
OFF_G = OFF_C + C_COLS
IN_COLS_PACKED = OFF_G + G_COLS

VMEM_LIMIT = 56 * 1024 * 1024


def _sigmoid(x):
    return 0.5 * jnp.tanh(0.5 * x) + 0.5


def _log_sigmoid(x):
    return jnp.minimum(x, 0.0) - jnp.log1p(jnp.exp(-jnp.abs(x)))


def _gelu_tanh(x):
    c = math.sqrt(2.0 / math.pi)
    return 0.5 * x * (1.0 + jnp.tanh(c * (x + 0.044715 * (x * x * x))))


def _rms_norm(x, g):
    ms = jnp.mean(x * x, axis=-1, keepdims=True)
    return x * lax.rsqrt(ms + EPS) * g


def _dot(a, b):
    return jnp.dot(a, b, preferred_element_type=F32)


def _dot_nt(a, b):
    return lax.dot_general(a, b, (((1,), (1,)), ((), ())), preferred_element_type=F32)


def _resident(shape):
    nd = len(shape)
    return pl.BlockSpec(shape, lambda *_: (0,) * nd, pipeline_mode=pl.Buffered(1))


def _row_tile(tp):
    return 384 if tp % 384 == 0 else BLOCK


def _pv_operand(v, lo_k):
    zero = jnp.zeros_like(v)
    ind_lo = jnp.where(lo_k, 1.0, 0.0).astype(BF16)
    ind_hi = jnp.where(lo_k, 0.0, 1.0).astype(BF16)
    top = jnp.concatenate([jnp.where(lo_k, v, zero), ind_lo], axis=1)
    bot = jnp.concatenate([jnp.where(lo_k, zero, v), ind_hi], axis=1)
    return jnp.concatenate([top, bot], axis=0)


def _in_proj_kernel(x_ref, g_ref, w_ref, wkt_ref, qa_ref, qf_ref, kt_ref, fl_ref, xy_ref, gt_ref):
    u = _rms_norm(x_ref[...], g_ref[...]).astype(BF16)

    def proj(c0, width):
        return _dot(u, w_ref[:, c0:c0 + width])

    qa_ref[:, 0:SWA_QCOLS] = proj(OFF_A, SWA_QCOLS).astype(BF16)
    v_fl = proj(OFF_A + SWA_QCOLS, 2 * LANES)
    qa_ref[:, SWA_QCOLS:] = v_fl[:, 0:LANES].astype(BF16)
    fl_ref[...] = v_fl[:, LANES:]
    for c in range(0, F_COLS, 512):
        qf_ref[:, c:c + 512] = proj(OFF_F + c, 512).astype(BF16)
    kt_ref[...] = _dot_nt(wkt_ref[...], u).astype(BF16)
    xy_ref[:, 0:LRU_WIDTH] = proj(OFF_C, LRU_WIDTH).astype(BF16)
    xy_ref[:, LRU_WIDTH:] = _gelu_tanh(proj(OFF_C + LRU_WIDTH, LRU_WIDTH)).astype(BF16)
    for c in range(0, G_COLS, 512):
        gt_ref[:, c:c + 512] = _sigmoid(proj(OFF_G + c, 512)).astype(BF16)


def _in_proj(h2, g, w, wkt):
    rows = h2.shape[0]
    tm = next(t for t in (768, 384, BLOCK) if rows % t == 0)
    row_spec = lambda cols: pl.BlockSpec((tm, cols), lambda i: (i, 0))
    return pl.pallas_call(
        _in_proj_kernel,
        grid=(rows // tm,),
        in_specs=[row_spec(D_MODEL), _resident((1, D_MODEL)), _resident((D_MODEL, IN_COLS_PACKED)),
                  _resident((KT_ROWS, D_MODEL))],
        out_specs=[row_spec(A_COLS), row_spec(F_COLS), pl.BlockSpec((KT_ROWS, tm), lambda i: (0, i)),
                   row_spec(LANES), row_spec(C_COLS), row_spec(G_COLS)],
        out_shape=[
            jax.ShapeDtypeStruct((rows, A_COLS), BF16),
            jax.ShapeDtypeStruct((rows, F_COLS), BF16),
            jax.ShapeDtypeStruct((KT_ROWS, rows), BF16),
            jax.ShapeDtypeStruct((rows, LANES), F32),
            jax.ShapeDtypeStruct((rows, C_COLS), BF16),
            jax.ShapeDtypeStruct((rows, G_COLS), BF16),
        ],
        compiler_params=pltpu.CompilerParams(dimension_semantics=("parallel",), vmem_limit_bytes=VMEM_LIMIT),
        name="in_proj",
    )(h2, g, w, wkt)


def _fox_prefix_kernel(fl_ref, fb_ref, ck_ref, cum_ref, *, tp, n_pad):
    lf = _log_sigmoid(fl_ref[0] + fb_ref[...])
    row = lax.broadcasted_iota(jnp.int32, (tp, 1), 0)
    lf = jnp.where(row >= n_pad, lf, 0.0)
    r = lax.broadcasted_iota(jnp.int32, (BLOCK, BLOCK), 0)
    c = lax.broadcasted_iota(jnp.int32, (BLOCK, BLOCK), 1)
    tri = jnp.where(r >= c, 1.0, 0.0).astype(F32)
    carry = jnp.zeros((1, LANES), F32)
    for blk in range(tp // BLOCK):
        sl = slice(blk * BLOCK, (blk + 1) * BLOCK)
        cum = jnp.dot(tri, lf[sl], preferred_element_type=F32, precision=lax.Precision.HIGHEST) + carry
        cum_ref[sl, :] = cum
        carry = cum[BLOCK - 1:BLOCK, :]
    ck_ref[0] = cum_ref[...].T[0:FOX_HEADS, :]


def _fox_prefix(fl3, fb, n_pad):
    b, tp, _ = fl3.shape
    return pl.pallas_call(
        functools.partial(_fox_prefix_kernel, tp=tp, n_pad=n_pad),
        grid=(b,),
        in_specs=[pl.BlockSpec((1, tp, LANES), lambda i: (i, 0, 0)), _resident((1, LANES))],
        out_specs=pl.BlockSpec((1, FOX_HEADS, tp), lambda i: (i, 0, 0)),
        out_shape=jax.ShapeDtypeStruct((b, FOX_HEADS, tp), F32),
        scratch_shapes=[pltpu.VMEM((tp, LANES), F32)],
        compiler_params=pltpu.CompilerParams(dimension_semantics=("parallel",), vmem_limit_bytes=VMEM_LIMIT),
        name="fox_prefix",
    )(fl3, fb)


def _swa_kernel(q_ref, v_ref, vp_ref, kt_ref, ktp_ref, bias_ref, sink_ref, o_ref, *, blocks, n_pad):
    ib = pl.program_id(1)
    half = SWA_Q_HEADS // 2
    rows = half * BLOCK
    lo = lax.broadcasted_iota(jnp.int32, (BLOCK, LANES), 1) < HEAD_DIM
    lo_k = lax.broadcasted_iota(jnp.int32, (2 * BLOCK, LANES), 1) < HEAD_DIM
    lo_o = lax.broadcasted_iota(jnp.int32, (rows, LANES), 1) < HEAD_DIM
    kidx = lax.broadcasted_iota(jnp.int32, (1, BLOCK), 1)
    sink = sink_ref[...]
    for g in range(blocks):
        cur = slice(g * BLOCK, (g + 1) * BLOCK)
        if g == 0:
            kt_prev, v_prev = ktp_ref[...], vp_ref[0]
        else:
            prev = slice((g - 1) * BLOCK, g * BLOCK)
            kt_prev, v_prev = kt_ref[:, prev], v_ref[0, prev, :]
        kt_band = jnp.concatenate([kt_prev, kt_ref[:, cur]], axis=1)
        v_band = jnp.concatenate([v_prev, v_ref[0, cur, :]], axis=0)
        slabs = [q_ref[0, cur, j * LANES:(j + 1) * LANES] for j in range(half)]
        zero = jnp.zeros_like(slabs[0])
        q_all = jnp.concatenate([jnp.where(lo, s, zero) for s in slabs] + [jnp.where(lo, zero, s) for s in slabs],
                                axis=0)
        s = _dot(q_all, kt_band) + bias_ref[...]
        key0 = (ib * blocks + g - 1) * BLOCK + kidx
        s_prev = jnp.where(key0 >= n_pad, s[:, 0:BLOCK], NEG_INF)
        s_cur = jnp.where(key0 + BLOCK >= n_pad, s[:, BLOCK:], NEG_INF)
        m = jnp.maximum(jnp.max(jnp.maximum(s_prev, s_cur), axis=-1, keepdims=True), sink)
        p = jnp.concatenate([jnp.exp2(s_prev - m), jnp.exp2(s_cur - m)], axis=1).astype(BF16)
        pv = _dot(jnp.concatenate([p[0:rows], p[rows:]], axis=1), _pv_operand(v_band, lo_k))
        esink = jnp.exp2(sink - m)
        den = pv[:, LANES:] + jnp.where(lo_o, esink[0:rows], esink[rows:])
        o = (pv[:, 0:LANES] / den).astype(BF16)
        for j in range(half):
            o_ref[0, cur, j * LANES:(j + 1) * LANES] = o[j * BLOCK:(j + 1) * BLOCK]


def _swa(qv3, kt, bias, sinks, n_pad):
    b, tp, _ = qv3.shape
    nb = tp // BLOCK
    blocks = 3 if nb % 3 == 0 else 1
    rows = blocks * BLOCK
    steps = nb // blocks
    vblk = SWA_QCOLS // LANES
    krow = FOX_WIDTH // LANES
    return pl.pallas_call(
        functools.partial(_swa_kernel, blocks=blocks, n_pad=n_pad),
        grid=(b, steps),
        in_specs=[
            pl.BlockSpec((1, rows, SWA_QCOLS), lambda i, j: (i, j, 0)),
            pl.BlockSpec((1, rows, LANES), lambda i, j: (i, j, vblk)),
            pl.BlockSpec((1, BLOCK, LANES), lambda i, j: (i, jnp.maximum(j * blocks - 1, 0), vblk)),
            pl.BlockSpec((LANES, rows), lambda i, j: (krow, i * steps + j)),
            pl.BlockSpec((LANES, BLOCK), lambda i, j: (krow, jnp.maximum((i * steps + j) * blocks - 1, 0))),
            _resident((SWA_Q_HEADS * BLOCK, 2 * BLOCK)),
            _resident((SWA_Q_HEADS * BLOCK, LANES)),
        ],
        out_specs=pl.BlockSpec((1, rows, SWA_QCOLS), lambda i, j: (i, j, 0)),
        out_shape=jax.ShapeDtypeStruct((b, tp, SWA_QCOLS), BF16),
        compiler_params=pltpu.CompilerParams(dimension_semantics=("parallel", "parallel"), vmem_limit_bytes=VMEM_LIMIT),
        name="swa",
    )(qv3, qv3, qv3, kt, kt, bias, sinks)


FOX_TQ = 384
FOX_TK = 512
AUG_ROWS = 16

def _split3(c):
    hi = c.astype(BF16).astype(F32)
    mid = (c - hi).astype(BF16).astype(F32)
    return hi, mid, (c - hi) - mid


def _fox_kernel(q_ref, v_ref, kt_ref, ck_ref, o_ref, qa_s, m_s, acc_s, *, tq, tk, n_tiles, tail, n_pad):
    i = pl.program_id(1)
    pairs = FOX_HEADS // 2
    lane = lax.broadcasted_iota(jnp.int32, (tq, LANES), 1)
    lo = lane < HEAD_DIM
    pats = [jnp.where((lane >= 3 * e) & (lane < 3 * e + 3), 1.0, 0.0).astype(BF16) for e in range(2)]
    for hp in range(pairs):
        q = q_ref[0, :, hp * LANES:(hp + 1) * LANES]
        zero = jnp.zeros_like(q)
        qa_s[hp, 0:tq, 0:LANES] = jnp.where(lo, q, zero)
        qa_s[hp, tq:, 0:LANES] = jnp.where(lo, zero, q)
        qa_s[hp, 0:tq, LANES:] = pats[0]
        qa_s[hp, tq:, LANES:] = pats[1]
    m_s[...] = jnp.full(m_s.shape, NEG_INF, F32)
    acc_s[...] = jnp.zeros(acc_s.shape, F32)

    def tile(start, size, masked):
        sub = lax.broadcasted_iota(jnp.int32, (AUG_ROWS, size), 0)
        key_ok = start + lax.broadcasted_iota(jnp.int32, (1, size), 1) >= n_pad
        lo_k = lax.broadcasted_iota(jnp.int32, (size, LANES), 1) < HEAD_DIM
        if masked:
            rel = lax.broadcasted_iota(jnp.int32, (tq, size), 1) - lax.broadcasted_iota(jnp.int32, (tq, size), 0)
            ok = rel <= i * tq - start
        for hp in range(pairs):
            kt = kt_ref[hp * LANES:(hp + 1) * LANES, pl.ds(start, size)]
            aug = jnp.zeros((AUG_ROWS, size), F32)
            for e in range(2):
                c = jnp.where(key_ok, ck_ref[0, 2 * hp + e:2 * hp + e + 1, pl.ds(start, size)] * (-LOG2E), NEG_INF)
                for r, piece in enumerate(_split3(c)):
                    aug = jnp.where(sub == 3 * e + r, piece, aug)
            kaug = jnp.concatenate([kt, aug.astype(BF16), jnp.zeros((LANES - AUG_ROWS, size), BF16)], axis=0)
            s2 = _dot(qa_s[hp], kaug)
            ps, alphas = [], []
            for e in range(2):
                s = s2[e * tq:(e + 1) * tq]
                if masked:
                    s = jnp.where(ok, s, NEG_INF)
                chunks = [s[:, c0:c0 + LANES] for c0 in range(0, size, LANES)]
                m_prev = m_s[2 * hp + e]
                m_new = jnp.maximum(m_prev, jnp.max(functools.reduce(jnp.maximum, chunks), axis=-1, keepdims=True))
                alphas.append(jnp.exp2(m_prev - m_new))
                ps.append(jnp.concatenate([jnp.exp2(ch - m_new) for ch in chunks], axis=1).astype(BF16))
                m_s[2 * hp + e] = m_new
            alpha = jnp.where(lo, alphas[0], alphas[1])
            alpha = jnp.concatenate([alpha, alpha], axis=1)
            v = v_ref[0, pl.ds(start, size), hp * LANES:(hp + 1) * LANES]
            acc_s[hp] = alpha * acc_s[hp] + _dot(jnp.concatenate(ps, axis=1), _pv_operand(v, lo_k))

    n_last = jnp.minimum((tq * (i + 1) - 1) // tk, n_tiles - 1)
    n_free = (tq * i + 1) // tk

    def full_tile(j, masked):
        tile(pl.multiple_of(j * tk, tk), tk, masked)

    def four_free(j4, carry):
        for u in range(4):
            full_tile(4 * j4 + u, False)
        return carry

    def one_masked(j, carry):
        full_tile(j, True)
        return carry

    def two_free(j):
        full_tile(j, False)
        full_tile(j + 1, False)

    lax.fori_loop(0, n_free // 4, four_free, 0)
    rest = n_free - n_free % 4
    pl.when(n_free % 4 >= 2)(lambda: two_free(rest))
    pl.when(n_free % 2 == 1)(lambda: full_tile(n_free - 1, False))
    lax.fori_loop(n_free, n_last, one_masked, 0)
    last_start = pl.multiple_of(n_last * tk, tk)
    reach = jnp.minimum(tq * (i + 1) - n_last * tk, tk)
    for size in range(LANES, tk + 1, LANES):
        pl.when(reach == size)(functools.partial(tile, last_start, size, True))
    if tail:
        pl.when(tq * (i + 1) > n_tiles * tk)(lambda: tile(n_tiles * tk, tail, True))

    for hp in range(pairs):
        acc = acc_s[hp]
        o_ref[0, :, hp * LANES:(hp + 1) * LANES] = (acc[:, 0:LANES] / acc[:, LANES:]).astype(BF16)


def _fox(qv3, kt, ck, n_pad):
    b, tp, _ = qv3.shape
    tq, tk = FOX_TQ, FOX_TK
    n_tiles, tail = tp // tk, tp % tk
    assert tp % tq == 0 and tail % LANES == 0 and tail <= tq and n_tiles >= 1
    pairs = FOX_HEADS // 2
    return pl.pallas_call(
        functools.partial(_fox_kernel, tq=tq, tk=tk, n_tiles=n_tiles, tail=tail, n_pad=n_pad),
        grid=(b, tp // tq),
        in_specs=[
            pl.BlockSpec((1, tq, FOX_WIDTH), lambda i, j: (i, j, 0)),
            pl.BlockSpec((1, tp, FOX_WIDTH), lambda i, j: (i, 0, 1)),
            pl.BlockSpec((FOX_WIDTH, tp), lambda i, j: (0, i)),
            pl.BlockSpec((1, FOX_HEADS, tp), lambda i, j: (i, 0, 0)),
        ],
        out_specs=pl.BlockSpec((1, tq, FOX_WIDTH), lambda i, j: (i, j, 0)),
        out_shape=jax.ShapeDtypeStruct((b, tp, FOX_WIDTH), BF16),
        scratch_shapes=[
            pltpu.VMEM((pairs, 2 * tq, 2 * LANES), BF16),
            pltpu.VMEM((FOX_HEADS, tq, LANES), F32),
            pltpu.VMEM((pairs, tq, 2 * LANES), F32),
        ],
        compiler_params=pltpu.CompilerParams(
            dimension_semantics=("parallel", "arbitrary"), vmem_limit_bytes=VMEM_LIMIT),
        name="fox",
    )(qv3, qv3, kt, ck)


SUBLANES = 8


LRU_SEQS = 2


def _lru_kernel(xy_ref, perm_ref, unperm_ref, cw_ref, cb_ref, wg_ref, bg_ref, lam_ref, o_ref,
                xe_s, a_s, h_s, hist_s, carry, *, seqs, tt, n_pad):
    @pl.when(pl.program_id(1) == 0)
    def _():
        hist_s[...] = jnp.zeros(hist_s.shape, F32)
        carry[...] = jnp.zeros(carry.shape, F32)

    chunk = BLOCK if tt % BLOCK == 0 else tt
    for q in range(seqs):
        _lru_one(xy_ref.at[q], perm_ref, unperm_ref, cw_ref, cb_ref, wg_ref, bg_ref, lam_ref, o_ref.at[q],
                 xe_s.at[q], a_s.at[q], h_s.at[q], hist_s.at[q], carry.at[q], tt=tt, chunk=chunk, n_pad=n_pad)


def _lru_one(xy_ref, perm_ref, unperm_ref, cw_ref, cb_ref, wg_ref, bg_ref, lam_ref, o_ref,
             xe_s, a_s, h_s, hist_s, carry, *, tt, chunk, n_pad):
    it = pl.program_id(1)
    seg = tt // SUBLANES
    taps = CONV_WIDTH - 1
    w = LRU_WIDTH
    sub = lax.broadcasted_iota(jnp.int32, (SUBLANES, w), 0)
    xy = xy_ref[...]
    log_sig_lam = _log_sigmoid(lam_ref[...])

    last = _dot(perm_ref[(seg - taps) * SUBLANES:, :], xy[:, 0:w])
    for v in range(taps):
        src = last[v * SUBLANES:(v + 1) * SUBLANES]
        xe_s[v * SUBLANES:(v + 1) * SUBLANES, :] = jnp.where(
            sub == 0, pltpu.roll(hist_s[v], 1, 0), pltpu.roll(src, 1, 0))
        hist_s[v] = src

    h_loc = jnp.zeros((SUBLANES, w), F32)
    prod = jnp.ones((SUBLANES, w), F32)
    gates = []
    for r0 in range(0, tt, chunk):
        xyp = _dot(perm_ref[r0:r0 + chunk, :], xy)
        gates.append(xyp[:, w:])
        xe_s[taps * SUBLANES + r0:taps * SUBLANES + r0 + chunk, :] = xyp[:, 0:w]
        conv = xe_s[r0:r0 + chunk, :] * cw_ref[0:1, :]
        for i in range(1, CONV_WIDTH):
            conv = conv + xe_s[i * SUBLANES + r0:i * SUBLANES + r0 + chunk, :] * cw_ref[i:i + 1, :]
        conv = conv + cb_ref[...]

        z = _dot(conv.astype(BF16), wg_ref[...]) + bg_ref[...]
        r = _sigmoid(z[:, 0:w])
        gi = _sigmoid(z[:, w:])
        log_a = LRU_C * r * log_sig_lam
        p_idx = r0 + lax.broadcasted_iota(jnp.int32, (chunk, 1), 0)
        time = it * tt + (p_idx & (SUBLANES - 1)) * seg + (p_idx >> 3)
        a = jnp.exp(log_a)
        one_minus_a2 = -jnp.tanh(log_a) * (a * a + 1.0)
        b = jnp.where(time >= n_pad, jnp.sqrt(one_minus_a2) * (gi * conv), 0.0)
        for v in range(chunk // SUBLANES):
            rows = slice(v * SUBLANES, (v + 1) * SUBLANES)
            h_loc = a[rows] * h_loc + b[rows]
            prod = a[rows] * prod
            h_s[r0 + v * SUBLANES:r0 + (v + 1) * SUBLANES, :] = h_loc
            a_s[r0 + v * SUBLANES:r0 + (v + 1) * SUBLANES, :] = prod
    gate = jnp.concatenate(gates, axis=0)
    aa, bb = prod, h_loc
    for s in (1, 2, 4):
        keep = sub >= s
        bb = jnp.where(keep, aa * pltpu.roll(bb, s, 0) + bb, bb)
        aa = jnp.where(keep, aa * pltpu.roll(aa, s, 0), aa)
    h_end = aa * carry[...] + bb
    h_in = jnp.where(sub == 0, carry[...], pltpu.roll(h_end, 1, 0))
    carry[...] = h_end[SUBLANES - 1:SUBLANES, :]

    h_in_all = jnp.concatenate([h_in] * seg, axis=0)
    out = ((h_s[...] + a_s[...] * h_in_all) * gate).astype(BF16)
    o_ref[...] = _dot(unperm_ref[...], out).astype(BF16)


def _lru(xy3, cw, cb, wg, bg, lam, n_pad):
    b, tp, _ = xy3.shape
    tt = _row_tile(tp)
    w = LRU_WIDTH
    seg = tt // SUBLANES
    p_idx = jnp.arange(tt)
    perm = (jnp.arange(tt)[None, :] == ((p_idx % SUBLANES) * seg + p_idx // SUBLANES)[:, None]).astype(BF16)
    seqs = LRU_SEQS if b % LRU_SEQS == 0 else 1
    return pl.pallas_call(
        functools.partial(_lru_kernel, seqs=seqs, tt=tt, n_pad=n_pad),
        grid=(b // seqs, tp // tt),
        in_specs=[
            pl.BlockSpec((seqs, tt, C_COLS), lambda i, j: (i, j, 0)),
            _resident((tt, tt)), _resident((tt, tt)),
            _resident((CONV_WIDTH, w)), _resident((1, w)), _resident((w, 2 * w)), _resident((1, 2 * w)),
            _resident((1, w)),
        ],
        out_specs=pl.BlockSpec((seqs, tt, w), lambda i, j: (i, j, 0)),
        out_shape=jax.ShapeDtypeStruct((b, tp, w), BF16),
        scratch_shapes=[
            pltpu.VMEM((seqs, tt + (CONV_WIDTH - 1) * SUBLANES, w), F32), pltpu.VMEM((seqs, tt, w), F32),
            pltpu.VMEM((seqs, tt, w), F32), pltpu.VMEM((seqs, CONV_WIDTH - 1, SUBLANES, w), F32),
            pltpu.VMEM((seqs, 1, w), F32),
        ],
        compiler_params=pltpu.CompilerParams(dimension_semantics=("parallel", "arbitrary"), vmem_limit_bytes=VMEM_LIMIT),
        name="lru",
    )(xy3, perm, perm.T, cw, cb, wg, bg, lam)


FF_CHUNK = 256


def _mix_ffn_rows(h, oa, of, oc, gt_ref, wb_ref, wo_ref, gn_ref, w1_ref, w2_ref, act_s):
    d = D_MODEL
    merged = gt_ref[:, 0:d].astype(F32) * _dot(oa, wb_ref[0])
    merged = merged + gt_ref[:, d:2 * d].astype(F32) * _dot(of, wb_ref[1])
    merged = merged + gt_ref[:, 2 * d:3 * d].astype(F32) * _dot(oc, wb_ref[2])
    hm = h + _dot(merged.astype(BF16), wo_ref[...])
    u = _rms_norm(hm, gn_ref[...]).astype(BF16)
    for c in range(0, D_FF, FF_CHUNK):
        gate = _dot(u, w1_ref[:, c:c + FF_CHUNK])
        up = _dot(u, w1_ref[:, D_FF + c:D_FF + c + FF_CHUNK])
        act_s[:, c:c + FF_CHUNK] = (gate * _sigmoid(gate) * up).astype(BF16)
    return hm + _dot(act_s[...], w2_ref[...])


def _mix_ffn_kernel(h_ref, oa_ref, of_ref, oc_ref, gt_ref, wb_ref, wo_ref, gn_ref, w1_ref, w2_ref, out_ref, act_s,
                    *, tm, n_pad):
    j = pl.program_id(1)
    y = _mix_ffn_rows(h_ref[0], oa_ref[0], of_ref[0], oc_ref[0], gt_ref.at[0], wb_ref, wo_ref, gn_ref, w1_ref, w2_ref,
                      act_s)
    row = j * tm + lax.broadcasted_iota(jnp.int32, (tm, 1), 0)
    out_ref[0] = jnp.where(row >= n_pad, y, 0.0)


def _mix_ffn_last_kernel(h_ref, oa_ref, of_ref, oc_ref, gt_ref, wb_ref, wo_ref, gn_ref, w1_ref, w2_ref, gf_ref,
                         out_ref, act_s):
    y = _mix_ffn_rows(h_ref[...], oa_ref[...], of_ref[...], oc_ref[...], gt_ref, wb_ref, wo_ref, gn_ref, w1_ref,
                      w2_ref, act_s)
    out_ref[0] = _rms_norm(y, gf_ref[...])


def _mix_ffn_last(h3, oa, of, oc, gt3, wb, wo, gn, w1, w2, gf, skip_rows, seq):
    b, tp, d = h3.shape
    tm = 512 if seq % 512 == 0 else BLOCK
    w = LRU_WIDTH

    def rows(cols):
        return pl.BlockSpec((pl.Element(tm), pl.Element(cols)),
                            lambda i, j: (pl.multiple_of(i * tp + skip_rows + j * tm, BLOCK), 0))

    flat = lambda a: a.reshape(b * tp, a.shape[-1])
    return pl.pallas_call(
        _mix_ffn_last_kernel,
        grid=(b, seq // tm),
        in_specs=[
            rows(d), rows(w), rows(w), rows(w), rows(G_COLS),
            _resident((N_BRANCH, w, d)), _resident((d, d)), _resident((1, d)),
            _resident((d, 2 * D_FF)), _resident((D_FF, d)), _resident((1, d)),
        ],
        out_specs=pl.BlockSpec((1, tm, d), lambda i, j: (i, j, 0)),
        out_shape=jax.ShapeDtypeStruct((b, seq, d), F32),
        scratch_shapes=[pltpu.VMEM((tm, D_FF), BF16)],
        compiler_params=pltpu.CompilerParams(dimension_semantics=("parallel", "parallel"), vmem_limit_bytes=VMEM_LIMIT),
        name="mix_ffn_last",
    )(flat(h3), flat(oa), flat(of), flat(oc), flat(gt3), wb, wo, gn, w1, w2, gf)


def _mix_ffn(h3, oa, of, oc, gt3, wb, wo, gn, w1, w2, n_pad):
    b, tp, d = h3.shape
    tm = _row_tile(tp)
    w = LRU_WIDTH
    tile = lambda cols: pl.BlockSpec((1, tm, cols), lambda i, j: (i, j, 0))
    return pl.pallas_call(
        functools.partial(_mix_ffn_kernel, tm=tm, n_pad=n_pad),
        grid=(b, tp // tm),
        in_specs=[
            tile(d), tile(w), tile(w), tile(w), tile(G_COLS),
            _resident((N_BRANCH, w, d)), _resident((d, d)), _resident((1, d)),
            _resident((d, 2 * D_FF)), _resident((D_FF, d)),
        ],
        out_specs=tile(d),
        out_shape=jax.ShapeDtypeStruct((b, tp, d), F32),
        scratch_shapes=[pltpu.VMEM((tm, D_FF), BF16)],
        compiler_params=pltpu.CompilerParams(dimension_semantics=("parallel", "parallel"), vmem_limit_bytes=VMEM_LIMIT),
        name="mix_ffn",
    )(h3, oa, of, oc, gt3, wb, wo, gn, w1, w2)


def _t5_bucket(dist):
    max_exact = REL_BUCKETS // 2
    d = jnp.maximum(dist, 0)
    scaled = jnp.log(jnp.maximum(d, 1).astype(F32) / max_exact) / math.log(REL_MAX_DIST / max_exact)
    large = jnp.minimum(max_exact + (scaled * (REL_BUCKETS - max_exact)).astype(jnp.int32), REL_BUCKETS - 1)
    return jnp.where(d < max_exact, d, large)


def _swa_bias_kernel(bucket_ref, table_ref, o_ref):
    bucket = bucket_ref[...]
    for h in range(SWA_Q_HEADS):
        acc = jnp.full(bucket.shape, NEG_INF, F32)
        for c in range(REL_BUCKETS):
            acc = jnp.where(bucket == c, table_ref[c, h] * LOG2E, acc)
        o_ref[h * BLOCK:(h + 1) * BLOCK, :] = acc


def _swa_bias(rel_table):
    q_idx = jnp.arange(BLOCK)[:, None]
    k_idx = jnp.arange(2 * BLOCK)[None, :]
    dist = q_idx + BLOCK - k_idx
    bucket = jnp.where((dist >= 0) & (dist < SWA_WINDOW), _t5_bucket(dist), -1).astype(jnp.int32)
    return pl.pallas_call(
        _swa_bias_kernel,
        in_specs=[pl.BlockSpec(memory_space=pltpu.VMEM), pl.BlockSpec(memory_space=pltpu.SMEM)],
        out_specs=pl.BlockSpec(memory_space=pltpu.VMEM),
        out_shape=jax.ShapeDtypeStruct((SWA_Q_HEADS * BLOCK, 2 * BLOCK), F32),
        name="swa_bias",
    )(bucket, rel_table.astype(F32))


def _slab_head_order():
    half = SWA_Q_HEADS // 2
    return [h for j in range(half) for h in (j, j + half)]


def _pack_w_in(w_in):
    depth = w_in.shape[0]
    sizes = (512, 128, 128, 512, 512, 512, 8, 512, 512, 3072)
    offs = [0]
    for s in sizes:
        offs.append(offs[-1] + s)
    qa, ka, va, qf, kf, vf, fl, xc, yc, gates = [w_in[:, :, offs[i]:offs[i + 1]] for i in range(len(sizes))]
    scale = HEAD_DIM ** -0.5 * LOG2E
    qa = jnp.concatenate([qa[:, :, h * HEAD_DIM:(h + 1) * HEAD_DIM] for h in _slab_head_order()], axis=-1) * scale
    fl = jnp.pad(fl, ((0, 0), (0, 0), (0, LANES - FOX_HEADS)))
    packed = jnp.concatenate([qa, va, fl, qf * scale, vf, xc, yc, gates], axis=-1)
    keys_t = jnp.swapaxes(jnp.concatenate([kf, ka], axis=-1), 1, 2)
    return packed.astype(BF16), keys_t.astype(BF16)


def _block_diag(w):
    depth, nb, n, _ = w.shape
    eye = jnp.eye(nb, dtype=w.dtype)
    return jnp.einsum('lhij,hk->lhikj', w, eye).reshape(depth, nb * n, nb * n)


def kernel(x, meta_tokens, rel_bias_table, norm_mix, w_in, swa_sinks, fox_forget_bias, conv_w, conv_b,
           lru_w_r, lru_b_r, lru_w_i, lru_b_i, lru_lambda, w_branch, w_out, norm_ffn, w_ffn_in, w_ffn_out,
           norm_final):
    b, seq, d = x.shape
    depth = w_in.shape[0]
    t = N_META + seq
    n_pad = (-t) % BLOCK
    tp = t + n_pad
    assert d == D_MODEL and (n_pad + N_META) % BLOCK == 0

    meta = jnp.broadcast_to(meta_tokens.astype(x.dtype)[None], (b, N_META, d))
    h = jnp.concatenate([jnp.zeros((b, n_pad, d), x.dtype), meta, x], axis=1)

    w_in_p, wkt = _pack_w_in(w_in)
    bias_a = _swa_bias(rel_bias_table)
    sinks =jnp.broadcast_to((swa_sinks.astype(F32) * LOG2E)[:, :, None, None], (depth, SWA_Q_HEADS, BLOCK, LANES))
    sinks = sinks.reshape(depth, SWA_Q_HEADS * BLOCK, LANES)
    fb = jnp.pad(fox_forget_bias.astype(F32), ((0, 0), (0, LANES - FOX_HEADS)))[:, None, :]
    wg = jnp.concatenate([_block_diag(lru_w_r), _block_diag(lru_w_i)], axis=-1).astype(BF16)
    bg = jnp.concatenate([lru_b_r, lru_b_i], axis=-1).astype(F32)[:, None, :]
    wb_a = jnp.concatenate([w_branch[:, 0, h * HEAD_DIM:(h + 1) * HEAD_DIM] for h in _slab_head_order()], axis=1)
    wb = jnp.stack([wb_a, w_branch[:, 1], w_branch[:, 2]], axis=1).astype(BF16)
    wo = w_out.astype(BF16)
    w1 = w_ffn_in.astype(BF16)
    w2 = w_ffn_out.astype(BF16)

    for l in range(depth):
        qva, qvf, kt, fl, xy, gt = _in_proj(h.reshape(b * tp, d), norm_mix[l][None, :], w_in_p[l], wkt[l])
        ck = _fox_prefix(fl.reshape(b, tp, LANES), fb[l], n_pad)
        o_a = _swa(qva.reshape(b, tp, A_COLS), kt, bias_a, sinks[l], n_pad)
        o_f = _fox(qvf.reshape(b, tp, F_COLS), kt, ck, n_pad)
        o_c = _lru(xy.reshape(b, tp, C_COLS), conv_w[l], conv_b[l][None, :], wg[l], bg[l], lru_lambda[l][None, :],
                   n_pad)
        mix_args = (h, o_a, o_f, o_c, gt.reshape(b, tp, G_COLS), wb[l], wo[l], norm_ffn[l][None, :], w1[l], w2[l])
        if l + 1 < depth:
            h = _mix_ffn(*mix_args, n_pad)
    return _mix_ffn_last(*mix_args, norm_final[None, :], n_pad + N_META, seq)
```

```python
import functools
import math

import jax
import jax.numpy as jnp
from jax import lax
from jax.experimental import pallas as pl
from jax.experimental.pallas import tpu as pltpu

F32 = jnp.float32
BF16 = jnp.bfloat16

D_MODEL = 1024
HEAD_DIM = 64
N_META = 16
BLOCK = 128
LANES = 128
NEG_INF = -1e30
LOG2E = math.log2(math.e)
SWA_WINDOW = 128
SWA_Q_HEADS = 8
SWA_KV_HEADS = 2
FOX_HEADS = 8
LRU_WIDTH = D_MODEL // 2
LRU_BLOCKS = 8
CONV_WIDTH = 4
LRU_C = 8.0
REL_BUCKETS = 32
REL_MAX_DIST = 128
D_FF = 2816
N_BRANCH = 3
EPS = 1e-6

SWA_QCOLS = SWA_Q_HEADS * HEAD_DIM
A_COLS = SWA_QCOLS + SWA_KV_HEADS * HEAD_DIM
FOX_WIDTH = FOX_HEADS * HEAD_DIM
F_COLS = 2 * FOX_WIDTH
KT_ROWS = FOX_WIDTH + SWA_KV_HEADS * HEAD_DIM
C_COLS = 2 * LRU_WIDTH
G_COLS = N_BRANCH * D_MODEL
OFF_A = 0
OFF_FL = OFF_A + A_COLS
OFF_F = OFF_FL + LANES
OFF_C = OFF_F + F_COLS
OFF_G = OFF_C + C_COLS
IN_COLS_PACKED = OFF_G + G_COLS

VMEM_LIMIT = 56 * 1024 * 1024


def _sigmoid(x):
    return 0.5 * jnp.tanh(0.5 * x) + 0.5


def _log_sigmoid(x):
    return jnp.minimum(x, 0.0) - jnp.log1p(jnp.exp(-jnp.abs(x)))


def _gelu_tanh(x):
    c = math.sqrt(2.0 / math.pi)
    return 0.5 * x * (1.0 + jnp.tanh(c * (x + 0.044715 * (x * x * x))))


def _rms_norm(x, g):
    ms = jnp.mean(x * x, axis=-1, keepdims=True)
    return x * lax.rsqrt(ms + EPS) * g


def _dot(a, b):
    return jnp.dot(a, b, preferred_element_type=F32)


def _dot_nt(a, b):
    return lax.dot_general(a, b, (((1,), (1,)), ((), ())), preferred_element_type=F32)


def _resident(shape):
    nd = len(shape)
    return pl.BlockSpec(shape, lambda *_: (0,) * nd, pipeline_mode=pl.Buffered(1))


def _row_tile(tp):
    return 384 if tp % 384 == 0 else BLOCK


def _pv_operand(v, lo_k):
    zero = jnp.zeros_like(v)
    ind_lo = jnp.where(lo_k, 1.0, 0.0).astype(BF16)
    ind_hi = jnp.where(lo_k, 0.0, 1.0).astype(BF16)
    top = jnp.concatenate([jnp.where(lo_k, v, zero), ind_lo], axis=1)
    bot = jnp.concatenate([jnp.where(lo_k, zero, v), ind_hi], axis=1)
    return jnp.concatenate([top, bot], axis=0)


def _in_proj_kernel(x_ref, g_ref, w_ref, wkt_ref, qa_ref, qf_ref, kt_ref, fl_ref, xy_ref, gt_ref):
    u = _rms_norm(x_ref[...], g_ref[...]).astype(BF16)

    def proj(c0, width):
        return _dot(u, w_ref[:, c0:c0 + width])

    qa_ref[:, 0:SWA_QCOLS] = proj(OFF_A, SWA_QCOLS).astype(BF16)
    v_fl = proj(OFF_A + SWA_QCOLS, 2 * LANES)
    qa_ref[:, SWA_QCOLS:] = v_fl[:, 0:LANES].astype(BF16)
    fl_ref[...] = v_fl[:, LANES:]
    for c in range(0, F_COLS, 512):
        qf_ref[:, c:c + 512] = proj(OFF_F + c, 512).astype(BF16)
    kt_ref[...] = _dot_nt(wkt_ref[...], u).astype(BF16)
    xy_ref[:, 0:LRU_WIDTH] = proj(OFF_C, LRU_WIDTH).astype(BF16)
    xy_ref[:, LRU_WIDTH:] = _gelu_tanh(proj(OFF_C + LRU_WIDTH, LRU_WIDTH)).astype(BF16)
    for c in range(0, G_COLS, 512):
        gt_ref[:, c:c + 512] = _sigmoid(proj(OFF_G + c, 512)).astype(BF16)


def _in_proj(h2, g, w, wkt):
    rows = h2.shape[0]
    tm = next(t for t in (768, 384, BLOCK) if rows % t == 0)
    row_spec = lambda cols: pl.BlockSpec((tm, cols), lambda i: (i, 0))
    return pl.pallas_call(
        _in_proj_kernel,
        grid=(rows // tm,),
        in_specs=[row_spec(D_MODEL), _resident((1, D_MODEL)), _resident((D_MODEL, IN_COLS_PACKED)),
                  _resident((KT_ROWS, D_MODEL))],
        out_specs=[row_spec(A_COLS), row_spec(F_COLS), pl.BlockSpec((KT_ROWS, tm), lambda i: (0, i)),
                   row_spec(LANES), row_spec(C_COLS), row_spec(G_COLS)],
        out_shape=[
            jax.ShapeDtypeStruct((rows, A_COLS), BF16),
            jax.ShapeDtypeStruct((rows, F_COLS), BF16),
            jax.ShapeDtypeStruct((KT_ROWS, rows), BF16),
            jax.ShapeDtypeStruct((rows, LANES), F32),
            jax.ShapeDtypeStruct((rows, C_COLS), BF16),
            jax.ShapeDtypeStruct((rows, G_COLS), BF16),
        ],
        compiler_params=pltpu.CompilerParams(dimension_semantics=("parallel",), vmem_limit_bytes=VMEM_LIMIT),
        name="in_proj",
    )(h2, g, w, wkt)


def _fox_prefix_kernel(fl_ref, fb_ref, ck_ref, cum_ref, *, tp, n_pad):
    lf = _log_sigmoid(fl_ref[0] + fb_ref[...])
    row = lax.broadcasted_iota(jnp.int32, (tp, 1), 0)
    lf = jnp.where(row >= n_pad, lf, 0.0)
    r = lax.broadcasted_iota(jnp.int32, (BLOCK, BLOCK), 0)
    c = lax.broadcasted_iota(jnp.int32, (BLOCK, BLOCK), 1)
    tri = jnp.where(r >= c, 1.0, 0.0).astype(F32)
    carry = jnp.zeros((1, LANES), F32)
    for blk in range(tp // BLOCK):
        sl = slice(blk * BLOCK, (blk + 1) * BLOCK)
        cum = jnp.dot(tri, lf[sl], preferred_element_type=F32, precision=lax.Precision.HIGHEST) + carry
        cum_ref[sl, :] = cum
        carry = cum[BLOCK - 1:BLOCK, :]
    ck_ref[0] = cum_ref[...].T[0:FOX_HEADS, :]


def _fox_prefix(fl3, fb, n_pad):
    b, tp, _ = fl3.shape
    return pl.pallas_call(
        functools.partial(_fox_prefix_kernel, tp=tp, n_pad=n_pad),
        grid=(b,),
        in_specs=[pl.BlockSpec((1, tp, LANES), lambda i: (i, 0, 0)), _resident((1, LANES))],
        out_specs=pl.BlockSpec((1, FOX_HEADS, tp), lambda i: (i, 0, 0)),
        out_shape=jax.ShapeDtypeStruct((b, FOX_HEADS, tp), F32),
        scratch_shapes=[pltpu.VMEM((tp, LANES), F32)],
        compiler_params=pltpu.CompilerParams(dimension_semantics=("parallel",), vmem_limit_bytes=VMEM_LIMIT),
        name="fox_prefix",
    )(fl3, fb)


def _swa_kernel(q_ref, v_ref, vp_ref, kt_ref, ktp_ref, bias_ref, sink_ref, o_ref, *, blocks, n_pad):
    ib = pl.program_id(1)
    half = SWA_Q_HEADS // 2
    rows = half * BLOCK
    lo = lax.broadcasted_iota(jnp.int32, (BLOCK, LANES), 1) < HEAD_DIM
    lo_k = lax.broadcasted_iota(jnp.int32, (2 * BLOCK, LANES), 1) < HEAD_DIM
    lo_o = lax.broadcasted_iota(jnp.int32, (rows, LANES), 1) < HEAD_DIM
    kidx = lax.broadcasted_iota(jnp.int32, (1, BLOCK), 1)
    sink = sink_ref[...]
    for g in range(blocks):
        cur = slice(g * BLOCK, (g + 1) * BLOCK)
        if g == 0:
            kt_prev, v_prev = ktp_ref[...], vp_ref[0]
        else:
            prev = slice((g - 1) * BLOCK, g * BLOCK)
            kt_prev, v_prev = kt_ref[:, prev], v_ref[0, prev, :]
        kt_band = jnp.concatenate([kt_prev, kt_ref[:, cur]], axis=1)
        v_band = jnp.concatenate([v_prev, v_ref[0, cur, :]], axis=0)
        slabs = [q_ref[0, cur, j * LANES:(j + 1) * LANES] for j in range(half)]
        zero = jnp.zeros_like(slabs[0])
        q_all = jnp.concatenate([jnp.where(lo, s, zero) for s in slabs] + [jnp.where(lo, zero, s) for s in slabs],
                                axis=0)
        s = _dot(q_all, kt_band) + bias_ref[...]
        key0 = (ib * blocks + g - 1) * BLOCK + kidx
        s_prev = jnp.where(key0 >= n_pad, s[:, 0:BLOCK], NEG_INF)
        s_cur = jnp.where(key0 + BLOCK >= n_pad, s[:, BLOCK:], NEG_INF)
        m = jnp.maximum(jnp.max(jnp.maximum(s_prev, s_cur), axis=-1, keepdims=True), sink)
        p = jnp.concatenate([jnp.exp2(s_prev - m), jnp.exp2(s_cur - m)], axis=1).astype(BF16)
        pv = _dot(jnp.concatenate([p[0:rows], p[rows:]], axis=1), _pv_operand(v_band, lo_k))
        esink = jnp.exp2(sink - m)
        den = pv[:, LANES:] + jnp.where(lo_o, esink[0:rows], esink[rows:])
        o = (pv[:, 0:LANES] / den).astype(BF16)
        for j in range(half):
            o_ref[0, cur, j * LANES:(j + 1) * LANES] = o[j * BLOCK:(j + 1) * BLOCK]


def _swa(qv3, kt, bias, sinks, n_pad):
    b, tp, _ = qv3.shape
    nb = tp // BLOCK
    blocks = next(n for n in (11, 3, 1) if nb % n == 0)
    rows = blocks * BLOCK
    steps = nb // blocks
    vblk = SWA_QCOLS // LANES
    krow = FOX_WIDTH // LANES
    return pl.pallas_call(
        functools.partial(_swa_kernel, blocks=blocks, n_pad=n_pad),
        grid=(b, steps),
        in_specs=[
            pl.BlockSpec((1, rows, SWA_QCOLS), lambda i, j: (i, j, 0)),
            pl.BlockSpec((1, rows, LANES), lambda i, j: (i, j, vblk)),
            pl.BlockSpec((1, BLOCK, LANES), lambda i, j: (i, jnp.maximum(j * blocks - 1, 0), vblk)),
            pl.BlockSpec((LANES, rows), lambda i, j: (krow, i * steps + j)),
            pl.BlockSpec((LANES, BLOCK), lambda i, j: (krow, jnp.maximum((i * steps + j) * blocks - 1, 0))),
            _resident((SWA_Q_HEADS * BLOCK, 2 * BLOCK)),
            _resident((SWA_Q_HEADS * BLOCK, LANES)),
        ],
        out_specs=pl.BlockSpec((1, rows, SWA_QCOLS), lambda i, j: (i, j, 0)),
        out_shape=jax.ShapeDtypeStruct((b, tp, SWA_QCOLS), BF16),
        compiler_params=pltpu.CompilerParams(dimension_semantics=("parallel", "parallel"), vmem_limit_bytes=VMEM_LIMIT),
        name="swa",
    )(qv3, qv3, qv3, kt, kt, bias, sinks)


FOX_TQ = 384
FOX_TK = 512
AUG_ROWS = 16

def _split3(c):
    hi = c.astype(BF16).astype(F32)
    mid = (c - hi).astype(BF16).astype(F32)
    return hi, mid, (c - hi) - mid


def _fox_kernel(q_ref, v_ref, kt_ref, ck_ref, o_ref, qa_s, m_s, acc_s, *, tq, tk, n_tiles, tail, n_pad):
    i = pl.program_id(1)
    pairs = FOX_HEADS // 2
    lane = lax.broadcasted_iota(jnp.int32, (tq, LANES), 1)
    lo = lane < HEAD_DIM
    pats = [jnp.where((lane >= 3 * e) & (lane < 3 * e + 3), 1.0, 0.0).astype(BF16) for e in range(2)]
    for hp in range(pairs):
        q = q_ref[0, :, hp * LANES:(hp + 1) * LANES]
        zero = jnp.zeros_like(q)
        qa_s[hp, 0:tq, 0:LANES] = jnp.where(lo, q, zero)
        qa_s[hp, tq:, 0:LANES] = jnp.where(lo, zero, q)
        qa_s[hp, 0:tq, LANES:] = pats[0]
        qa_s[hp, tq:, LANES:] = pats[1]
    m_s[...] = jnp.full(m_s.shape, NEG_INF, F32)
    acc_s[...] = jnp.zeros(acc_s.shape, F32)

    def tile(start, size, masked):
        sub = lax.broadcasted_iota(jnp.int32, (AUG_ROWS, size), 0)
        key_ok = start + lax.broadcasted_iota(jnp.int32, (1, size), 1) >= n_pad
        lo_k = lax.broadcasted_iota(jnp.int32, (size, LANES), 1) < HEAD_DIM
        if masked:
            rel = lax.broadcasted_iota(jnp.int32, (tq, size), 1) - lax.broadcasted_iota(jnp.int32, (tq, size), 0)
            ok = rel <= i * tq - start
        for hp in range(pairs):
            kt = kt_ref[hp * LANES:(hp + 1) * LANES, pl.ds(start, size)]
            aug = jnp.zeros((AUG_ROWS, size), F32)
            for e in range(2):
                c = jnp.where(key_ok, ck_ref[0, 2 * hp + e:2 * hp + e + 1, pl.ds(start, size)] * (-LOG2E), NEG_INF)
                for r, piece in enumerate(_split3(c)):
                    aug = jnp.where(sub == 3 * e + r, piece, aug)
            kaug = jnp.concatenate([kt, aug.astype(BF16), jnp.zeros((LANES - AUG_ROWS, size), BF16)], axis=0)
            s2 = _dot(qa_s[hp], kaug)
            ps, alphas = [], []
            for e in range(2):
                s = s2[e * tq:(e + 1) * tq]
                if masked:
                    s = jnp.where(ok, s, NEG_INF)
                chunks = [s[:, c0:c0 + LANES] for c0 in range(0, size, LANES)]
                m_prev = m_s[2 * hp + e]
                m_new = jnp.maximum(m_prev, jnp.max(functools.reduce(jnp.maximum, chunks), axis=-1, keepdims=True))
                alphas.append(jnp.exp2(m_prev - m_new))
                ps.append(jnp.concatenate([jnp.exp2(ch - m_new) for ch in chunks], axis=1).astype(BF16))
                m_s[2 * hp + e] = m_new
            alpha = jnp.where(lo, alphas[0], alphas[1])
            alpha = jnp.concatenate([alpha, alpha], axis=1)
            v = v_ref[0, pl.ds(start, size), hp * LANES:(hp + 1) * LANES]
            acc_s[hp] = alpha * acc_s[hp] + _dot(jnp.concatenate(ps, axis=1), _pv_operand(v, lo_k))

    n_last = jnp.minimum((tq * (i + 1) - 1) // tk, n_tiles - 1)
    n_free = (tq * i + 1) // tk

    def full_tile(j, masked):
        tile(pl.multiple_of(j * tk, tk), tk, masked)

    def two_free(j2, carry):
        full_tile(2 * j2, False)
        full_tile(2 * j2 + 1, False)
        return carry

    def one_masked(j, carry):
        full_tile(j, True)
        return carry

    lax.fori_loop(0, n_free // 2, two_free, 0)
    pl.when(n_free % 2 == 1)(lambda: full_tile(n_free - 1, False))
    lax.fori_loop(n_free, n_last, one_masked, 0)
    last_start = pl.multiple_of(n_last * tk, tk)
    reach = jnp.minimum(tq * (i + 1) - n_last * tk, tk)
    for size in range(LANES, tk + 1, LANES):
        pl.when(reach == size)(functools.partial(tile, last_start, size, True))
    if tail:
        pl.when(tq * (i + 1) > n_tiles * tk)(lambda: tile(n_tiles * tk, tail, True))

    for hp in range(pairs):
        acc = acc_s[hp]
        o_ref[0, :, hp * LANES:(hp + 1) * LANES] = (acc[:, 0:LANES] / acc[:, LANES:]).astype(BF16)


def _fox(qv3, kt, ck, n_pad):
    b, tp, _ = qv3.shape
    tq, tk = FOX_TQ, FOX_TK
    n_tiles, tail = tp // tk, tp % tk
    assert tp % tq == 0 and tail % LANES == 0 and tail <= tq and n_tiles >= 1
    pairs = FOX_HEADS // 2
    return pl.pallas_call(
        functools.partial(_fox_kernel, tq=tq, tk=tk, n_tiles=n_tiles, tail=tail, n_pad=n_pad),
        grid=(b, tp // tq),
        in_specs=[
            pl.BlockSpec((1, tq, FOX_WIDTH), lambda i, j: (i, j, 0)),
            pl.BlockSpec((1, tp, FOX_WIDTH), lambda i, j: (i, 0, 1)),
            pl.BlockSpec((FOX_WIDTH, tp), lambda i, j: (0, i)),
            pl.BlockSpec((1, FOX_HEADS, tp), lambda i, j: (i, 0, 0)),
        ],
        out_specs=pl.BlockSpec((1, tq, FOX_WIDTH), lambda i, j: (i, j, 0)),
        out_shape=jax.ShapeDtypeStruct((b, tp, FOX_WIDTH), BF16),
        scratch_shapes=[
            pltpu.VMEM((pairs, 2 * tq, 2 * LANES), BF16),
            pltpu.VMEM((FOX_HEADS, tq, LANES), F32),
            pltpu.VMEM((pairs, tq, 2 * LANES), F32),
        ],
        compiler_params=pltpu.CompilerParams(
            dimension_semantics=("parallel", "arbitrary"), vmem_limit_bytes=VMEM_LIMIT),
        name="fox",
    )(qv3, qv3, kt, ck)


SUBLANES = 8


LRU_SEQS = 2


def _lru_kernel(xy_ref, perm_ref, unperm_ref, cw_ref, cb_ref, wg_ref, bg_ref, lam_ref, o_ref,
                xe_s, a_s, h_s, hist_s, carry, *, seqs, tt, n_pad):
    @pl.when(pl.program_id(1) == 0)
    def _():
        hist_s[...] = jnp.zeros(hist_s.shape, F32)
        carry[...] = jnp.zeros(carry.shape, F32)

    for q in range(seqs):
        _lru_one(xy_ref.at[q], perm_ref, unperm_ref, cw_ref, cb_ref, wg_ref, bg_ref, lam_ref, o_ref.at[q],
                 xe_s.at[q], a_s.at[q], h_s.at[q], hist_s.at[q], carry.at[q], tt=tt, n_pad=n_pad)


def _lru_one(xy_ref, perm_ref, unperm_ref, cw_ref, cb_ref, wg_ref, bg_ref, lam_ref, o_ref,
             xe_s, a_s, h_s, hist_s, carry, *, tt, n_pad):
    it = pl.program_id(1)
    seg = tt // SUBLANES
    taps = CONV_WIDTH - 1
    w = LRU_WIDTH
    sub = lax.broadcasted_iota(jnp.int32, (SUBLANES, w), 0)
    xyp = _dot(perm_ref[...], xy_ref[...])
    xp, gate = xyp[:, 0:w], xyp[:, w:]

    for v in range(taps):
        src = slice((seg - taps + v) * SUBLANES, (seg - taps + v + 1) * SUBLANES)
        xe_s[v * SUBLANES:(v + 1) * SUBLANES, :] = jnp.where(
            sub == 0, pltpu.roll(hist_s[v], 1, 0), pltpu.roll(xp[src], 1, 0))
        hist_s[v] = xp[src]
    xe_s[taps * SUBLANES:, :] = xp
    conv = xe_s[0:tt, :] * cw_ref[0:1, :]
    for i in range(1, CONV_WIDTH):
        conv = conv + xe_s[i * SUBLANES:i * SUBLANES + tt, :] * cw_ref[i:i + 1, :]
    conv = conv + cb_ref[...]

    z = _dot(conv.astype(BF16), wg_ref[...]) + bg_ref[...]
    r = _sigmoid(z[:, 0:w])
    gi = _sigmoid(z[:, w:])
    log_a = LRU_C * r * _log_sigmoid(lam_ref[...])
    p_idx = lax.broadcasted_iota(jnp.int32, (tt, 1), 0)
    time = it * tt + (p_idx & (SUBLANES - 1)) * seg + (p_idx >> 3)
    a = jnp.exp(log_a)
    one_minus_a2 = -jnp.tanh(log_a) * (a * a + 1.0)
    b = jnp.where(time >= n_pad, jnp.sqrt(one_minus_a2) * (gi * conv), 0.0)

    h_loc = jnp.zeros((SUBLANES, w), F32)
    prod = jnp.ones((SUBLANES, w), F32)
    for v in range(seg):
        rows = slice(v * SUBLANES, (v + 1) * SUBLANES)
        h_loc = a[rows] * h_loc + b[rows]
        prod = a[rows] * prod
        h_s[rows, :] = h_loc
        a_s[rows, :] = prod
    aa, bb = prod, h_loc
    for s in (1, 2, 4):
        keep = sub >= s
        bb = jnp.where(keep, aa * pltpu.roll(bb, s, 0) + bb, bb)
        aa = jnp.where(keep, aa * pltpu.roll(aa, s, 0), aa)
    h_end = aa * carry[...] + bb
    h_in = jnp.where(sub == 0, carry[...], pltpu.roll(h_end, 1, 0))
    carry[...] = h_end[SUBLANES - 1:SUBLANES, :]

    h_in_all = jnp.concatenate([h_in] * seg, axis=0)
    out = ((h_s[...] + a_s[...] * h_in_all) * gate).astype(BF16)
    o_ref[...] = _dot(unperm_ref[...], out).astype(BF16)


def _lru(xy3, cw, cb, wg, bg, lam, n_pad):
    b, tp, _ = xy3.shape
    tt = _row_tile(tp)
    w = LRU_WIDTH
    seg = tt // SUBLANES
    p_idx = jnp.arange(tt)
    perm = (jnp.arange(tt)[None, :] == ((p_idx % SUBLANES) * seg + p_idx // SUBLANES)[:, None]).astype(BF16)
    seqs = LRU_SEQS if b % LRU_SEQS == 0 else 1
    return pl.pallas_call(
        functools.partial(_lru_kernel, seqs=seqs, tt=tt, n_pad=n_pad),
        grid=(b // seqs, tp // tt),
        in_specs=[
            pl.BlockSpec((seqs, tt, C_COLS), lambda i, j: (i, j, 0)),
            _resident((tt, tt)), _resident((tt, tt)),
            _resident((CONV_WIDTH, w)), _resident((1, w)), _resident((w, 2 * w)), _resident((1, 2 * w)),
            _resident((1, w)),
        ],
        out_specs=pl.BlockSpec((seqs, tt, w), lambda i, j: (i, j, 0)),
        out_shape=jax.ShapeDtypeStruct((b, tp, w), BF16),
        scratch_shapes=[
            pltpu.VMEM((seqs, tt + (CONV_WIDTH - 1) * SUBLANES, w), F32), pltpu.VMEM((seqs, tt, w), F32),
            pltpu.VMEM((seqs, tt, w), F32), pltpu.VMEM((seqs, CONV_WIDTH - 1, SUBLANES, w), F32),
            pltpu.VMEM((seqs, 1, w), F32),
        ],
        compiler_params=pltpu.CompilerParams(dimension_semantics=("parallel", "arbitrary"), vmem_limit_bytes=VMEM_LIMIT),
        name="lru",
    )(xy3, perm, perm.T, cw, cb, wg, bg, lam)


FF_CHUNK = 256


def _mix_ffn_rows(h, oa, of, oc, gt_ref, wb_ref, wo_ref, gn_ref, w1_ref, w2_ref, act_s):
    d = D_MODEL
    merged = gt_ref[:, 0:d].astype(F32) * _dot(oa, wb_ref[0])
    merged = merged + gt_ref[:, d:2 * d].astype(F32) * _dot(of, wb_ref[1])
    merged = merged + gt_ref[:, 2 * d:3 * d].astype(F32) * _dot(oc, wb_ref[2])
    hm = h + _dot(merged.astype(BF16), wo_ref[...])
    u = _rms_norm(hm, gn_ref[...]).astype(BF16)
    for c in range(0, D_FF, FF_CHUNK):
        gate = _dot(u, w1_ref[:, c:c + FF_CHUNK])
        up = _dot(u, w1_ref[:, D_FF + c:D_FF + c + FF_CHUNK])
        act_s[:, c:c + FF_CHUNK] = (gate * _sigmoid(gate) * up).astype(BF16)
    return hm + _dot(act_s[...], w2_ref[...])


def _mix_ffn_kernel(h_ref, oa_ref, of_ref, oc_ref, gt_ref, wb_ref, wo_ref, gn_ref, w1_ref, w2_ref, out_ref, act_s,
                    *, tm, n_pad):
    j = pl.program_id(1)
    y = _mix_ffn_rows(h_ref[0], oa_ref[0], of_ref[0], oc_ref[0], gt_ref.at[0], wb_ref, wo_ref, gn_ref, w1_ref, w2_ref,
                      act_s)
    row = j * tm + lax.broadcasted_iota(jnp.int32, (tm, 1), 0)
    out_ref[0] = jnp.where(row >= n_pad, y, 0.0)


def _mix_ffn_last_kernel(h_ref, oa_ref, of_ref, oc_ref, gt_ref, wb_ref, wo_ref, gn_ref, w1_ref, w2_ref, gf_ref,
                         out_ref, act_s):
    y = _mix_ffn_rows(h_ref[...], oa_ref[...], of_ref[...], oc_ref[...], gt_ref, wb_ref, wo_ref, gn_ref, w1_ref,
                      w2_ref, act_s)
    out_ref[0] = _rms_norm(y, gf_ref[...])


def _mix_ffn_last(h3, oa, of, oc, gt3, wb, wo, gn, w1, w2, gf, skip_rows, seq):
    b, tp, d = h3.shape
    tm = 512 if seq % 512 == 0 else BLOCK
    w = LRU_WIDTH

    def rows(cols):
        return pl.BlockSpec((pl.Element(tm), pl.Element(cols)),
                            lambda i, j: (pl.multiple_of(i * tp + skip_rows + j * tm, BLOCK), 0))

    flat = lambda a: a.reshape(b * tp, a.shape[-1])
    return pl.pallas_call(
        _mix_ffn_last_kernel,
        grid=(b, seq // tm),
        in_specs=[
            rows(d), rows(w), rows(w), rows(w), rows(G_COLS),
            _resident((N_BRANCH, w, d)), _resident((d, d)), _resident((1, d)),
            _resident((d, 2 * D_FF)), _resident((D_FF, d)), _resident((1, d)),
        ],
        out_specs=pl.BlockSpec((1, tm, d), lambda i, j: (i, j, 0)),
        out_shape=jax.ShapeDtypeStruct((b, seq, d), F32),
        scratch_shapes=[pltpu.VMEM((tm, D_FF), BF16)],
        compiler_params=pltpu.CompilerParams(dimension_semantics=("parallel", "parallel"), vmem_limit_bytes=VMEM_LIMIT),
        name="mix_ffn_last",
    )(flat(h3), flat(oa), flat(of), flat(oc), flat(gt3), wb, wo, gn, w1, w2, gf)


def _mix_ffn(h3, oa, of, oc, gt3, wb, wo, gn, w1, w2, n_pad):
    b, tp, d = h3.shape
    tm = _row_tile(tp)
    w = LRU_WIDTH
    tile = lambda cols: pl.BlockSpec((1, tm, cols), lambda i, j: (i, j, 0))
    return pl.pallas_call(
        functools.partial(_mix_ffn_kernel, tm=tm, n_pad=n_pad),
        grid=(b, tp // tm),
        in_specs=[
            tile(d), tile(w), tile(w), tile(w), tile(G_COLS),
            _resident((N_BRANCH, w, d)), _resident((d, d)), _resident((1, d)),
            _resident((d, 2 * D_FF)), _resident((D_FF, d)),
        ],
        out_specs=tile(d),
        out_shape=jax.ShapeDtypeStruct((b, tp, d), F32),
        scratch_shapes=[pltpu.VMEM((tm, D_FF), BF16)],
        compiler_params=pltpu.CompilerParams(dimension_semantics=("parallel", "parallel"), vmem_limit_bytes=VMEM_LIMIT),
        name="mix_ffn",
    )(h3, oa, of, oc, gt3, wb, wo, gn, w1, w2)


def _t5_bucket(dist):
    max_exact = REL_BUCKETS // 2
    d = jnp.maximum(dist, 0)
    scaled = jnp.log(jnp.maximum(d, 1).astype(F32) / max_exact) / math.log(REL_MAX_DIST / max_exact)
    large = jnp.minimum(max_exact + (scaled * (REL_BUCKETS - max_exact)).astype(jnp.int32), REL_BUCKETS - 1)
    return jnp.where(d < max_exact, d, large)


def _swa_bias_kernel(bucket_ref, table_ref, o_ref):
    bucket = bucket_ref[...]
    for h in range(SWA_Q_HEADS):
        acc = jnp.full(bucket.shape, NEG_INF, F32)
        for c in range(REL_BUCKETS):
            acc = jnp.where(bucket == c, table_ref[c, h] * LOG2E, acc)
        o_ref[h * BLOCK:(h + 1) * BLOCK, :] = acc


def _swa_bias(rel_table):
    q_idx = jnp.arange(BLOCK)[:, None]
    k_idx = jnp.arange(2 * BLOCK)[None, :]
    dist = q_idx + BLOCK - k_idx
    bucket = jnp.where((dist >= 0) & (dist < SWA_WINDOW), _t5_bucket(dist), -1).astype(jnp.int32)
    return pl.pallas_call(
        _swa_bias_kernel,
        in_specs=[pl.BlockSpec(memory_space=pltpu.VMEM), pl.BlockSpec(memory_space=pltpu.SMEM)],
        out_specs=pl.BlockSpec(memory_space=pltpu.VMEM),
        out_shape=jax.ShapeDtypeStruct((SWA_Q_HEADS * BLOCK, 2 * BLOCK), F32),
        name="swa_bias",
    )(bucket, rel_table.astype(F32))


def _slab_head_order():
    half = SWA_Q_HEADS // 2
    return [h for j in range(half) for h in (j, j + half)]


def _pack_w_in(w_in):
    depth = w_in.shape[0]
    sizes = (512, 128, 128, 512, 512, 512, 8, 512, 512, 3072)
    offs = [0]
    for s in sizes:
        offs.append(offs[-1] + s)
    qa, ka, va, qf, kf, vf, fl, xc, yc, gates = [w_in[:, :, offs[i]:offs[i + 1]] for i in range(len(sizes))]
    scale = HEAD_DIM ** -0.5 * LOG2E
    qa = jnp.concatenate([qa[:, :, h * HEAD_DIM:(h + 1) * HEAD_DIM] for h in _slab_head_order()], axis=-1) * scale
    fl = jnp.pad(fl, ((0, 0), (0, 0), (0, LANES - FOX_HEADS)))
    packed = jnp.concatenate([qa, va, fl, qf * scale, vf, xc, yc, gates], axis=-1)
    keys_t = jnp.swapaxes(jnp.concatenate([kf, ka], axis=-1), 1, 2)
    return packed.astype(BF16), keys_t.astype(BF16)


def _block_diag(w):
    depth, nb, n, _ = w.shape
    eye = jnp.eye(nb, dtype=w.dtype)
    return jnp.einsum('lhij,hk->lhikj', w, eye).reshape(depth, nb * n, nb * n)


def kernel(x, meta_tokens, rel_bias_table, norm_mix, w_in, swa_sinks, fox_forget_bias, conv_w, conv_b,
           lru_w_r, lru_b_r, lru_w_i, lru_b_i, lru_lambda, w_branch, w_out, norm_ffn, w_ffn_in, w_ffn_out,
           norm_final):
    b, seq, d = x.shape
    depth = w_in.shape[0]
    t = N_META + seq
    n_pad = (-t) % BLOCK
    tp = t + n_pad
    assert d == D_MODEL and (n_pad + N_META) % BLOCK == 0

    meta = jnp.broadcast_to(meta_tokens.astype(x.dtype)[None], (b, N_META, d))
    h = jnp.concatenate([jnp.zeros((b, n_pad, d), x.dtype), meta, x], axis=1)

    w_in_p, wkt = _pack_w_in(w_in)
    bias_a = _swa_bias(rel_bias_table)
    sinks =jnp.broadcast_to((swa_sinks.astype(F32) * LOG2E)[:, :, None, None], (depth, SWA_Q_HEADS, BLOCK, LANES))
    sinks = sinks.reshape(depth, SWA_Q_HEADS * BLOCK, LANES)
    fb = jnp.pad(fox_forget_bias.astype(F32), ((0, 0), (0, LANES - FOX_HEADS)))[:, None, :]
    wg = jnp.concatenate([_block_diag(lru_w_r), _block_diag(lru_w_i)], axis=-1).astype(BF16)
    bg = jnp.concatenate([lru_b_r, lru_b_i], axis=-1).astype(F32)[:, None, :]
    wb_a = jnp.concatenate([w_branch[:, 0, h * HEAD_DIM:(h + 1) * HEAD_DIM] for h in _slab_head_order()], axis=1)
    wb = jnp.stack([wb_a, w_branch[:, 1], w_branch[:, 2]], axis=1).astype(BF16)
    wo = w_out.astype(BF16)
    w1 = w_ffn_in.astype(BF16)
    w2 = w_ffn_out.astype(BF16)

    for l in range(depth):
        qva, qvf, kt, fl, xy, gt = _in_proj(h.reshape(b * tp, d), norm_mix[l][None, :], w_in_p[l], wkt[l])
        ck = _fox_prefix(fl.reshape(b, tp, LANES), fb[l], n_pad)
        o_a = _swa(qva.reshape(b, tp, A_COLS), kt, bias_a, sinks[l], n_pad)
        o_f = _fox(qvf.reshape(b, tp, F_COLS), kt, ck, n_pad)
        o_c = _lru(xy.reshape(b, tp, C_COLS), conv_w[l], conv_b[l][None, :], wg[l], bg[l], lru_lambda[l][None, :],
                   n_pad)
        mix_args = (h, o_a, o_f, o_c, gt.reshape(b, tp, G_COLS), wb[l], wo[l], norm_ffn[l][None, :], w1[l], w2[l])
        if l + 1 < depth:
            h = _mix_ffn(*mix_args, n_pad)
    return _mix_ffn_last(*mix_args, norm_final[None, :], n_pad + N_META, seq)
```

```python
import functools
import math

import jax
import jax.numpy as jnp
from jax import lax
from jax.experimental import pallas as pl
from jax.experimental.pallas import tpu as pltpu

F32 = jnp.float32
BF16 = jnp.bfloat16

D_MODEL = 1024
HEAD_DIM = 64
N_META = 16
BLOCK = 128
LANES = 128
NEG_INF = -1e30
LOG2E = math.log2(math.e)
SWA_WINDOW = 128
SWA_Q_HEADS = 8
SWA_KV_HEADS = 2
FOX_HEADS = 8
LRU_WIDTH = D_MODEL // 2
LRU_BLOCKS = 8
CONV_WIDTH = 4
LRU_C = 8.0
REL_BUCKETS = 32
REL_MAX_DIST = 128
D_FF = 2816
N_BRANCH = 3
EPS = 1e-6

SWA_QCOLS = SWA_Q_HEADS * HEAD_DIM
A_COLS = SWA_QCOLS + SWA_KV_HEADS * HEAD_DIM
FOX_WIDTH = FOX_HEADS * HEAD_DIM
F_COLS = 2 * FOX_WIDTH
KT_ROWS = FOX_WIDTH + SWA_KV_HEADS * HEAD_DIM
C_COLS = 2 * LRU_WIDTH
G_COLS = N_BRANCH * D_MODEL
OFF_A = 0
OFF_FL = OFF_A + A_COLS
OFF_F = OFF_FL + LANES
OFF_C = OFF_F + F_COLS
OFF_G = OFF_C + C_COLS
IN_COLS_PACKED = OFF_G + G_COLS

VMEM_LIMIT = 56 * 1024 * 1024


def _sigmoid(x):
    return 0.5 * jnp.tanh(0.5 * x) + 0.5


def _log_sigmoid(x):
    return jnp.minimum(x, 0.0) - jnp.log1p(jnp.exp(-jnp.abs(x)))


def _gelu_tanh(x):
    c = math.sqrt(2.0 / math.pi)
    return 0.5 * x * (1.0 + jnp.tanh(c * (x + 0.044715 * (x * x * x))))


def _rms_norm(x, g):
    ms = jnp.mean(x * x, axis=-1, keepdims=True)
    return x * lax.rsqrt(ms + EPS) * g


def _dot(a, b):
    return jnp.dot(a, b, preferred_element_type=F32)


def _dot_nt(a, b):
    return lax.dot_general(a, b, (((1,), (1,)), ((), ())), preferred_element_type=F32)


def _resident(shape):
    nd = len(shape)
    return pl.BlockSpec(shape, lambda *_: (0,) * nd, pipeline_mode=pl.Buffered(1))


def _row_tile(tp):
    return 384 if tp % 384 == 0 else BLOCK


def _pv_operand(v, lo_k):
    zero = jnp.zeros_like(v)
    ind_lo = jnp.where(lo_k, 1.0, 0.0).astype(BF16)
    ind_hi = jnp.where(lo_k, 0.0, 1.0).astype(BF16)
    top = jnp.concatenate([jnp.where(lo_k, v, zero), ind_lo], axis=1)
    bot = jnp.concatenate([jnp.where(lo_k, zero, v), ind_hi], axis=1)
    return jnp.concatenate([top, bot], axis=0)


def _in_proj_kernel(x_ref, g_ref, w_ref, wkt_ref, qa_ref, qf_ref, kt_ref, fl_ref, xy_ref, gt_ref):
    u = _rms_norm(x_ref[...], g_ref[...]).astype(BF16)

    def proj(c0, width):
        return _dot(u, w_ref[:, c0:c0 + width])

    qa_ref[:, 0:SWA_QCOLS] = proj(OFF_A, SWA_QCOLS).astype(BF16)
    v_fl = proj(OFF_A + SWA_QCOLS, 2 * LANES)
    qa_ref[:, SWA_QCOLS:] = v_fl[:, 0:LANES].astype(BF16)
    fl_ref[...] = v_fl[:, LANES:]
    for c in range(0, F_COLS, 512):
        qf_ref[:, c:c + 512] = proj(OFF_F + c, 512).astype(BF16)
    kt_ref[...] = _dot_nt(wkt_ref[...], u).astype(BF16)
    xy_ref[:, 0:LRU_WIDTH] = proj(OFF_C, LRU_WIDTH).astype(BF16)
    xy_ref[:, LRU_WIDTH:] = _gelu_tanh(proj(OFF_C + LRU_WIDTH, LRU_WIDTH)).astype(BF16)
    for c in range(0, G_COLS, 512):
        gt_ref[:, c:c + 512] = _sigmoid(proj(OFF_G + c, 512)).astype(BF16)


def _in_proj(h2, g, w, wkt):
    rows = h2.shape[0]
    tm = next(t for t in (768, 384, BLOCK) if rows % t == 0)
    row_spec = lambda cols: pl.BlockSpec((tm, cols), lambda i: (i, 0))
    return pl.pallas_call(
        _in_proj_kernel,
        grid=(rows // tm,),
        in_specs=[row_spec(D_MODEL), _resident((1, D_MODEL)), _resident((D_MODEL, IN_COLS_PACKED)),
                  _resident((KT_ROWS, D_MODEL))],
        out_specs=[row_spec(A_COLS), row_spec(F_COLS), pl.BlockSpec((KT_ROWS, tm), lambda i: (0, i)),
                   row_spec(LANES), row_spec(C_COLS), row_spec(G_COLS)],
        out_shape=[
            jax.ShapeDtypeStruct((rows, A_COLS), BF16),
            jax.ShapeDtypeStruct((rows, F_COLS), BF16),
            jax.ShapeDtypeStruct((KT_ROWS, rows), BF16),
            jax.ShapeDtypeStruct((rows, LANES), F32),
            jax.ShapeDtypeStruct((rows, C_COLS), BF16),
            jax.ShapeDtypeStruct((rows, G_COLS), BF16),
        ],
        compiler_params=pltpu.CompilerParams(dimension_semantics=("parallel",), vmem_limit_bytes=VMEM_LIMIT),
        name="in_proj",
    )(h2, g, w, wkt)


def _fox_prefix_kernel(fl_ref, fb_ref, ck_ref, cum_ref, *, tp, n_pad):
    lf = _log_sigmoid(fl_ref[0] + fb_ref[...])
    row = lax.broadcasted_iota(jnp.int32, (tp, 1), 0)
    lf = jnp.where(row >= n_pad, lf, 0.0)
    r = lax.broadcasted_iota(jnp.int32, (BLOCK, BLOCK), 0)
    c = lax.broadcasted_iota(jnp.int32, (BLOCK, BLOCK), 1)
    tri = jnp.where(r >= c, 1.0, 0.0).astype(F32)
    carry = jnp.zeros((1, LANES), F32)
    for blk in range(tp // BLOCK):
        sl = slice(blk * BLOCK, (blk + 1) * BLOCK)
        cum = jnp.dot(tri, lf[sl], preferred_element_type=F32, precision=lax.Precision.HIGHEST) + carry
        cum_ref[sl, :] = cum
        carry = cum[BLOCK - 1:BLOCK, :]
    ck_ref[0] = cum_ref[...].T[0:FOX_HEADS, :]


def _fox_prefix(fl3, fb, n_pad):
    b, tp, _ = fl3.shape
    return pl.pallas_call(
        functools.partial(_fox_prefix_kernel, tp=tp, n_pad=n_pad),
        grid=(b,),
        in_specs=[pl.BlockSpec((1, tp, LANES), lambda i: (i, 0, 0)), _resident((1, LANES))],
        out_specs=pl.BlockSpec((1, FOX_HEADS, tp), lambda i: (i, 0, 0)),
        out_shape=jax.ShapeDtypeStruct((b, FOX_HEADS, tp), F32),
        scratch_shapes=[pltpu.VMEM((tp, LANES), F32)],
        compiler_params=pltpu.CompilerParams(dimension_semantics=("parallel",), vmem_limit_bytes=VMEM_LIMIT),
        name="fox_prefix",
    )(fl3, fb)


def _swa_kernel(q_ref, v_ref, vp_ref, kt_ref, ktp_ref, bias_ref, sink_ref, o_ref, *, blocks, n_pad):
    ib = pl.program_id(1)
    half = SWA_Q_HEADS // 2
    rows = half * BLOCK
    lo = lax.broadcasted_iota(jnp.int32, (BLOCK, LANES), 1) < HEAD_DIM
    lo_k = lax.broadcasted_iota(jnp.int32, (2 * BLOCK, LANES), 1) < HEAD_DIM
    lo_o = lax.broadcasted_iota(jnp.int32, (rows, LANES), 1) < HEAD_DIM
    kidx = lax.broadcasted_iota(jnp.int32, (1, BLOCK), 1)
    sink = sink_ref[...]
    for g in range(blocks):
        cur = slice(g * BLOCK, (g + 1) * BLOCK)
        if g == 0:
            kt_prev, v_prev = ktp_ref[...], vp_ref[0]
        else:
            prev = slice((g - 1) * BLOCK, g * BLOCK)
            kt_prev, v_prev = kt_ref[:, prev], v_ref[0, prev, :]
        kt_band = jnp.concatenate([kt_prev, kt_ref[:, cur]], axis=1)
        v_band = jnp.concatenate([v_prev, v_ref[0, cur, :]], axis=0)
        slabs = [q_ref[0, cur, j * LANES:(j + 1) * LANES] for j in range(half)]
        zero = jnp.zeros_like(slabs[0])
        q_all = jnp.concatenate([jnp.where(lo, s, zero) for s in slabs] + [jnp.where(lo, zero, s) for s in slabs],
                                axis=0)
        s = _dot(q_all, kt_band) + bias_ref[...]
        key0 = (ib * blocks + g - 1) * BLOCK + kidx
        s_prev = jnp.where(key0 >= n_pad, s[:, 0:BLOCK], NEG_INF)
        s_cur = jnp.where(key0 + BLOCK >= n_pad, s[:, BLOCK:], NEG_INF)
        m = jnp.maximum(jnp.max(jnp.maximum(s_prev, s_cur), axis=-1, keepdims=True), sink)
        p = jnp.concatenate([jnp.exp2(s_prev - m), jnp.exp2(s_cur - m)], axis=1).astype(BF16)
        pv = _dot(jnp.concatenate([p[0:rows], p[rows:]], axis=1), _pv_operand(v_band, lo_k))
        esink = jnp.exp2(sink - m)
        den = pv[:, LANES:] + jnp.where(lo_o, esink[0:rows], esink[rows:])
        o = (pv[:, 0:LANES] / den).astype(BF16)
        for j in range(half):
            o_ref[0, cur, j * LANES:(j + 1) * LANES] = o[j * BLOCK:(j + 1) * BLOCK]


def _swa(qv3, kt, bias, sinks, n_pad):
    b, tp, _ = qv3.shape
    nb = tp // BLOCK
    blocks = next(n for n in (11, 3, 1) if nb % n == 0)
    rows = blocks * BLOCK
    steps = nb // blocks
    vblk = SWA_QCOLS // LANES
    krow = FOX_WIDTH // LANES
    return pl.pallas_call(
        functools.partial(_swa_kernel, blocks=blocks, n_pad=n_pad),
        grid=(b, steps),
        in_specs=[
            pl.BlockSpec((1, rows, SWA_QCOLS), lambda i, j: (i, j, 0)),
            pl.BlockSpec((1, rows, LANES), lambda i, j: (i, j, vblk)),
            pl.BlockSpec((1, BLOCK, LANES), lambda i, j: (i, jnp.maximum(j * blocks - 1, 0), vblk)),
            pl.BlockSpec((LANES, rows), lambda i, j: (krow, i * steps + j)),
            pl.BlockSpec((LANES, BLOCK), lambda i, j: (krow, jnp.maximum((i * steps + j) * blocks - 1, 0))),
            _resident((SWA_Q_HEADS * BLOCK, 2 * BLOCK)),
            _resident((SWA_Q_HEADS * BLOCK, LANES)),
        ],
        out_specs=pl.BlockSpec((1, rows, SWA_QCOLS), lambda i, j: (i, j, 0)),
        out_shape=jax.ShapeDtypeStruct((b, tp, SWA_QCOLS), BF16),
        compiler_params=pltpu.CompilerParams(dimension_semantics=("parallel", "parallel"), vmem_limit_bytes=VMEM_LIMIT),
        name="swa",
    )(qv3, qv3, qv3, kt, kt, bias, sinks)


FOX_TQ = 384
FOX_TK = 512


def _fox_kernel(q_ref, v_ref, kt_ref, ck_ref, o_ref, qa_s, m_s, acc_s, *, tq, tk, n_tiles, tail, n_pad):
    i = pl.program_id(1)
    pairs = FOX_HEADS // 2
    lane = lax.broadcasted_iota(jnp.int32, (tq, LANES), 1)
    lo = lane < HEAD_DIM
    for hp in range(pairs):
        q = q_ref[0, :, hp * LANES:(hp + 1) * LANES]
        zero = jnp.zeros_like(q)
        qa_s[hp, 0:tq, :] = jnp.where(lo, q, zero)
        qa_s[hp, tq:, :] = jnp.where(lo, zero, q)
    m_s[...] = jnp.full(m_s.shape, NEG_INF, F32)
    acc_s[...] = jnp.zeros(acc_s.shape, F32)

    def tile(start, size, masked):
        key_ok = start + lax.broadcasted_iota(jnp.int32, (1, size), 1) >= n_pad
        lo_k = lax.broadcasted_iota(jnp.int32, (size, LANES), 1) < HEAD_DIM
        if masked:
            rel = lax.broadcasted_iota(jnp.int32, (tq, size), 1) - lax.broadcasted_iota(jnp.int32, (tq, size), 0)
            ok = rel <= i * tq - start
        for hp in range(pairs):
            kt = kt_ref[hp * LANES:(hp + 1) * LANES, pl.ds(start, size)]
            s2 = _dot(qa_s[hp], kt)
            ps, alphas = [], []
            for e in range(2):
                c = jnp.where(key_ok, ck_ref[0, 2 * hp + e:2 * hp + e + 1, pl.ds(start, size)] * (-LOG2E), NEG_INF)
                s = s2[e * tq:(e + 1) * tq] + c
                if masked:
                    s = jnp.where(ok, s, NEG_INF)
                chunks = [s[:, c0:c0 + LANES] for c0 in range(0, size, LANES)]
                m_prev = m_s[2 * hp + e]
                m_new = jnp.maximum(m_prev, jnp.max(functools.reduce(jnp.maximum, chunks), axis=-1, keepdims=True))
                alphas.append(jnp.exp2(m_prev - m_new))
                ps.append(jnp.concatenate([jnp.exp2(ch - m_new) for ch in chunks], axis=1).astype(BF16))
                m_s[2 * hp + e] = m_new
            alpha = jnp.where(lo, alphas[0], alphas[1])
            alpha = jnp.concatenate([alpha, alpha], axis=1)
            v = v_ref[0, pl.ds(start, size), hp * LANES:(hp + 1) * LANES]
            acc_s[hp] = alpha * acc_s[hp] + _dot(jnp.concatenate(ps, axis=1), _pv_operand(v, lo_k))

    n_last = jnp.minimum((tq * (i + 1) - 1) // tk, n_tiles - 1)
    n_free = (tq * i + 1) // tk

    def full_tile(j, masked):
        tile(pl.multiple_of(j * tk, tk), tk, masked)

    def two_free(j2, carry):
        full_tile(2 * j2, False)
        full_tile(2 * j2 + 1, False)
        return carry

    def one_masked(j, carry):
        full_tile(j, True)
        return carry

    lax.fori_loop(0, n_free // 2, two_free, 0)
    pl.when(n_free % 2 == 1)(lambda: full_tile(n_free - 1, False))
    lax.fori_loop(n_free, n_last, one_masked, 0)
    last_start = pl.multiple_of(n_last * tk, tk)
    reach = jnp.minimum(tq * (i + 1) - n_last * tk, tk)
    for size in range(LANES, tk + 1, LANES):
        pl.when(reach == size)(functools.partial(tile, last_start, size, True))
    if tail:
        pl.when(tq * (i + 1) > n_tiles * tk)(lambda: tile(n_tiles * tk, tail, True))

    for hp in range(pairs):
        acc = acc_s[hp]
        o_ref[0, :, hp * LANES:(hp + 1) * LANES] = (acc[:, 0:LANES] / acc[:, LANES:]).astype(BF16)


def _fox(qv3, kt, ck, n_pad):
    b, tp, _ = qv3.shape
    tq, tk = FOX_TQ, FOX_TK
    n_tiles, tail = tp // tk, tp % tk
    assert tp % tq == 0 and tail % LANES == 0 and tail <= tq and n_tiles >= 1
    pairs = FOX_HEADS // 2
    return pl.pallas_call(
        functools.partial(_fox_kernel, tq=tq, tk=tk, n_tiles=n_tiles, tail=tail, n_pad=n_pad),
        grid=(b, tp // tq),
        in_specs=[
            pl.BlockSpec((1, tq, FOX_WIDTH), lambda i, j: (i, j, 0)),
            pl.BlockSpec((1, tp, FOX_WIDTH), lambda i, j: (i, 0, 1)),
            pl.BlockSpec((FOX_WIDTH, tp), lambda i, j: (0, i)),
            pl.BlockSpec((1, FOX_HEADS, tp), lambda i, j: (i, 0, 0)),
        ],
        out_specs=pl.BlockSpec((1, tq, FOX_WIDTH), lambda i, j: (i, j, 0)),
        out_shape=jax.ShapeDtypeStruct((b, tp, FOX_WIDTH), BF16),
        scratch_shapes=[
            pltpu.VMEM((pairs, 2 * tq, LANES), BF16),
            pltpu.VMEM((FOX_HEADS, tq, LANES), F32),
            pltpu.VMEM((pairs, tq, 2 * LANES), F32),
        ],
        compiler_params=pltpu.CompilerParams(
            dimension_semantics=("parallel", "arbitrary"), vmem_limit_bytes=VMEM_LIMIT),
        name="fox",
    )(qv3, qv3, kt, ck)


SUBLANES = 8


LRU_SEQS = 2


def _lru_kernel(xy_ref, perm_ref, unperm_ref, cw_ref, cb_ref, wg_ref, bg_ref, lam_ref, o_ref,
                xe_s, a_s, h_s, hist_s, carry, *, seqs, tt, n_pad):
    @pl.when(pl.program_id(1) == 0)
    def _():
        hist_s[...] = jnp.zeros(hist_s.shape, F32)
        carry[...] = jnp.zeros(carry.shape, F32)

    for q in range(seqs):
        _lru_one(xy_ref.at[q], perm_ref, unperm_ref, cw_ref, cb_ref, wg_ref, bg_ref, lam_ref, o_ref.at[q],
                 xe_s.at[q], a_s.at[q], h_s.at[q], hist_s.at[q], carry.at[q], tt=tt, n_pad=n_pad)


def _lru_one(xy_ref, perm_ref, unperm_ref, cw_ref, cb_ref, wg_ref, bg_ref, lam_ref, o_ref,
             xe_s, a_s, h_s, hist_s, carry, *, tt, n_pad):
    it = pl.program_id(1)
    seg = tt // SUBLANES
    taps = CONV_WIDTH - 1
    w = LRU_WIDTH
    sub = lax.broadcasted_iota(jnp.int32, (SUBLANES, w), 0)
    xyp = _dot(perm_ref[...], xy_ref[...])
    xp, gate = xyp[:, 0:w], xyp[:, w:]

    for v in range(taps):
        src = slice((seg - taps + v) * SUBLANES, (seg - taps + v + 1) * SUBLANES)
        xe_s[v * SUBLANES:(v + 1) * SUBLANES, :] = jnp.where(
            sub == 0, pltpu.roll(hist_s[v], 1, 0), pltpu.roll(xp[src], 1, 0))
        hist_s[v] = xp[src]
    xe_s[taps * SUBLANES:, :] = xp
    conv = xe_s[0:tt, :] * cw_ref[0:1, :]
    for i in range(1, CONV_WIDTH):
        conv = conv + xe_s[i * SUBLANES:i * SUBLANES + tt, :] * cw_ref[i:i + 1, :]
    conv = conv + cb_ref[...]

    z = _dot(conv.astype(BF16), wg_ref[...]) + bg_ref[...]
    r = _sigmoid(z[:, 0:w])
    gi = _sigmoid(z[:, w:])
    log_a = LRU_C * r * _log_sigmoid(lam_ref[...])
    p_idx = lax.broadcasted_iota(jnp.int32, (tt, 1), 0)
    time = it * tt + (p_idx & (SUBLANES - 1)) * seg + (p_idx >> 3)
    a = jnp.exp(log_a)
    one_minus_a2 = -jnp.tanh(log_a) * (a * a + 1.0)
    b = jnp.where(time >= n_pad, jnp.sqrt(one_minus_a2) * (gi * conv), 0.0)

    h_loc = jnp.zeros((SUBLANES, w), F32)
    prod = jnp.ones((SUBLANES, w), F32)
    for v in range(seg):
        rows = slice(v * SUBLANES, (v + 1) * SUBLANES)
        h_loc = a[rows] * h_loc + b[rows]
        prod = a[rows] * prod
        h_s[rows, :] = h_loc
        a_s[rows, :] = prod
    aa, bb = prod, h_loc
    for s in (1, 2, 4):
        keep = sub >= s
        bb = jnp.where(keep, aa * pltpu.roll(bb, s, 0) + bb, bb)
        aa = jnp.where(keep, aa * pltpu.roll(aa, s, 0), aa)
    h_end = aa * carry[...] + bb
    h_in = jnp.where(sub == 0, carry[...], pltpu.roll(h_end, 1, 0))
    carry[...] = h_end[SUBLANES - 1:SUBLANES, :]

    h_in_all = jnp.concatenate([h_in] * seg, axis=0)
    out = ((h_s[...] + a_s[...] * h_in_all) * gate).astype(BF16)
    o_ref[...] = _dot(unperm_ref[...], out).astype(BF16)


def _lru(xy3, cw, cb, wg, bg, lam, n_pad):
    b, tp, _ = xy3.shape
    tt = _row_tile(tp)
    w = LRU_WIDTH
    seg = tt // SUBLANES
    p_idx = jnp.arange(tt)
    perm = (jnp.arange(tt)[None, :] == ((p_idx % SUBLANES) * seg + p_idx // SUBLANES)[:, None]).astype(BF16)
    seqs = LRU_SEQS if b % LRU_SEQS == 0 else 1
    return pl.pallas_call(
        functools.partial(_lru_kernel, seqs=seqs, tt=tt, n_pad=n_pad),
        grid=(b // seqs, tp // tt),
        in_specs=[
            pl.BlockSpec((seqs, tt, C_COLS), lambda i, j: (i, j, 0)),
            _resident((tt, tt)), _resident((tt, tt)),
            _resident((CONV_WIDTH, w)), _resident((1, w)), _resident((w, 2 * w)), _resident((1, 2 * w)),
            _resident((1, w)),
        ],
        out_specs=pl.BlockSpec((seqs, tt, w), lambda i, j: (i, j, 0)),
        out_shape=jax.ShapeDtypeStruct((b, tp, w), BF16),
        scratch_shapes=[
            pltpu.VMEM((seqs, tt + (CONV_WIDTH - 1) * SUBLANES, w), F32), pltpu.VMEM((seqs, tt, w), F32),
            pltpu.VMEM((seqs, tt, w), F32), pltpu.VMEM((seqs, CONV_WIDTH - 1, SUBLANES, w), F32),
            pltpu.VMEM((seqs, 1, w), F32),
        ],
        compiler_params=pltpu.CompilerParams(dimension_semantics=("parallel", "arbitrary"), vmem_limit_bytes=VMEM_LIMIT),
        name="lru",
    )(xy3, perm, perm.T, cw, cb, wg, bg, lam)


FF_CHUNK = 256


def _mix_ffn_rows(h, oa, of, oc, gt_ref, wb_ref, wo_ref, gn_ref, w1_ref, w2_ref, act_s):
    d = D_MODEL
    merged = gt_ref[:, 0:d].astype(F32) * _dot(oa, wb_ref[0])
    merged = merged + gt_ref[:, d:2 * d].astype(F32) * _dot(of, wb_ref[1])
    merged = merged + gt_ref[:, 2 * d:3 * d].astype(F32) * _dot(oc, wb_ref[2])
    hm = h + _dot(merged.astype(BF16), wo_ref[...])
    u = _rms_norm(hm, gn_ref[...]).astype(BF16)
    for c in range(0, D_FF, FF_CHUNK):
        gate = _dot(u, w1_ref[:, c:c + FF_CHUNK])
        up = _dot(u, w1_ref[:, D_FF + c:D_FF + c + FF_CHUNK])
        act_s[:, c:c + FF_CHUNK] = (gate * _sigmoid(gate) * up).astype(BF16)
    return hm + _dot(act_s[...], w2_ref[...])


def _mix_ffn_kernel(h_ref, oa_ref, of_ref, oc_ref, gt_ref, wb_ref, wo_ref, gn_ref, w1_ref, w2_ref, out_ref, act_s,
                    *, tm, n_pad):
    j = pl.program_id(1)
    y = _mix_ffn_rows(h_ref[0], oa_ref[0], of_ref[0], oc_ref[0], gt_ref.at[0], wb_ref, wo_ref, gn_ref, w1_ref, w2_ref,
                      act_s)
    row = j * tm + lax.broadcasted_iota(jnp.int32, (tm, 1), 0)
    out_ref[0] = jnp.where(row >= n_pad, y, 0.0)


def _mix_ffn_last_kernel(h_ref, oa_ref, of_ref, oc_ref, gt_ref, wb_ref, wo_ref, gn_ref, w1_ref, w2_ref, gf_ref,
                         out_ref, act_s):
    y = _mix_ffn_rows(h_ref[...], oa_ref[...], of_ref[...], oc_ref[...], gt_ref, wb_ref, wo_ref, gn_ref, w1_ref,
                      w2_ref, act_s)
    out_ref[0] = _rms_norm(y, gf_ref[...])


def _mix_ffn_last(h3, oa, of, oc, gt3, wb, wo, gn, w1, w2, gf, skip_rows, seq):
    b, tp, d = h3.shape
    tm = 512 if seq % 512 == 0 else BLOCK
    w = LRU_WIDTH

    def rows(cols):
        return pl.BlockSpec((pl.Element(tm), pl.Element(cols)),
                            lambda i, j: (pl.multiple_of(i * tp + skip_rows + j * tm, BLOCK), 0))

    flat = lambda a: a.reshape(b * tp, a.shape[-1])
    return pl.pallas_call(
        _mix_ffn_last_kernel,
        grid=(b, seq // tm),
        in_specs=[
            rows(d), rows(w), rows(w), rows(w), rows(G_COLS),
            _resident((N_BRANCH, w, d)), _resident((d, d)), _resident((1, d)),
            _resident((d, 2 * D_FF)), _resident((D_FF, d)), _resident((1, d)),
        ],
        out_specs=pl.BlockSpec((1, tm, d), lambda i, j: (i, j, 0)),
        out_shape=jax.ShapeDtypeStruct((b, seq, d), F32),
        scratch_shapes=[pltpu.VMEM((tm, D_FF), BF16)],
        compiler_params=pltpu.CompilerParams(dimension_semantics=("parallel", "parallel"), vmem_limit_bytes=VMEM_LIMIT),
        name="mix_ffn_last",
    )(flat(h3), flat(oa), flat(of), flat(oc), flat(gt3), wb, wo, gn, w1, w2, gf)


def _mix_ffn(h3, oa, of, oc, gt3, wb, wo, gn, w1, w2, n_pad):
    b, tp, d = h3.shape
    tm = _row_tile(tp)
    w = LRU_WIDTH
    tile = lambda cols: pl.BlockSpec((1, tm, cols), lambda i, j: (i, j, 0))
    return pl.pallas_call(
        functools.partial(_mix_ffn_kernel, tm=tm, n_pad=n_pad),
        grid=(b, tp // tm),
        in_specs=[
            tile(d), tile(w), tile(w), tile(w), tile(G_COLS),
            _resident((N_BRANCH, w, d)), _resident((d, d)), _resident((1, d)),
            _resident((d, 2 * D_FF)), _resident((D_FF, d)),
        ],
        out_specs=tile(d),
        out_shape=jax.ShapeDtypeStruct((b, tp, d), F32),
        scratch_shapes=[pltpu.VMEM((tm, D_FF), BF16)],
        compiler_params=pltpu.CompilerParams(dimension_semantics=("parallel", "parallel"), vmem_limit_bytes=VMEM_LIMIT),
        name="mix_ffn",
    )(h3, oa, of, oc, gt3, wb, wo, gn, w1, w2)


def _t5_bucket(dist):
    max_exact = REL_BUCKETS // 2
    d = jnp.maximum(dist, 0)
    scaled = jnp.log(jnp.maximum(d, 1).astype(F32) / max_exact) / math.log(REL_MAX_DIST / max_exact)
    large = jnp.minimum(max_exact + (scaled * (REL_BUCKETS - max_exact)).astype(jnp.int32), REL_BUCKETS - 1)
    return jnp.where(d < max_exact, d, large)


def _swa_bias_kernel(bucket_ref, table_ref, o_ref):
    bucket = bucket_ref[...]
    for h in range(SWA_Q_HEADS):
        acc = jnp.full(bucket.shape, NEG_INF, F32)
        for c in range(REL_BUCKETS):
            acc = jnp.where(bucket == c, table_ref[c, h] * LOG2E, acc)
        o_ref[h * BLOCK:(h + 1) * BLOCK, :] = acc


def _swa_bias(rel_table):
    q_idx = jnp.arange(BLOCK)[:, None]
    k_idx = jnp.arange(2 * BLOCK)[None, :]
    dist = q_idx + BLOCK - k_idx
    bucket = jnp.where((dist >= 0) & (dist < SWA_WINDOW), _t5_bucket(dist), -1).astype(jnp.int32)
    return pl.pallas_call(
        _swa_bias_kernel,
        in_specs=[pl.BlockSpec(memory_space=pltpu.VMEM), pl.BlockSpec(memory_space=pltpu.SMEM)],
        out_specs=pl.BlockSpec(memory_space=pltpu.VMEM),
        out_shape=jax.ShapeDtypeStruct((SWA_Q_HEADS * BLOCK, 2 * BLOCK), F32),
        name="swa_bias",
    )(bucket, rel_table.astype(F32))


def _slab_head_order():
    half = SWA_Q_HEADS // 2
    return [h for j in range(half) for h in (j, j + half)]


def _pack_w_in(w_in):
    depth = w_in.shape[0]
    sizes = (512, 128, 128, 512, 512, 512, 8, 512, 512, 3072)
    offs = [0]
    for s in sizes:
        offs.append(offs[-1] + s)
    qa, ka, va, qf, kf, vf, fl, xc, yc, gates = [w_in[:, :, offs[i]:offs[i + 1]] for i in range(len(sizes))]
    scale = HEAD_DIM ** -0.5 * LOG2E
    qa = jnp.concatenate([qa[:, :, h * HEAD_DIM:(h + 1) * HEAD_DIM] for h in _slab_head_order()], axis=-1) * scale
    fl = jnp.pad(fl, ((0, 0), (0, 0), (0, LANES - FOX_HEADS)))
    packed = jnp.concatenate([qa, va, fl, qf * scale, vf, xc, yc, gates], axis=-1)
    keys_t = jnp.swapaxes(jnp.concatenate([kf, ka], axis=-1), 1, 2)
    return packed.astype(BF16), keys_t.astype(BF16)


def _block_diag(w):
    depth, nb, n, _ = w.shape
    eye = jnp.eye(nb, dtype=w.dtype)
    return jnp.einsum('lhij,hk->lhikj', w, eye).reshape(depth, nb * n, nb * n)


def kernel(x, meta_tokens, rel_bias_table, norm_mix, w_in, swa_sinks, fox_forget_bias, conv_w, conv_b,
           lru_w_r, lru_b_r, lru_w_i, lru_b_i, lru_lambda, w_branch, w_out, norm_ffn, w_ffn_in, w_ffn_out,
           norm_final):
    b, seq, d = x.shape
    depth = w_in.shape[0]
    t = N_META + seq
    n_pad = (-t) % BLOCK
    tp = t + n_pad
    assert d == D_MODEL and (n_pad + N_META) % BLOCK == 0

    meta = jnp.broadcast_to(meta_tokens.astype(x.dtype)[None], (b, N_META, d))
    h = jnp.concatenate([jnp.zeros((b, n_pad, d), x.dtype), meta, x], axis=1)

    w_in_p, wkt = _pack_w_in(w_in)
    bias_a = _swa_bias(rel_bias_table)
    sinks =jnp.broadcast_to((swa_sinks.astype(F32) * LOG2E)[:, :, None, None], (depth, SWA_Q_HEADS, BLOCK, LANES))
    sinks = sinks.reshape(depth, SWA_Q_HEADS * BLOCK, LANES)
    fb = jnp.pad(fox_forget_bias.astype(F32), ((0, 0), (0, LANES - FOX_HEADS)))[:, None, :]
    wg = jnp.concatenate([_block_diag(lru_w_r), _block_diag(lru_w_i)], axis=-1).astype(BF16)
    bg = jnp.concatenate([lru_b_r, lru_b_i], axis=-1).astype(F32)[:, None, :]
    wb_a = jnp.concatenate([w_branch[:, 0, h * HEAD_DIM:(h + 1) * HEAD_DIM] for h in _slab_head_order()], axis=1)
    wb = jnp.stack([wb_a, w_branch[:, 1], w_branch[:, 2]], axis=1).astype(BF16)
    wo = w_out.astype(BF16)
    w1 = w_ffn_in.astype(BF16)
    w2 = w_ffn_out.astype(BF16)

    for l in range(depth):
        qva, qvf, kt, fl, xy, gt = _in_proj(h.reshape(b * tp, d), norm_mix[l][None, :], w_in_p[l], wkt[l])
        ck = _fox_prefix(fl.reshape(b, tp, LANES), fb[l], n_pad)
        o_a = _swa(qva.reshape(b, tp, A_COLS), kt, bias_a, sinks[l], n_pad)
        o_f = _fox(qvf.reshape(b, tp, F_COLS), kt, ck, n_pad)
        o_c = _lru(xy.reshape(b, tp, C_COLS), conv_w[l], conv_b[l][None, :], wg[l], bg[l], lru_lambda[l][None, :],
                   n_pad)
        mix_args = (h, o_a, o_f, o_c, gt.reshape(b, tp, G_COLS), wb[l], wo[l], norm_ffn[l][None, :], w1[l], w2[l])
        if l + 1 < depth:
            h = _mix_ffn(*mix_args, n_pad)
    return _mix_ffn_last(*mix_args, norm_final[None, :], n_pad + N_META, seq)
```

```python
import functools
import math

import jax
import jax.numpy as jnp
from jax import lax
from jax.experimental import pallas as pl
from jax.experimental.pallas import tpu as pltpu

F32 = jnp.float32
BF16 = jnp.bfloat16

D_MODEL = 1024
HEAD_DIM = 64
N_META = 16
BLOCK = 128
LANES = 128
NEG_INF = -1e30
LOG2E = math.log2(math.e)
SWA_WINDOW = 128
SWA_Q_HEADS = 8
SWA_KV_HEADS = 2
FOX_HEADS = 8
LRU_WIDTH = D_MODEL // 2
LRU_BLOCKS = 8
CONV_WIDTH = 4
LRU_C = 8.0
REL_BUCKETS = 32
REL_MAX_DIST = 128
D_FF = 2816
N_BRANCH = 3
EPS = 1e-6

SWA_QCOLS = SWA_Q_HEADS * HEAD_DIM
A_COLS = SWA_QCOLS + SWA_KV_HEADS * HEAD_DIM
FOX_WIDTH = FOX_HEADS * HEAD_DIM
F_COLS = 2 * FOX_WIDTH
KT_ROWS = FOX_WIDTH + SWA_KV_HEADS * HEAD_DIM
C_COLS = 2 * LRU_WIDTH
G_COLS = N_BRANCH * D_MODEL
OFF_A = 0
OFF_FL = OFF_A + A_COLS
OFF_F = OFF_FL + LANES
OFF_C = OFF_F + F_COLS
OFF_G = OFF_C + C_COLS
IN_COLS_PACKED = OFF_G + G_COLS

VMEM_LIMIT = 56 * 1024 * 1024


def _sigmoid(x):
    return 0.5 * jnp.tanh(0.5 * x) + 0.5


def _log_sigmoid(x):
    return jnp.minimum(x, 0.0) - jnp.log1p(jnp.exp(-jnp.abs(x)))


def _gelu_tanh(x):
    c = math.sqrt(2.0 / math.pi)
    return 0.5 * x * (1.0 + jnp.tanh(c * (x + 0.044715 * (x * x * x))))


def _rms_norm(x, g):
    ms = jnp.mean(x * x, axis=-1, keepdims=True)
    return x * lax.rsqrt(ms + EPS) * g


def _dot(a, b):
    return jnp.dot(a, b, preferred_element_type=F32)


def _dot_nt(a, b):
    return lax.dot_general(a, b, (((1,), (1,)), ((), ())), preferred_element_type=F32)


def _resident(shape):
    nd = len(shape)
    return pl.BlockSpec(shape, lambda *_: (0,) * nd, pipeline_mode=pl.Buffered(1))


def _row_tile(tp):
    return 384 if tp % 384 == 0 else BLOCK


def _pv_operand(v, lo_k):
    zero = jnp.zeros_like(v)
    ind_lo = jnp.where(lo_k, 1.0, 0.0).astype(BF16)
    ind_hi = jnp.where(lo_k, 0.0, 1.0).astype(BF16)
    top = jnp.concatenate([jnp.where(lo_k, v, zero), ind_lo], axis=1)
    bot = jnp.concatenate([jnp.where(lo_k, zero, v), ind_hi], axis=1)
    return jnp.concatenate([top, bot], axis=0)


def _in_proj_kernel(x_ref, g_ref, w_ref, wkt_ref, qa_ref, qf_ref, kt_ref, fl_ref, xy_ref, gt_ref):
    u = _rms_norm(x_ref[...], g_ref[...]).astype(BF16)

    def proj(c0, width):
        return _dot(u, w_ref[:, c0:c0 + width])

    qa_ref[:, 0:SWA_QCOLS] = proj(OFF_A, SWA_QCOLS).astype(BF16)
    v_fl = proj(OFF_A + SWA_QCOLS, 2 * LANES)
    qa_ref[:, SWA_QCOLS:] = v_fl[:, 0:LANES].astype(BF16)
    fl_ref[...] = v_fl[:, LANES:]
    for c in range(0, F_COLS, 512):
        qf_ref[:, c:c + 512] = proj(OFF_F + c, 512).astype(BF16)
    kt_ref[...] = _dot_nt(wkt_ref[...], u).astype(BF16)
    xy_ref[:, 0:LRU_WIDTH] = proj(OFF_C, LRU_WIDTH).astype(BF16)
    xy_ref[:, LRU_WIDTH:] = _gelu_tanh(proj(OFF_C + LRU_WIDTH, LRU_WIDTH)).astype(BF16)
    for c in range(0, G_COLS, 512):
        gt_ref[:, c:c + 512] = _sigmoid(proj(OFF_G + c, 512)).astype(BF16)


def _in_proj(h2, g, w, wkt):
    rows = h2.shape[0]
    tm = next(t for t in (768, 384, BLOCK) if rows % t == 0)
    row_spec = lambda cols: pl.BlockSpec((tm, cols), lambda i: (i, 0))
    return pl.pallas_call(
        _in_proj_kernel,
        grid=(rows // tm,),
        in_specs=[row_spec(D_MODEL), _resident((1, D_MODEL)), _resident((D_MODEL, IN_COLS_PACKED)),
                  _resident((KT_ROWS, D_MODEL))],
        out_specs=[row_spec(A_COLS), row_spec(F_COLS), pl.BlockSpec((KT_ROWS, tm), lambda i: (0, i)),
                   row_spec(LANES), row_spec(C_COLS), row_spec(G_COLS)],
        out_shape=[
            jax.ShapeDtypeStruct((rows, A_COLS), BF16),
            jax.ShapeDtypeStruct((rows, F_COLS), BF16),
            jax.ShapeDtypeStruct((KT_ROWS, rows), BF16),
            jax.ShapeDtypeStruct((rows, LANES), F32),
            jax.ShapeDtypeStruct((rows, C_COLS), BF16),
            jax.ShapeDtypeStruct((rows, G_COLS), BF16),
        ],
        compiler_params=pltpu.CompilerParams(dimension_semantics=("parallel",), vmem_limit_bytes=VMEM_LIMIT),
        name="in_proj",
    )(h2, g, w, wkt)


def _fox_prefix_kernel(fl_ref, fb_ref, ck_ref, cum_ref, *, tp, n_pad):
    lf = _log_sigmoid(fl_ref[0] + fb_ref[...])
    row = lax.broadcasted_iota(jnp.int32, (tp, 1), 0)
    lf = jnp.where(row >= n_pad, lf, 0.0)
    r = lax.broadcasted_iota(jnp.int32, (BLOCK, BLOCK), 0)
    c = lax.broadcasted_iota(jnp.int32, (BLOCK, BLOCK), 1)
    tri = jnp.where(r >= c, 1.0, 0.0).astype(F32)
    carry = jnp.zeros((1, LANES), F32)
    for blk in range(tp // BLOCK):
        sl = slice(blk * BLOCK, (blk + 1) * BLOCK)
        cum = jnp.dot(tri, lf[sl], preferred_element_type=F32, precision=lax.Precision.HIGHEST) + carry
        cum_ref[sl, :] = cum
        carry = cum[BLOCK - 1:BLOCK, :]
    ck_ref[0] = cum_ref[...].T[0:FOX_HEADS, :]


def _fox_prefix(fl3, fb, n_pad):
    b, tp, _ = fl3.shape
    return pl.pallas_call(
        functools.partial(_fox_prefix_kernel, tp=tp, n_pad=n_pad),
        grid=(b,),
        in_specs=[pl.BlockSpec((1, tp, LANES), lambda i: (i, 0, 0)), _resident((1, LANES))],
        out_specs=pl.BlockSpec((1, FOX_HEADS, tp), lambda i: (i, 0, 0)),
        out_shape=jax.ShapeDtypeStruct((b, FOX_HEADS, tp), F32),
        scratch_shapes=[pltpu.VMEM((tp, LANES), F32)],
        compiler_params=pltpu.CompilerParams(dimension_semantics=("parallel",), vmem_limit_bytes=VMEM_LIMIT),
        name="fox_prefix",
    )(fl3, fb)


def _swa_kernel(q_ref, v_ref, vp_ref, kt_ref, ktp_ref, bias_ref, sink_ref, o_ref, *, blocks, n_pad):
    ib = pl.program_id(1)
    half = SWA_Q_HEADS // 2
    rows = half * BLOCK
    lo = lax.broadcasted_iota(jnp.int32, (BLOCK, LANES), 1) < HEAD_DIM
    lo_k = lax.broadcasted_iota(jnp.int32, (2 * BLOCK, LANES), 1) < HEAD_DIM
    lo_o = lax.broadcasted_iota(jnp.int32, (rows, LANES), 1) < HEAD_DIM
    kidx = lax.broadcasted_iota(jnp.int32, (1, BLOCK), 1)
    sink = sink_ref[...]
    for g in range(blocks):
        cur = slice(g * BLOCK, (g + 1) * BLOCK)
        if g == 0:
            kt_prev, v_prev = ktp_ref[...], vp_ref[0]
        else:
            prev = slice((g - 1) * BLOCK, g * BLOCK)
            kt_prev, v_prev = kt_ref[:, prev], v_ref[0, prev, :]
        kt_band = jnp.concatenate([kt_prev, kt_ref[:, cur]], axis=1)
        v_band = jnp.concatenate([v_prev, v_ref[0, cur, :]], axis=0)
        slabs = [q_ref[0, cur, j * LANES:(j + 1) * LANES] for j in range(half)]
        zero = jnp.zeros_like(slabs[0])
        q_all = jnp.concatenate([jnp.where(lo, s, zero) for s in slabs] + [jnp.where(lo, zero, s) for s in slabs],
                                axis=0)
        s = _dot(q_all, kt_band) + bias_ref[...]
        key0 = (ib * blocks + g - 1) * BLOCK + kidx
        s_prev = jnp.where(key0 >= n_pad, s[:, 0:BLOCK], NEG_INF)
        s_cur = jnp.where(key0 + BLOCK >= n_pad, s[:, BLOCK:], NEG_INF)
        m = jnp.maximum(jnp.max(jnp.maximum(s_prev, s_cur), axis=-1, keepdims=True), sink)
        p = jnp.concatenate([jnp.exp2(s_prev - m), jnp.exp2(s_cur - m)], axis=1).astype(BF16)
        pv = _dot(jnp.concatenate([p[0:rows], p[rows:]], axis=1), _pv_operand(v_band, lo_k))
        esink = jnp.exp2(sink - m)
        den = pv[:, LANES:] + jnp.where(lo_o, esink[0:rows], esink[rows:])
        o = (pv[:, 0:LANES] / den).astype(BF16)
        for j in range(half):
            o_ref[0, cur, j * LANES:(j + 1) * LANES] = o[j * BLOCK:(j + 1) * BLOCK]


def _swa(qv3, kt, bias, sinks, n_pad):
    b, tp, _ = qv3.shape
    nb = tp // BLOCK
    blocks = next(n for n in (11, 3, 1) if nb % n == 0)
    rows = blocks * BLOCK
    steps = nb // blocks
    vblk = SWA_QCOLS // LANES
    krow = FOX_WIDTH // LANES
    return pl.pallas_call(
        functools.partial(_swa_kernel, blocks=blocks, n_pad=n_pad),
        grid=(b, steps),
        in_specs=[
            pl.BlockSpec((1, rows, SWA_QCOLS), lambda i, j: (i, j, 0)),
            pl.BlockSpec((1, rows, LANES), lambda i, j: (i, j, vblk)),
            pl.BlockSpec((1, BLOCK, LANES), lambda i, j: (i, jnp.maximum(j * blocks - 1, 0), vblk)),
            pl.BlockSpec((LANES, rows), lambda i, j: (krow, i * steps + j)),
            pl.BlockSpec((LANES, BLOCK), lambda i, j: (krow, jnp.maximum((i * steps + j) * blocks - 1, 0))),
            _resident((SWA_Q_HEADS * BLOCK, 2 * BLOCK)),
            _resident((SWA_Q_HEADS * BLOCK, LANES)),
        ],
        out_specs=pl.BlockSpec((1, rows, SWA_QCOLS), lambda i, j: (i, j, 0)),
        out_shape=jax.ShapeDtypeStruct((b, tp, SWA_QCOLS), BF16),
        compiler_params=pltpu.CompilerParams(dimension_semantics=("parallel", "parallel"), vmem_limit_bytes=VMEM_LIMIT),
        name="swa",
    )(qv3, qv3, qv3, kt, kt, bias, sinks)


FOX_TQ = 384
FOX_TK = 512


def _fox_kernel(q_ref, v_ref, kt_ref, ck_ref, o_ref, qa_s, m_s, acc_s, *, tq, tk, n_tiles, tail, n_pad):
    i = pl.program_id(1)
    pairs = FOX_HEADS // 2
    lane = lax.broadcasted_iota(jnp.int32, (tq, LANES), 1)
    lo = lane < HEAD_DIM
    for hp in range(pairs):
        q = q_ref[0, :, hp * LANES:(hp + 1) * LANES]
        zero = jnp.zeros_like(q)
        qa_s[hp, 0:tq, :] = jnp.where(lo, q, zero)
        qa_s[hp, tq:, :] = jnp.where(lo, zero, q)
    m_s[...] = jnp.full(m_s.shape, NEG_INF, F32)
    acc_s[...] = jnp.zeros(acc_s.shape, F32)

    def tile(start, size, masked):
        key_ok = start + lax.broadcasted_iota(jnp.int32, (1, size), 1) >= n_pad
        lo_k = lax.broadcasted_iota(jnp.int32, (size, LANES), 1) < HEAD_DIM
        if masked:
            rel = lax.broadcasted_iota(jnp.int32, (tq, size), 1) - lax.broadcasted_iota(jnp.int32, (tq, size), 0)
            ok = rel <= i * tq - start
        for hp in range(pairs):
            kt = kt_ref[hp * LANES:(hp + 1) * LANES, pl.ds(start, size)]
            s2 = _dot(qa_s[hp], kt)
            ps, alphas = [], []
            for e in range(2):
                c = jnp.where(key_ok, ck_ref[0, 2 * hp + e:2 * hp + e + 1, pl.ds(start, size)] * (-LOG2E), NEG_INF)
                s = s2[e * tq:(e + 1) * tq] + c
                if masked:
                    s = jnp.where(ok, s, NEG_INF)
                chunks = [s[:, c0:c0 + LANES] for c0 in range(0, size, LANES)]
                m_prev = m_s[2 * hp + e]
                m_new = jnp.maximum(m_prev, jnp.max(functools.reduce(jnp.maximum, chunks), axis=-1, keepdims=True))
                alphas.append(jnp.exp2(m_prev - m_new))
                ps.append(jnp.concatenate([jnp.exp2(ch - m_new) for ch in chunks], axis=1).astype(BF16))
                m_s[2 * hp + e] = m_new
            alpha = jnp.where(lo, alphas[0], alphas[1])
            alpha = jnp.concatenate([alpha, alpha], axis=1)
            v = v_ref[0, pl.ds(start, size), hp * LANES:(hp + 1) * LANES]
            acc_s[hp] = alpha * acc_s[hp] + _dot(jnp.concatenate(ps, axis=1), _pv_operand(v, lo_k))

    n_last = jnp.minimum((tq * (i + 1) - 1) // tk, n_tiles - 1)
    n_free = (tq * i + 1) // tk

    def full_tile(j, masked):
        tile(pl.multiple_of(j * tk, tk), tk, masked)

    def two_free(j2, carry):
        full_tile(2 * j2, False)
        full_tile(2 * j2 + 1, False)
        return carry

    def one_masked(j, carry):
        full_tile(j, True)
        return carry

    lax.fori_loop(0, n_free // 2, two_free, 0)
    pl.when(n_free % 2 == 1)(lambda: full_tile(n_free - 1, False))
    lax.fori_loop(n_free, n_last, one_masked, 0)
    last_start = pl.multiple_of(n_last * tk, tk)
    reach = jnp.minimum(tq * (i + 1) - n_last * tk, tk)
    for size in range(LANES, tk + 1, LANES):
        pl.when(reach == size)(functools.partial(tile, last_start, size, True))
    if tail:
        pl.when(tq * (i + 1) > n_tiles * tk)(lambda: tile(n_tiles * tk, tail, True))

    for hp in range(pairs):
        acc = acc_s[hp]
        o_ref[0, :, hp * LANES:(hp + 1) * LANES] = (acc[:, 0:LANES] / acc[:, LANES:]).astype(BF16)


def _fox(qv3, kt, ck, n_pad):
    b, tp, _ = qv3.shape
    tq, tk = FOX_TQ, FOX_TK
    n_tiles, tail = tp // tk, tp % tk
    assert tp % tq == 0 and tail % LANES == 0 and tail <= tq and n_tiles >= 1
    pairs = FOX_HEADS // 2
    return pl.pallas_call(
        functools.partial(_fox_kernel, tq=tq, tk=tk, n_tiles=n_tiles, tail=tail, n_pad=n_pad),
        grid=(b, tp // tq),
        in_specs=[
            pl.BlockSpec((1, tq, FOX_WIDTH), lambda i, j: (i, j, 0)),
            pl.BlockSpec((1, tp, FOX_WIDTH), lambda i, j: (i, 0, 1)),
            pl.BlockSpec((FOX_WIDTH, tp), lambda i, j: (0, i)),
            pl.BlockSpec((1, FOX_HEADS, tp), lambda i, j: (i, 0, 0)),
        ],
        out_specs=pl.BlockSpec((1, tq, FOX_WIDTH), lambda i, j: (i, j, 0)),
        out_shape=jax.ShapeDtypeStruct((b, tp, FOX_WIDTH), BF16),
        scratch_shapes=[
            pltpu.VMEM((pairs, 2 * tq, LANES), BF16),
            pltpu.VMEM((FOX_HEADS, tq, LANES), F32),
            pltpu.VMEM((pairs, tq, 2 * LANES), F32),
        ],
        compiler_params=pltpu.CompilerParams(
            dimension_semantics=("parallel", "arbitrary"), vmem_limit_bytes=VMEM_LIMIT),
        name="fox",
    )(qv3, qv3, kt, ck)


SUBLANES = 8


LRU_SEQS = 2


def _lru_kernel(xy_ref, perm_ref, unperm_ref, cw_ref, cb_ref, wg_ref, bg_ref, lam_ref, o_ref,
                xe_s, a_s, h_s, hist_s, carry, *, seqs, tt, n_pad):
    @pl.when(pl.program_id(1) == 0)
    def _():
        hist_s[...] = jnp.zeros(hist_s.shape, F32)
        carry[...] = jnp.zeros(carry.shape, F32)

    for q in range(seqs):
        _lru_one(xy_ref.at[q], perm_ref, unperm_ref, cw_ref, cb_ref, wg_ref, bg_ref, lam_ref, o_ref.at[q],
                 xe_s.at[q], a_s.at[q], h_s.at[q], hist_s.at[q], carry.at[q], tt=tt, n_pad=n_pad)


def _lru_one(xy_ref, perm_ref, unperm_ref, cw_ref, cb_ref, wg_ref, bg_ref, lam_ref, o_ref,
             xe_s, a_s, h_s, hist_s, carry, *, tt, n_pad):
    it = pl.program_id(1)
    seg = tt // SUBLANES
    taps = CONV_WIDTH - 1
    w = LRU_WIDTH
    sub = lax.broadcasted_iota(jnp.int32, (SUBLANES, w), 0)
    xyp = _dot(perm_ref[...], xy_ref[...])
    xp, gate = xyp[:, 0:w], xyp[:, w:]

    for v in range(taps):
        src = slice((seg - taps + v) * SUBLANES, (seg - taps + v + 1) * SUBLANES)
        xe_s[v * SUBLANES:(v + 1) * SUBLANES, :] = jnp.where(
            sub == 0, pltpu.roll(hist_s[v], 1, 0), pltpu.roll(xp[src], 1, 0))
        hist_s[v] = xp[src]
    xe_s[taps * SUBLANES:, :] = xp
    conv = xe_s[0:tt, :] * cw_ref[0:1, :]
    for i in range(1, CONV_WIDTH):
        conv = conv + xe_s[i * SUBLANES:i * SUBLANES + tt, :] * cw_ref[i:i + 1, :]
    conv = conv + cb_ref[...]

    t = jnp.tanh(_dot(conv.astype(BF16), wg_ref[...]) + bg_ref[...])
    k = (0.5 * LRU_C) * _log_sigmoid(lam_ref[...])
    log_a = k * t[:, 0:w] + k
    p_idx = lax.broadcasted_iota(jnp.int32, (tt, 1), 0)
    time = it * tt + (p_idx & (SUBLANES - 1)) * seg + (p_idx >> 3)
    a = jnp.exp(log_a)
    quarter_one_minus_a2 = (-0.25 * jnp.tanh(log_a)) * (a * a + 1.0)
    b = jnp.where(time >= n_pad, jnp.sqrt(quarter_one_minus_a2) * (t[:, w:] * conv + conv), 0.0)

    h_loc = jnp.zeros((SUBLANES, w), F32)
    prod = jnp.ones((SUBLANES, w), F32)
    for v in range(seg):
        rows = slice(v * SUBLANES, (v + 1) * SUBLANES)
        h_loc = a[rows] * h_loc + b[rows]
        prod = a[rows] * prod
        h_s[rows, :] = h_loc
        a_s[rows, :] = prod
    aa, bb = prod, h_loc
    for s in (1, 2, 4):
        keep = sub >= s
        bb = jnp.where(keep, aa * pltpu.roll(bb, s, 0) + bb, bb)
        aa = jnp.where(keep, aa * pltpu.roll(aa, s, 0), aa)
    h_end = aa * carry[...] + bb
    h_in = jnp.where(sub == 0, carry[...], pltpu.roll(h_end, 1, 0))
    carry[...] = h_end[SUBLANES - 1:SUBLANES, :]

    h_in_all = jnp.concatenate([h_in] * seg, axis=0)
    out = ((h_s[...] + a_s[...] * h_in_all) * gate).astype(BF16)
    o_ref[...] = _dot(unperm_ref[...], out).astype(BF16)


def _lru(xy3, cw, cb, wg, bg, lam, n_pad):
    b, tp, _ = xy3.shape
    tt = _row_tile(tp)
    w = LRU_WIDTH
    seg = tt // SUBLANES
    p_idx = jnp.arange(tt)
    perm = (jnp.arange(tt)[None, :] == ((p_idx % SUBLANES) * seg + p_idx // SUBLANES)[:, None]).astype(BF16)
    seqs = LRU_SEQS if b % LRU_SEQS == 0 else 1
    return pl.pallas_call(
        functools.partial(_lru_kernel, seqs=seqs, tt=tt, n_pad=n_pad),
        grid=(b // seqs, tp // tt),
        in_specs=[
            pl.BlockSpec((seqs, tt, C_COLS), lambda i, j: (i, j, 0)),
            _resident((tt, tt)), _resident((tt, tt)),
            _resident((CONV_WIDTH, w)), _resident((1, w)), _resident((w, 2 * w)), _resident((1, 2 * w)),
            _resident((1, w)),
        ],
        out_specs=pl.BlockSpec((seqs, tt, w), lambda i, j: (i, j, 0)),
        out_shape=jax.ShapeDtypeStruct((b, tp, w), BF16),
        scratch_shapes=[
            pltpu.VMEM((seqs, tt + (CONV_WIDTH - 1) * SUBLANES, w), F32), pltpu.VMEM((seqs, tt, w), F32),
            pltpu.VMEM((seqs, tt, w), F32), pltpu.VMEM((seqs, CONV_WIDTH - 1, SUBLANES, w), F32),
            pltpu.VMEM((seqs, 1, w), F32),
        ],
        compiler_params=pltpu.CompilerParams(dimension_semantics=("parallel", "arbitrary"), vmem_limit_bytes=VMEM_LIMIT),
        name="lru",
    )(xy3, perm, perm.T, cw, cb, wg, bg, lam)


FF_CHUNK = 256


def _mix_ffn_rows(h, oa, of, oc, gt_ref, wb_ref, wo_ref, gn_ref, w1_ref, w2_ref, act_s):
    d = D_MODEL
    merged = gt_ref[:, 0:d].astype(F32) * _dot(oa, wb_ref[0])
    merged = merged + gt_ref[:, d:2 * d].astype(F32) * _dot(of, wb_ref[1])
    merged = merged + gt_ref[:, 2 * d:3 * d].astype(F32) * _dot(oc, wb_ref[2])
    hm = h + _dot(merged.astype(BF16), wo_ref[...])
    u = _rms_norm(hm, gn_ref[...]).astype(BF16)
    for c in range(0, D_FF, FF_CHUNK):
        gate = _dot(u, w1_ref[:, c:c + FF_CHUNK])
        up = _dot(u, w1_ref[:, D_FF + c:D_FF + c + FF_CHUNK])
        act_s[:, c:c + FF_CHUNK] = (gate * _sigmoid(gate) * up).astype(BF16)
    return hm + _dot(act_s[...], w2_ref[...])


def _mix_ffn_kernel(h_ref, oa_ref, of_ref, oc_ref, gt_ref, wb_ref, wo_ref, gn_ref, w1_ref, w2_ref, out_ref, act_s,
                    *, tm, n_pad):
    j = pl.program_id(1)
    y = _mix_ffn_rows(h_ref[0], oa_ref[0], of_ref[0], oc_ref[0], gt_ref.at[0], wb_ref, wo_ref, gn_ref, w1_ref, w2_ref,
                      act_s)
    row = j * tm + lax.broadcasted_iota(jnp.int32, (tm, 1), 0)
    out_ref[0] = jnp.where(row >= n_pad, y, 0.0)


def _mix_ffn_last_kernel(h_ref, oa_ref, of_ref, oc_ref, gt_ref, wb_ref, wo_ref, gn_ref, w1_ref, w2_ref, gf_ref,
                         out_ref, act_s):
    y = _mix_ffn_rows(h_ref[...], oa_ref[...], of_ref[...], oc_ref[...], gt_ref, wb_ref, wo_ref, gn_ref, w1_ref,
                      w2_ref, act_s)
    out_ref[0] = _rms_norm(y, gf_ref[...])


def _mix_ffn_last(h3, oa, of, oc, gt3, wb, wo, gn, w1, w2, gf, skip_rows, seq):
    b, tp, d = h3.shape
    tm = 512 if seq % 512 == 0 else BLOCK
    w = LRU_WIDTH

    def rows(cols):
        return pl.BlockSpec((pl.Element(tm), pl.Element(cols)),
                            lambda i, j: (pl.multiple_of(i * tp + skip_rows + j * tm, BLOCK), 0))

    flat = lambda a: a.reshape(b * tp, a.shape[-1])
    return pl.pallas_call(
        _mix_ffn_last_kernel,
        grid=(b, seq // tm),
        in_specs=[
            rows(d), rows(w), rows(w), rows(w), rows(G_COLS),
            _resident((N_BRANCH, w, d)), _resident((d, d)), _resident((1, d)),
            _resident((d, 2 * D_FF)), _resident((D_FF, d)), _resident((1, d)),
        ],
        out_specs=pl.BlockSpec((1, tm, d), lambda i, j: (i, j, 0)),
        out_shape=jax.ShapeDtypeStruct((b, seq, d), F32),
        scratch_shapes=[pltpu.VMEM((tm, D_FF), BF16)],
        compiler_params=pltpu.CompilerParams(dimension_semantics=("parallel", "parallel"), vmem_limit_bytes=VMEM_LIMIT),
        name="mix_ffn_last",
    )(flat(h3), flat(oa), flat(of), flat(oc), flat(gt3), wb, wo, gn, w1, w2, gf)


def _mix_ffn(h3, oa, of, oc, gt3, wb, wo, gn, w1, w2, n_pad):
    b, tp, d = h3.shape
    tm = _row_tile(tp)
    w = LRU_WIDTH
    tile = lambda cols: pl.BlockSpec((1, tm, cols), lambda i, j: (i, j, 0))
    return pl.pallas_call(
        functools.partial(_mix_ffn_kernel, tm=tm, n_pad=n_pad),
        grid=(b, tp // tm),
        in_specs=[
            tile(d), tile(w), tile(w), tile(w), tile(G_COLS),
            _resident((N_BRANCH, w, d)), _resident((d, d)), _resident((1, d)),
            _resident((d, 2 * D_FF)), _resident((D_FF, d)),
        ],
        out_specs=tile(d),
        out_shape=jax.ShapeDtypeStruct((b, tp, d), F32),
        scratch_shapes=[pltpu.VMEM((tm, D_FF), BF16)],
        compiler_params=pltpu.CompilerParams(dimension_semantics=("parallel", "parallel"), vmem_limit_bytes=VMEM_LIMIT),
        name="mix_ffn",
    )(h3, oa, of, oc, gt3, wb, wo, gn, w1, w2)


def _t5_bucket(dist):
    max_exact = REL_BUCKETS // 2
    d = jnp.maximum(dist, 0)
    scaled = jnp.log(jnp.maximum(d, 1).astype(F32) / max_exact) / math.log(REL_MAX_DIST / max_exact)
    large = jnp.minimum(max_exact + (scaled * (REL_BUCKETS - max_exact)).astype(jnp.int32), REL_BUCKETS - 1)
    return jnp.where(d < max_exact, d, large)


def _swa_bias_kernel(bucket_ref, table_ref, o_ref):
    bucket = bucket_ref[...]
    for h in range(SWA_Q_HEADS):
        acc = jnp.full(bucket.shape, NEG_INF, F32)
        for c in range(REL_BUCKETS):
            acc = jnp.where(bucket == c, table_ref[c, h] * LOG2E, acc)
        o_ref[h * BLOCK:(h + 1) * BLOCK, :] = acc


def _swa_bias(rel_table):
    q_idx = jnp.arange(BLOCK)[:, None]
    k_idx = jnp.arange(2 * BLOCK)[None, :]
    dist = q_idx + BLOCK - k_idx
    bucket = jnp.where((dist >= 0) & (dist < SWA_WINDOW), _t5_bucket(dist), -1).astype(jnp.int32)
    return pl.pallas_call(
        _swa_bias_kernel,
        in_specs=[pl.BlockSpec(memory_space=pltpu.VMEM), pl.BlockSpec(memory_space=pltpu.SMEM)],
        out_specs=pl.BlockSpec(memory_space=pltpu.VMEM),
        out_shape=jax.ShapeDtypeStruct((SWA_Q_HEADS * BLOCK, 2 * BLOCK), F32),
        name="swa_bias",
    )(bucket, rel_table.astype(F32))


def _slab_head_order():
    half = SWA_Q_HEADS // 2
    return [h for j in range(half) for h in (j, j + half)]


def _pack_w_in(w):
    sizes = (512, 128, 128, 512, 512, 512, 8, 512, 512, 3072)
    offs = [0]
    for s in sizes:
        offs.append(offs[-1] + s)
    qa, ka, va, qf, kf, vf, fl, xc, yc, gates = [w[:, offs[i]:offs[i + 1]] for i in range(len(sizes))]
    scale = HEAD_DIM ** -0.5 * LOG2E
    qa = jnp.concatenate([qa[:, h * HEAD_DIM:(h + 1) * HEAD_DIM] for h in _slab_head_order()], axis=-1) * scale
    fl = jnp.pad(fl, ((0, 0), (0, LANES - FOX_HEADS)))
    packed = jnp.concatenate([p.astype(BF16) for p in (qa, va, fl, qf * scale, vf, xc, yc, gates)], axis=-1)
    keys_t = jnp.concatenate([kf, ka], axis=-1).astype(BF16).T
    return packed, keys_t


def _block_diag(w):
    depth, nb, n, _ = w.shape
    eye = jnp.eye(nb, dtype=w.dtype)
    return jnp.einsum('lhij,hk->lhikj', w, eye).reshape(depth, nb * n, nb * n)


def kernel(x, meta_tokens, rel_bias_table, norm_mix, w_in, swa_sinks, fox_forget_bias, conv_w, conv_b,
           lru_w_r, lru_b_r, lru_w_i, lru_b_i, lru_lambda, w_branch, w_out, norm_ffn, w_ffn_in, w_ffn_out,
           norm_final):
    b, seq, d = x.shape
    depth = w_in.shape[0]
    t = N_META + seq
    n_pad = (-t) % BLOCK
    tp = t + n_pad
    assert d == D_MODEL and (n_pad + N_META) % BLOCK == 0

    meta = jnp.broadcast_to(meta_tokens.astype(x.dtype)[None], (b, N_META, d))
    h = jnp.concatenate([jnp.zeros((b, n_pad, d), x.dtype), meta, x], axis=1)

    bias_a = _swa_bias(rel_bias_table)
    sinks = jnp.broadcast_to((swa_sinks.astype(F32) * LOG2E)[:, :, None, None], (depth, SWA_Q_HEADS, BLOCK, LANES))
    sinks = sinks.reshape(depth, SWA_Q_HEADS * BLOCK, LANES)
    fb = jnp.pad(fox_forget_bias.astype(F32), ((0, 0), (0, LANES - FOX_HEADS)))[:, None, :]
    wg = (0.5 * jnp.concatenate([_block_diag(lru_w_r), _block_diag(lru_w_i)], axis=-1)).astype(BF16)
    bg = 0.5 * jnp.concatenate([lru_b_r, lru_b_i], axis=-1).astype(F32)[:, None, :]
    wb_a = jnp.concatenate([w_branch[:, 0, h * HEAD_DIM:(h + 1) * HEAD_DIM] for h in _slab_head_order()], axis=1)
    wb = jnp.stack([wb_a, w_branch[:, 1], w_branch[:, 2]], axis=1).astype(BF16)
    wo = w_out.astype(BF16)
    w1 = w_ffn_in.astype(BF16)
    w2 = w_ffn_out.astype(BF16)

    for l in range(depth):
        qva, qvf, kt, fl, xy, gt = _in_proj(h.reshape(b * tp, d), norm_mix[l][None, :], *_pack_w_in(w_in[l]))
        ck = _fox_prefix(fl.reshape(b, tp, LANES), fb[l], n_pad)
        o_a = _swa(qva.reshape(b, tp, A_COLS), kt, bias_a, sinks[l], n_pad)
        o_f = _fox(qvf.reshape(b, tp, F_COLS), kt, ck, n_pad)
        o_c = _lru(xy.reshape(b, tp, C_COLS), conv_w[l], conv_b[l][None, :], wg[l], bg[l], lru_lambda[l][None, :],
                   n_pad)
        mix_args = (h, o_a, o_f, o_c, gt.reshape(b, tp, G_COLS), wb[l], wo[l], norm_ffn[l][None, :], w1[l], w2[l])
        if l + 1 < depth:
            h = _mix_ffn(*mix_args, n_pad)
    return _mix_ffn_last(*mix_args, norm_final[None, :], n_pad + N_META, seq)
```

```python
import functools
import math

import jax
import jax.numpy as jnp
from jax import lax
from jax.experimental import pallas as pl
from jax.experimental.pallas import tpu as pltpu

F32 = jnp.float32
BF16 = jnp.bfloat16

D_MODEL = 1024
HEAD_DIM = 64
N_META = 16
BLOCK = 128
LANES = 128
NEG_INF = -1e30
LOG2E = math.log2(math.e)
SWA_WINDOW = 128
SWA_Q_HEADS = 8
SWA_KV_HEADS = 2
FOX_HEADS = 8
LRU_WIDTH = D_MODEL // 2
LRU_BLOCKS = 8
CONV_WIDTH = 4
LRU_C = 8.0
REL_BUCKETS = 32
REL_MAX_DIST = 128
D_FF = 2816
N_BRANCH = 3
EPS = 1e-6

SWA_QCOLS = SWA_Q_HEADS * HEAD_DIM
A_COLS = SWA_QCOLS + SWA_KV_HEADS * HEAD_DIM
FOX_WIDTH = FOX_HEADS * HEAD_DIM
F_COLS = 2 * FOX_WIDTH
KT_ROWS = FOX_WIDTH + SWA_KV_HEADS * HEAD_DIM
C_COLS = 2 * LRU_WIDTH
G_COLS = N_BRANCH * D_MODEL
OFF_A = 0
OFF_FL = OFF_A + A_COLS
OFF_F = OFF_FL + LANES
OFF_C = OFF_F + F_COLS
OFF_G = OFF_C + C_COLS
IN_COLS_PACKED = OFF_G + G_COLS

VMEM_LIMIT = 56 * 1024 * 1024


def _sigmoid(x):
    return 0.5 * jnp.tanh(0.5 * x) + 0.5


def _log_sigmoid(x):
    return jnp.minimum(x, 0.0) - jnp.log1p(jnp.exp(-jnp.abs(x)))


def _gelu_tanh(x):
    c = math.sqrt(2.0 / math.pi)
    return 0.5 * x * (1.0 + jnp.tanh(c * (x + 0.044715 * (x * x * x))))


def _rms_norm(x, g):
    ms = jnp.mean(x * x, axis=-1, keepdims=True)
    return x * lax.rsqrt(ms + EPS) * g


def _dot(a, b):
    return jnp.dot(a, b, preferred_element_type=F32)


def _dot_nt(a, b):
    return lax.dot_general(a, b, (((1,), (1,)), ((), ())), preferred_element_type=F32)


def _resident(shape):
    nd = len(shape)
    return pl.BlockSpec(shape, lambda *_: (0,) * nd, pipeline_mode=pl.Buffered(1))


def _row_tile(tp):
    return 384 if tp % 384 == 0 else BLOCK


def _pv_operand(v, lo_k):
    zero = jnp.zeros_like(v)
    ind_lo = jnp.where(lo_k, 1.0, 0.0).astype(BF16)
    ind_hi = jnp.where(lo_k, 0.0, 1.0).astype(BF16)
    top = jnp.concatenate([jnp.where(lo_k, v, zero), ind_lo], axis=1)
    bot = jnp.concatenate([jnp.where(lo_k, zero, v), ind_hi], axis=1)
    return jnp.concatenate([top, bot], axis=0)


def _in_proj_kernel(x_ref, g_ref, w_ref, wkt_ref, qa_ref, qf_ref, kt_ref, fl_ref, xy_ref, gt_ref):
    u = _rms_norm(x_ref[...], g_ref[...]).astype(BF16)

    def proj(c0, width):
        return _dot(u, w_ref[:, c0:c0 + width])

    qa_ref[:, 0:SWA_QCOLS] = proj(OFF_A, SWA_QCOLS).astype(BF16)
    v_fl = proj(OFF_A + SWA_QCOLS, 2 * LANES)
    qa_ref[:, SWA_QCOLS:] = v_fl[:, 0:LANES].astype(BF16)
    fl_ref[...] = v_fl[:, LANES:]
    for c in range(0, F_COLS, 512):
        qf_ref[:, c:c + 512] = proj(OFF_F + c, 512).astype(BF16)
    kt_ref[...] = _dot_nt(wkt_ref[...], u).astype(BF16)
    xy_ref[:, 0:LRU_WIDTH] = proj(OFF_C, LRU_WIDTH).astype(BF16)
    xy_ref[:, LRU_WIDTH:] = _gelu_tanh(proj(OFF_C + LRU_WIDTH, LRU_WIDTH)).astype(BF16)
    for c in range(0, G_COLS, 512):
        gt_ref[:, c:c + 512] = _sigmoid(proj(OFF_G + c, 512)).astype(BF16)


def _in_proj(h2, g, w, wkt):
    rows = h2.shape[0]
    tm = next(t for t in (768, 384, BLOCK) if rows % t == 0)
    row_spec = lambda cols: pl.BlockSpec((tm, cols), lambda i: (i, 0))
    return pl.pallas_call(
        _in_proj_kernel,
        grid=(rows // tm,),
        in_specs=[row_spec(D_MODEL), _resident((1, D_MODEL)), _resident((D_MODEL, IN_COLS_PACKED)),
                  _resident((KT_ROWS, D_MODEL))],
        out_specs=[row_spec(A_COLS), row_spec(F_COLS), pl.BlockSpec((KT_ROWS, tm), lambda i: (0, i)),
                   row_spec(LANES), row_spec(C_COLS), row_spec(G_COLS)],
        out_shape=[
            jax.ShapeDtypeStruct((rows, A_COLS), BF16),
            jax.ShapeDtypeStruct((rows, F_COLS), BF16),
            jax.ShapeDtypeStruct((KT_ROWS, rows), BF16),
            jax.ShapeDtypeStruct((rows, LANES), F32),
            jax.ShapeDtypeStruct((rows, C_COLS), BF16),
            jax.ShapeDtypeStruct((rows, G_COLS), BF16),
        ],
        compiler_params=pltpu.CompilerParams(dimension_semantics=("parallel",), vmem_limit_bytes=VMEM_LIMIT),
        name="in_proj",
    )(h2, g, w, wkt)


def _fox_prefix_kernel(fl_ref, fb_ref, ck_ref, cum_ref, *, tp, n_pad):
    lf = _log_sigmoid(fl_ref[0] + fb_ref[...])
    row = lax.broadcasted_iota(jnp.int32, (tp, 1), 0)
    lf = jnp.where(row >= n_pad, lf, 0.0)
    r = lax.broadcasted_iota(jnp.int32, (BLOCK, BLOCK), 0)
    c = lax.broadcasted_iota(jnp.int32, (BLOCK, BLOCK), 1)
    tri = jnp.where(r >= c, 1.0, 0.0).astype(F32)
    carry = jnp.zeros((1, LANES), F32)
    for blk in range(tp // BLOCK):
        sl = slice(blk * BLOCK, (blk + 1) * BLOCK)
        cum = jnp.dot(tri, lf[sl], preferred_element_type=F32, precision=lax.Precision.HIGHEST) + carry
        cum_ref[sl, :] = cum
        carry = cum[BLOCK - 1:BLOCK, :]
    ck_ref[0] = cum_ref[...].T[0:FOX_HEADS, :]


def _fox_prefix(fl3, fb, n_pad):
    b, tp, _ = fl3.shape
    return pl.pallas_call(
        functools.partial(_fox_prefix_kernel, tp=tp, n_pad=n_pad),
        grid=(b,),
        in_specs=[pl.BlockSpec((1, tp, LANES), lambda i: (i, 0, 0)), _resident((1, LANES))],
        out_specs=pl.BlockSpec((1, FOX_HEADS, tp), lambda i: (i, 0, 0)),
        out_shape=jax.ShapeDtypeStruct((b, FOX_HEADS, tp), F32),
        scratch_shapes=[pltpu.VMEM((tp, LANES), F32)],
        compiler_params=pltpu.CompilerParams(dimension_semantics=("parallel",), vmem_limit_bytes=VMEM_LIMIT),
        name="fox_prefix",
    )(fl3, fb)


def _swa_kernel(q_ref, v_ref, vp_ref, kt_ref, ktp_ref, bias_ref, sink_ref, o_ref, *, blocks, n_pad):
    ib = pl.program_id(1)
    half = SWA_Q_HEADS // 2
    rows = half * BLOCK
    lo = lax.broadcasted_iota(jnp.int32, (BLOCK, LANES), 1) < HEAD_DIM
    lo_k = lax.broadcasted_iota(jnp.int32, (2 * BLOCK, LANES), 1) < HEAD_DIM
    lo_o = lax.broadcasted_iota(jnp.int32, (rows, LANES), 1) < HEAD_DIM
    kidx = lax.broadcasted_iota(jnp.int32, (1, BLOCK), 1)
    sink = sink_ref[...]
    for g in range(blocks):
        cur = slice(g * BLOCK, (g + 1) * BLOCK)
        if g == 0:
            kt_prev, v_prev = ktp_ref[...], vp_ref[0]
        else:
            prev = slice((g - 1) * BLOCK, g * BLOCK)
            kt_prev, v_prev = kt_ref[:, prev], v_ref[0, prev, :]
        kt_band = jnp.concatenate([kt_prev, kt_ref[:, cur]], axis=1)
        v_band = jnp.concatenate([v_prev, v_ref[0, cur, :]], axis=0)
        slabs = [q_ref[0, cur, j * LANES:(j + 1) * LANES] for j in range(half)]
        zero = jnp.zeros_like(slabs[0])
        q_all = jnp.concatenate([jnp.where(lo, s, zero) for s in slabs] + [jnp.where(lo, zero, s) for s in slabs],
                                axis=0)
        s = _dot(q_all, kt_band) + bias_ref[...]
        key0 = (ib * blocks + g - 1) * BLOCK + kidx
        s_prev = jnp.where(key0 >= n_pad, s[:, 0:BLOCK], NEG_INF)
        s_cur = jnp.where(key0 + BLOCK >= n_pad, s[:, BLOCK:], NEG_INF)
        m = jnp.maximum(jnp.max(jnp.maximum(s_prev, s_cur), axis=-1, keepdims=True), sink)
        p = jnp.concatenate([jnp.exp2(s_prev - m), jnp.exp2(s_cur - m)], axis=1).astype(BF16)
        pv = _dot(jnp.concatenate([p[0:rows], p[rows:]], axis=1), _pv_operand(v_band, lo_k))
        esink = jnp.exp2(sink - m)
        den = pv[:, LANES:] + jnp.where(lo_o, esink[0:rows], esink[rows:])
        o = (pv[:, 0:LANES] / den).astype(BF16)
        for j in range(half):
            o_ref[0, cur, j * LANES:(j + 1) * LANES] = o[j * BLOCK:(j + 1) * BLOCK]


def _swa(qv3, kt, bias, sinks, n_pad):
    b, tp, _ = qv3.shape
    nb = tp // BLOCK
    blocks = next(n for n in (33, 11, 3, 1) if nb % n == 0)
    rows = blocks * BLOCK
    steps = nb // blocks
    vblk = SWA_QCOLS // LANES
    krow = FOX_WIDTH // LANES
    return pl.pallas_call(
        functools.partial(_swa_kernel, blocks=blocks, n_pad=n_pad),
        grid=(b, steps),
        in_specs=[
            pl.BlockSpec((1, rows, SWA_QCOLS), lambda i, j: (i, j, 0)),
            pl.BlockSpec((1, rows, LANES), lambda i, j: (i, j, vblk)),
            pl.BlockSpec((1, BLOCK, LANES), lambda i, j: (i, jnp.maximum(j * blocks - 1, 0), vblk)),
            pl.BlockSpec((LANES, rows), lambda i, j: (krow, i * steps + j)),
            pl.BlockSpec((LANES, BLOCK), lambda i, j: (krow, jnp.maximum((i * steps + j) * blocks - 1, 0))),
            _resident((SWA_Q_HEADS * BLOCK, 2 * BLOCK)),
            _resident((SWA_Q_HEADS * BLOCK, LANES)),
        ],
        out_specs=pl.BlockSpec((1, rows, SWA_QCOLS), lambda i, j: (i, j, 0)),
        out_shape=jax.ShapeDtypeStruct((b, tp, SWA_QCOLS), BF16),
        compiler_params=pltpu.CompilerParams(dimension_semantics=("parallel", "parallel"), vmem_limit_bytes=VMEM_LIMIT),
        name="swa",
    )(qv3, qv3, qv3, kt, kt, bias, sinks)


FOX_TQ = 384
FOX_TK = 512


def _fox_kernel(q_ref, v_ref, kt_ref, ck_ref, o_ref, qa_s, m_s, acc_s, *, tq, tk, n_tiles, tail, n_pad):
    i = pl.program_id(1)
    pairs = FOX_HEADS // 2
    lane = lax.broadcasted_iota(jnp.int32, (tq, LANES), 1)
    lo = lane < HEAD_DIM
    for hp in range(pairs):
        q = q_ref[0, :, hp * LANES:(hp + 1) * LANES]
        zero = jnp.zeros_like(q)
        qa_s[hp, 0:tq, :] = jnp.where(lo, q, zero)
        qa_s[hp, tq:, :] = jnp.where(lo, zero, q)
    m_s[...] = jnp.full(m_s.shape, NEG_INF, F32)
    acc_s[...] = jnp.zeros(acc_s.shape, F32)

    def tile(start, size, masked):
        key_ok = start + lax.broadcasted_iota(jnp.int32, (1, size), 1) >= n_pad
        lo_k = lax.broadcasted_iota(jnp.int32, (size, LANES), 1) < HEAD_DIM
        if masked:
            rel = lax.broadcasted_iota(jnp.int32, (tq, size), 1) - lax.broadcasted_iota(jnp.int32, (tq, size), 0)
            ok = rel <= i * tq - start
        for hp in range(pairs):
            kt = kt_ref[hp * LANES:(hp + 1) * LANES, pl.ds(start, size)]
            s2 = _dot(qa_s[hp], kt)
            ps, alphas = [], []
            for e in range(2):
                c = jnp.where(key_ok, ck_ref[0, 2 * hp + e:2 * hp + e + 1, pl.ds(start, size)] * (-LOG2E), NEG_INF)
                s = s2[e * tq:(e + 1) * tq] + c
                if masked:
                    s = jnp.where(ok, s, NEG_INF)
                chunks = [s[:, c0:c0 + LANES] for c0 in range(0, size, LANES)]
                m_prev = m_s[2 * hp + e]
                m_new = jnp.maximum(m_prev, jnp.max(functools.reduce(jnp.maximum, chunks), axis=-1, keepdims=True))
                alphas.append(jnp.exp2(m_prev - m_new))
                ps.append(jnp.concatenate([jnp.exp2(ch - m_new) for ch in chunks], axis=1).astype(BF16))
                m_s[2 * hp + e] = m_new
            alpha = jnp.where(lo, alphas[0], alphas[1])
            alpha = jnp.concatenate([alpha, alpha], axis=1)
            v = v_ref[0, pl.ds(start, size), hp * LANES:(hp + 1) * LANES]
            acc_s[hp] = alpha * acc_s[hp] + _dot(jnp.concatenate(ps, axis=1), _pv_operand(v, lo_k))

    n_last = jnp.minimum((tq * (i + 1) - 1) // tk, n_tiles - 1)
    n_free = (tq * i + 1) // tk

    def full_tile(j, masked):
        tile(pl.multiple_of(j * tk, tk), tk, masked)

    def two_free(j2, carry):
        full_tile(2 * j2, False)
        full_tile(2 * j2 + 1, False)
        return carry

    def one_masked(j, carry):
        full_tile(j, True)
        return carry

    lax.fori_loop(0, n_free // 2, two_free, 0)
    pl.when(n_free % 2 == 1)(lambda: full_tile(n_free - 1, False))
    lax.fori_loop(n_free, n_last, one_masked, 0)
    last_start = pl.multiple_of(n_last * tk, tk)
    reach = jnp.minimum(tq * (i + 1) - n_last * tk, tk)
    for size in range(LANES, tk + 1, LANES):
        pl.when(reach == size)(functools.partial(tile, last_start, size, True))
    if tail:
        pl.when(tq * (i + 1) > n_tiles * tk)(lambda: tile(n_tiles * tk, tail, True))

    for hp in range(pairs):
        acc = acc_s[hp]
        o_ref[0, :, hp * LANES:(hp + 1) * LANES] = (acc[:, 0:LANES] / acc[:, LANES:]).astype(BF16)


def _fox(qv3, kt, ck, n_pad):
    b, tp, _ = qv3.shape
    tq, tk = FOX_TQ, FOX_TK
    n_tiles, tail = tp // tk, tp % tk
    assert tp % tq == 0 and tail % LANES == 0 and tail <= tq and n_tiles >= 1
    pairs = FOX_HEADS // 2
    return pl.pallas_call(
        functools.partial(_fox_kernel, tq=tq, tk=tk, n_tiles=n_tiles, tail=tail, n_pad=n_pad),
        grid=(b, tp // tq),
        in_specs=[
            pl.BlockSpec((1, tq, FOX_WIDTH), lambda i, j: (i, j, 0)),
            pl.BlockSpec((1, tp, FOX_WIDTH), lambda i, j: (i, 0, 1)),
            pl.BlockSpec((FOX_WIDTH, tp), lambda i, j: (0, i)),
            pl.BlockSpec((1, FOX_HEADS, tp), lambda i, j: (i, 0, 0)),
        ],
        out_specs=pl.BlockSpec((1, tq, FOX_WIDTH), lambda i, j: (i, j, 0)),
        out_shape=jax.ShapeDtypeStruct((b, tp, FOX_WIDTH), BF16),
        scratch_shapes=[
            pltpu.VMEM((pairs, 2 * tq, LANES), BF16),
            pltpu.VMEM((FOX_HEADS, tq, LANES), F32),
            pltpu.VMEM((pairs, tq, 2 * LANES), F32),
        ],
        compiler_params=pltpu.CompilerParams(
            dimension_semantics=("parallel", "arbitrary"), vmem_limit_bytes=VMEM_LIMIT),
        name="fox",
    )(qv3, qv3, kt, ck)


SUBLANES = 8


LRU_SEQS = 2


def _lru_kernel(xy_ref, perm_ref, unperm_ref, cw_ref, cb_ref, wg_ref, bg_ref, lam_ref, o_ref,
                xe_s, a_s, h_s, hist_s, carry, *, seqs, tt, n_pad):
    @pl.when(pl.program_id(1) == 0)
    def _():
        hist_s[...] = jnp.zeros(hist_s.shape, F32)
        carry[...] = jnp.zeros(carry.shape, F32)

    for q in range(seqs):
        _lru_one(xy_ref.at[q], perm_ref, unperm_ref, cw_ref, cb_ref, wg_ref, bg_ref, lam_ref, o_ref.at[q],
                 xe_s.at[q], a_s.at[q], h_s.at[q], hist_s.at[q], carry.at[q], tt=tt, n_pad=n_pad)


def _lru_one(xy_ref, perm_ref, unperm_ref, cw_ref, cb_ref, wg_ref, bg_ref, lam_ref, o_ref,
             xe_s, a_s, h_s, hist_s, carry, *, tt, n_pad):
    it = pl.program_id(1)
    seg = tt // SUBLANES
    taps = CONV_WIDTH - 1
    w = LRU_WIDTH
    sub = lax.broadcasted_iota(jnp.int32, (SUBLANES, w), 0)
    xyp = _dot(perm_ref[...], xy_ref[...])
    xp, gate = xyp[:, 0:w], xyp[:, w:]

    for v in range(taps):
        src = slice((seg - taps + v) * SUBLANES, (seg - taps + v + 1) * SUBLANES)
        xe_s[v * SUBLANES:(v + 1) * SUBLANES, :] = jnp.where(
            sub == 0, pltpu.roll(hist_s[v], 1, 0), pltpu.roll(xp[src], 1, 0))
        hist_s[v] = xp[src]
    xe_s[taps * SUBLANES:, :] = xp
    conv = xe_s[0:tt, :] * cw_ref[0:1, :]
    for i in range(1, CONV_WIDTH):
        conv = conv + xe_s[i * SUBLANES:i * SUBLANES + tt, :] * cw_ref[i:i + 1, :]
    conv = conv + cb_ref[...]

    t = jnp.tanh(_dot(conv.astype(BF16), wg_ref[...]) + bg_ref[...])
    k = (0.5 * LRU_C) * _log_sigmoid(lam_ref[...])
    log_a = k * t[:, 0:w] + k
    p_idx = lax.broadcasted_iota(jnp.int32, (tt, 1), 0)
    time = it * tt + (p_idx & (SUBLANES - 1)) * seg + (p_idx >> 3)
    a = jnp.exp(log_a)
    quarter_one_minus_a2 = (-0.25 * jnp.tanh(log_a)) * (a * a + 1.0)
    b = jnp.where(time >= n_pad, jnp.sqrt(quarter_one_minus_a2) * (t[:, w:] * conv + conv), 0.0)

    h_loc = jnp.zeros((SUBLANES, w), F32)
    prod = jnp.ones((SUBLANES, w), F32)
    for v in range(seg):
        rows = slice(v * SUBLANES, (v + 1) * SUBLANES)
        h_loc = a[rows] * h_loc + b[rows]
        prod = a[rows] * prod
        h_s[rows, :] = h_loc
        a_s[rows, :] = prod
    aa, bb = prod, h_loc
    for s in (1, 2, 4):
        keep = sub >= s
        bb = jnp.where(keep, aa * pltpu.roll(bb, s, 0) + bb, bb)
        aa = jnp.where(keep, aa * pltpu.roll(aa, s, 0), aa)
    h_end = aa * carry[...] + bb
    h_in = jnp.where(sub == 0, carry[...], pltpu.roll(h_end, 1, 0))
    carry[...] = h_end[SUBLANES - 1:SUBLANES, :]

    h_in_all = jnp.concatenate([h_in] * seg, axis=0)
    out = ((h_s[...] + a_s[...] * h_in_all) * gate).astype(BF16)
    o_ref[...] = _dot(unperm_ref[...], out).astype(BF16)


def _lru(xy3, cw, cb, wg, bg, lam, n_pad):
    b, tp, _ = xy3.shape
    tt = _row_tile(tp)
    w = LRU_WIDTH
    seg = tt // SUBLANES
    p_idx = jnp.arange(tt)
    perm = (jnp.arange(tt)[None, :] == ((p_idx % SUBLANES) * seg + p_idx // SUBLANES)[:, None]).astype(BF16)
    seqs = LRU_SEQS if b % LRU_SEQS == 0 else 1
    return pl.pallas_call(
        functools.partial(_lru_kernel, seqs=seqs, tt=tt, n_pad=n_pad),
        grid=(b // seqs, tp // tt),
        in_specs=[
            pl.BlockSpec((seqs, tt, C_COLS), lambda i, j: (i, j, 0)),
            _resident((tt, tt)), _resident((tt, tt)),
            _resident((CONV_WIDTH, w)), _resident((1, w)), _resident((w, 2 * w)), _resident((1, 2 * w)),
            _resident((1, w)),
        ],
        out_specs=pl.BlockSpec((seqs, tt, w), lambda i, j: (i, j, 0)),
        out_shape=jax.ShapeDtypeStruct((b, tp, w), BF16),
        scratch_shapes=[
            pltpu.VMEM((seqs, tt + (CONV_WIDTH - 1) * SUBLANES, w), F32), pltpu.VMEM((seqs, tt, w), F32),
            pltpu.VMEM((seqs, tt, w), F32), pltpu.VMEM((seqs, CONV_WIDTH - 1, SUBLANES, w), F32),
            pltpu.VMEM((seqs, 1, w), F32),
        ],
        compiler_params=pltpu.CompilerParams(dimension_semantics=("parallel", "arbitrary"), vmem_limit_bytes=VMEM_LIMIT),
        name="lru",
    )(xy3, perm, perm.T, cw, cb, wg, bg, lam)


FF_CHUNK = 256


def _mix_ffn_rows(h, oa, of, oc, gt_ref, wb_ref, wo_ref, gn_ref, w1_ref, w2_ref, act_s):
    d = D_MODEL
    merged = gt_ref[:, 0:d].astype(F32) * _dot(oa, wb_ref[0])
    merged = merged + gt_ref[:, d:2 * d].astype(F32) * _dot(of, wb_ref[1])
    merged = merged + gt_ref[:, 2 * d:3 * d].astype(F32) * _dot(oc, wb_ref[2])
    hm = h + _dot(merged.astype(BF16), wo_ref[...])
    u = _rms_norm(hm, gn_ref[...]).astype(BF16)
    for c in range(0, D_FF, FF_CHUNK):
        gate = _dot(u, w1_ref[:, c:c + FF_CHUNK])
        up = _dot(u, w1_ref[:, D_FF + c:D_FF + c + FF_CHUNK])
        act_s[:, c:c + FF_CHUNK] = (gate * _sigmoid(gate) * up).astype(BF16)
    return hm + _dot(act_s[...], w2_ref[...])


def _mix_ffn_kernel(h_ref, oa_ref, of_ref, oc_ref, gt_ref, wb_ref, wo_ref, gn_ref, w1_ref, w2_ref, out_ref, act_s,
                    *, tm, n_pad):
    j = pl.program_id(1)
    y = _mix_ffn_rows(h_ref[0], oa_ref[0], of_ref[0], oc_ref[0], gt_ref.at[0], wb_ref, wo_ref, gn_ref, w1_ref, w2_ref,
                      act_s)
    row = j * tm + lax.broadcasted_iota(jnp.int32, (tm, 1), 0)
    out_ref[0] = jnp.where(row >= n_pad, y, 0.0)


def _mix_ffn_last_kernel(h_ref, oa_ref, of_ref, oc_ref, gt_ref, wb_ref, wo_ref, gn_ref, w1_ref, w2_ref, gf_ref,
                         out_ref, act_s):
    y = _mix_ffn_rows(h_ref[...], oa_ref[...], of_ref[...], oc_ref[...], gt_ref, wb_ref, wo_ref, gn_ref, w1_ref,
                      w2_ref, act_s)
    out_ref[0] = _rms_norm(y, gf_ref[...])


def _mix_ffn_last(h3, oa, of, oc, gt3, wb, wo, gn, w1, w2, gf, skip_rows, seq):
    b, tp, d = h3.shape
    tm = 512 if seq % 512 == 0 else BLOCK
    w = LRU_WIDTH

    def rows(cols):
        return pl.BlockSpec((pl.Element(tm), pl.Element(cols)),
                            lambda i, j: (pl.multiple_of(i * tp + skip_rows + j * tm, BLOCK), 0))

    flat = lambda a: a.reshape(b * tp, a.shape[-1])
    return pl.pallas_call(
        _mix_ffn_last_kernel,
        grid=(b, seq // tm),
        in_specs=[
            rows(d), rows(w), rows(w), rows(w), rows(G_COLS),
            _resident((N_BRANCH, w, d)), _resident((d, d)), _resident((1, d)),
            _resident((d, 2 * D_FF)), _resident((D_FF, d)), _resident((1, d)),
        ],
        out_specs=pl.BlockSpec((1, tm, d), lambda i, j: (i, j, 0)),
        out_shape=jax.ShapeDtypeStruct((b, seq, d), F32),
        scratch_shapes=[pltpu.VMEM((tm, D_FF), BF16)],
        compiler_params=pltpu.CompilerParams(dimension_semantics=("parallel", "parallel"), vmem_limit_bytes=VMEM_LIMIT),
        name="mix_ffn_last",
    )(flat(h3), flat(oa), flat(of), flat(oc), flat(gt3), wb, wo, gn, w1, w2, gf)


def _mix_ffn(h3, oa, of, oc, gt3, wb, wo, gn, w1, w2, n_pad):
    b, tp, d = h3.shape
    tm = _row_tile(tp)
    w = LRU_WIDTH
    tile = lambda cols: pl.BlockSpec((1, tm, cols), lambda i, j: (i, j, 0))
    return pl.pallas_call(
        functools.partial(_mix_ffn_kernel, tm=tm, n_pad=n_pad),
        grid=(b, tp // tm),
        in_specs=[
            tile(d), tile(w), tile(w), tile(w), tile(G_COLS),
            _resident((N_BRANCH, w, d)), _resident((d, d)), _resident((1, d)),
            _resident((d, 2 * D_FF)), _resident((D_FF, d)),
        ],
        out_specs=tile(d),
        out_shape=jax.ShapeDtypeStruct((b, tp, d), F32),
        scratch_shapes=[pltpu.VMEM((tm, D_FF), BF16)],
        compiler_params=pltpu.CompilerParams(dimension_semantics=("parallel", "parallel"), vmem_limit_bytes=VMEM_LIMIT),
        name="mix_ffn",
    )(h3, oa, of, oc, gt3, wb, wo, gn, w1, w2)


def _t5_bucket(dist):
    max_exact = REL_BUCKETS // 2
    d = jnp.maximum(dist, 0)
    scaled = jnp.log(jnp.maximum(d, 1).astype(F32) / max_exact) / math.log(REL_MAX_DIST / max_exact)
    large = jnp.minimum(max_exact + (scaled * (REL_BUCKETS - max_exact)).astype(jnp.int32), REL_BUCKETS - 1)
    return jnp.where(d < max_exact, d, large)


def _swa_bias_kernel(bucket_ref, table_ref, o_ref):
    bucket = bucket_ref[...]
    for h in range(SWA_Q_HEADS):
        acc = jnp.full(bucket.shape, NEG_INF, F32)
        for c in range(REL_BUCKETS):
            acc = jnp.where(bucket == c, table_ref[c, h] * LOG2E, acc)
        o_ref[h * BLOCK:(h + 1) * BLOCK, :] = acc


def _swa_bias(rel_table):
    q_idx = jnp.arange(BLOCK)[:, None]
    k_idx = jnp.arange(2 * BLOCK)[None, :]
    dist = q_idx + BLOCK - k_idx
    bucket = jnp.where((dist >= 0) & (dist < SWA_WINDOW), _t5_bucket(dist), -1).astype(jnp.int32)
    return pl.pallas_call(
        _swa_bias_kernel,
        in_specs=[pl.BlockSpec(memory_space=pltpu.VMEM), pl.BlockSpec(memory_space=pltpu.SMEM)],
        out_specs=pl.BlockSpec(memory_space=pltpu.VMEM),
        out_shape=jax.ShapeDtypeStruct((SWA_Q_HEADS * BLOCK, 2 * BLOCK), F32),
        name="swa_bias",
    )(bucket, rel_table.astype(F32))


def _slab_head_order():
    half = SWA_Q_HEADS // 2
    return [h for j in range(half) for h in (j, j + half)]


def _pack_w_in(w_in):
    sizes = (512, 128, 128, 512, 512, 512, 8, 512, 512, 3072)
    offs = [0]
    for s in sizes:
        offs.append(offs[-1] + s)
    qa, ka, va, qf, kf, vf, fl, xc, yc, gates = [w_in[:, :, offs[i]:offs[i + 1]] for i in range(len(sizes))]
    scale = HEAD_DIM ** -0.5 * LOG2E
    qa = jnp.concatenate([qa[:, :, h * HEAD_DIM:(h + 1) * HEAD_DIM] for h in _slab_head_order()], axis=-1) * scale
    fl = jnp.pad(fl, ((0, 0), (0, 0), (0, LANES - FOX_HEADS)))
    packed = jnp.concatenate([qa, va, fl, qf * scale, vf, xc, yc, gates], axis=-1)
    keys_t = jnp.swapaxes(jnp.concatenate([kf, ka], axis=-1), 1, 2)
    return packed.astype(BF16), keys_t.astype(BF16)


def _block_diag(w):
    depth, nb, n, _ = w.shape
    eye = jnp.eye(nb, dtype=w.dtype)
    return jnp.einsum('lhij,hk->lhikj', w, eye).reshape(depth, nb * n, nb * n)


def kernel(x, meta_tokens, rel_bias_table, norm_mix, w_in, swa_sinks, fox_forget_bias, conv_w, conv_b,
           lru_w_r, lru_b_r, lru_w_i, lru_b_i, lru_lambda, w_branch, w_out, norm_ffn, w_ffn_in, w_ffn_out,
           norm_final):
    b, seq, d = x.shape
    depth = w_in.shape[0]
    t = N_META + seq
    n_pad = (-t) % BLOCK
    tp = t + n_pad
    assert d == D_MODEL and (n_pad + N_META) % BLOCK == 0

    meta = jnp.broadcast_to(meta_tokens.astype(x.dtype)[None], (b, N_META, d))
    h = jnp.concatenate([jnp.zeros((b, n_pad, d), x.dtype), meta, x], axis=1)

    w_in_p, wkt = _pack_w_in(w_in)
    bias_a = _swa_bias(rel_bias_table)
    sinks = jnp.broadcast_to((swa_sinks.astype(F32) * LOG2E)[:, :, None, None], (depth, SWA_Q_HEADS, BLOCK, LANES))
    sinks = sinks.reshape(depth, SWA_Q_HEADS * BLOCK, LANES)
    fb = jnp.pad(fox_forget_bias.astype(F32), ((0, 0), (0, LANES - FOX_HEADS)))[:, None, :]
    wg = (0.5 * jnp.concatenate([_block_diag(lru_w_r), _block_diag(lru_w_i)], axis=-1)).astype(BF16)
    bg = 0.5 * jnp.concatenate([lru_b_r, lru_b_i], axis=-1).astype(F32)[:, None, :]
    wb_a = jnp.concatenate([w_branch[:, 0, h * HEAD_DIM:(h + 1) * HEAD_DIM] for h in _slab_head_order()], axis=1)
    wb = jnp.stack([wb_a, w_branch[:, 1], w_branch[:, 2]], axis=1).astype(BF16)
    wo = w_out.astype(BF16)
    w1 = w_ffn_in.astype(BF16)
    w2 = w_ffn_out.astype(BF16)

    for l in range(depth):
        qva, qvf, kt, fl, xy, gt = _in_proj(h.reshape(b * tp, d), norm_mix[l][None, :], w_in_p[l], wkt[l])
        ck = _fox_prefix(fl.reshape(b, tp, LANES), fb[l], n_pad)
        o_a = _swa(qva.reshape(b, tp, A_COLS), kt, bias_a, sinks[l], n_pad)
        o_f = _fox(qvf.reshape(b, tp, F_COLS), kt, ck, n_pad)
        o_c = _lru(xy.reshape(b, tp, C_COLS), conv_w[l], conv_b[l][None, :], wg[l], bg[l], lru_lambda[l][None, :],
                   n_pad)
        mix_args = (h, o_a, o_f, o_c, gt.reshape(b, tp, G_COLS), wb[l], wo[l], norm_ffn[l][None, :], w1[l], w2[l])
        if l + 1 < depth:
            h = _mix_ffn(*mix_args, n_pad)
    return _mix_ffn_last(*mix_args, norm_final[None, :], n_pad + N_META, seq)
```

```python
import functools
import math

import jax
import jax.numpy as jnp
from jax import lax
from jax.experimental import pallas as pl
from jax.experimental.pallas import tpu as pltpu

F32 = jnp.float32
BF16 = jnp.bfloat16

D_MODEL = 1024
HEAD_DIM = 64
N_META = 16
BLOCK = 128
LANES = 128
NEG_INF = -1e30
LOG2E = math.log2(math.e)
SWA_WINDOW = 128
SWA_Q_HEADS = 8
SWA_KV_HEADS = 2
FOX_HEADS = 8
LRU_WIDTH = D_MODEL // 2
LRU_BLOCKS = 8
CONV_WIDTH = 4
LRU_C = 8.0
REL_BUCKETS = 32
REL_MAX_DIST = 128
D_FF = 2816
N_BRANCH = 3
EPS = 1e-6

SWA_QCOLS = SWA_Q_HEADS * HEAD_DIM
A_COLS = SWA_QCOLS + SWA_KV_HEADS * HEAD_DIM
FOX_WIDTH = FOX_HEADS * HEAD_DIM
F_COLS = 2 * FOX_WIDTH
KT_ROWS = FOX_WIDTH + SWA_KV_HEADS * HEAD_DIM
C_COLS = 2 * LRU_WIDTH
G_COLS = N_BRANCH * D_MODEL
OFF_A = 0
OFF_FL = OFF_A + A_COLS
OFF_F = OFF_FL + LANES
OFF_C = OFF_F + F_COLS
OFF_G = OFF_C + C_COLS
IN_COLS_PACKED = OFF_G + G_COLS

VMEM_LIMIT = 56 * 1024 * 1024


def _sigmoid(x):
    return 0.5 * jnp.tanh(0.5 * x) + 0.5


def _log_sigmoid(x):
    return jnp.minimum(x, 0.0) - jnp.log1p(jnp.exp(-jnp.abs(x)))


def _gelu_tanh(x):
    c = math.sqrt(2.0 / math.pi)
    return 0.5 * x * (1.0 + jnp.tanh(c * (x + 0.044715 * (x * x * x))))


def _rms_norm(x, g):
    ms = jnp.mean(x * x, axis=-1, keepdims=True)
    return x * lax.rsqrt(ms + EPS) * g


def _dot(a, b):
    return jnp.dot(a, b, preferred_element_type=F32)


def _dot_nt(a, b):
    return lax.dot_general(a, b, (((1,), (1,)), ((), ())), preferred_element_type=F32)


def _resident(shape):
    nd = len(shape)
    return pl.BlockSpec(shape, lambda *_: (0,) * nd, pipeline_mode=pl.Buffered(1))


def _row_tile(tp):
    return 384 if tp % 384 == 0 else BLOCK


def _pv_operand(v, lo_k):
    zero = jnp.zeros_like(v)
    ind_lo = jnp.where(lo_k, 1.0, 0.0).astype(BF16)
    ind_hi = jnp.where(lo_k, 0.0, 1.0).astype(BF16)
    top = jnp.concatenate([jnp.where(lo_k, v, zero), ind_lo], axis=1)
    bot = jnp.concatenate([jnp.where(lo_k, zero, v), ind_hi], axis=1)
    return jnp.concatenate([top, bot], axis=0)


def _in_proj_kernel(x_ref, g_ref, w_ref, wkt_ref, qa_ref, qf_ref, kt_ref, fl_ref, xy_ref, gt_ref):
    rows = x_ref.shape[0]
    group = rows // 4 if rows % 64 == 0 else rows
    parts = []
    for r0 in range(0, rows, group):
        part = _rms_norm(x_ref[r0:r0 + group, :], g_ref[...]).astype(BF16)
        qa_ref[r0:r0 + group, 0:SWA_QCOLS] = _dot(part, w_ref[:, OFF_A:OFF_A + SWA_QCOLS]).astype(BF16)
        parts.append(part)
    u = jnp.concatenate(parts, axis=0)

    def proj(c0, width):
        return _dot(u, w_ref[:, c0:c0 + width])

    v_fl = proj(OFF_A + SWA_QCOLS, 2 * LANES)
    qa_ref[:, SWA_QCOLS:] = v_fl[:, 0:LANES].astype(BF16)
    fl_ref[...] = v_fl[:, LANES:]
    for c in range(0, F_COLS, 512):
        qf_ref[:, c:c + 512] = proj(OFF_F + c, 512).astype(BF16)
    kt_ref[...] = _dot_nt(wkt_ref[...], u).astype(BF16)
    xy_ref[:, 0:LRU_WIDTH] = proj(OFF_C, LRU_WIDTH).astype(BF16)
    xy_ref[:, LRU_WIDTH:] = _gelu_tanh(proj(OFF_C + LRU_WIDTH, LRU_WIDTH)).astype(BF16)
    for c in range(0, G_COLS, 512):
        gt_ref[:, c:c + 512] = _sigmoid(proj(OFF_G + c, 512)).astype(BF16)


def _in_proj(h2, g, w, wkt):
    rows = h2.shape[0]
    tm = next(t for t in (768, 384, BLOCK) if rows % t == 0)
    row_spec = lambda cols: pl.BlockSpec((tm, cols), lambda i: (i, 0))
    return pl.pallas_call(
        _in_proj_kernel,
        grid=(rows // tm,),
        in_specs=[row_spec(D_MODEL), _resident((1, D_MODEL)), _resident((D_MODEL, IN_COLS_PACKED)),
                  _resident((KT_ROWS, D_MODEL))],
        out_specs=[row_spec(A_COLS), row_spec(F_COLS), pl.BlockSpec((KT_ROWS, tm), lambda i: (0, i)),
                   row_spec(LANES), row_spec(C_COLS), row_spec(G_COLS)],
        out_shape=[
            jax.ShapeDtypeStruct((rows, A_COLS), BF16),
            jax.ShapeDtypeStruct((rows, F_COLS), BF16),
            jax.ShapeDtypeStruct((KT_ROWS, rows), BF16),
            jax.ShapeDtypeStruct((rows, LANES), F32),
            jax.ShapeDtypeStruct((rows, C_COLS), BF16),
            jax.ShapeDtypeStruct((rows, G_COLS), BF16),
        ],
        compiler_params=pltpu.CompilerParams(dimension_semantics=("parallel",), vmem_limit_bytes=VMEM_LIMIT),
        name="in_proj",
    )(h2, g, w, wkt)


def _fox_prefix_kernel(fl_ref, fb_ref, ck_ref, cum_ref, *, tp, n_pad):
    lf = _log_sigmoid(fl_ref[0] + fb_ref[...])
    row = lax.broadcasted_iota(jnp.int32, (tp, 1), 0)
    lf = jnp.where(row >= n_pad, lf, 0.0)
    r = lax.broadcasted_iota(jnp.int32, (BLOCK, BLOCK), 0)
    c = lax.broadcasted_iota(jnp.int32, (BLOCK, BLOCK), 1)
    tri = jnp.where(r >= c, 1.0, 0.0).astype(F32)
    carry = jnp.zeros((1, LANES), F32)
    for blk in range(tp // BLOCK):
        sl = slice(blk * BLOCK, (blk + 1) * BLOCK)
        cum = jnp.dot(tri, lf[sl], preferred_element_type=F32, precision=lax.Precision.HIGHEST) + carry
        cum_ref[sl, :] = cum
        carry = cum[BLOCK - 1:BLOCK, :]
    ck_ref[0] = cum_ref[...].T[0:FOX_HEADS, :]


def _fox_prefix(fl3, fb, n_pad):
    b, tp, _ = fl3.shape
    return pl.pallas_call(
        functools.partial(_fox_prefix_kernel, tp=tp, n_pad=n_pad),
        grid=(b,),
        in_specs=[pl.BlockSpec((1, tp, LANES), lambda i: (i, 0, 0)), _resident((1, LANES))],
        out_specs=pl.BlockSpec((1, FOX_HEADS, tp), lambda i: (i, 0, 0)),
        out_shape=jax.ShapeDtypeStruct((b, FOX_HEADS, tp), F32),
        scratch_shapes=[pltpu.VMEM((tp, LANES), F32)],
        compiler_params=pltpu.CompilerParams(dimension_semantics=("parallel",), vmem_limit_bytes=VMEM_LIMIT),
        name="fox_prefix",
    )(fl3, fb)


def _swa_kernel(q_ref, v_ref, vp_ref, kt_ref, ktp_ref, bias_ref, sink_ref, o_ref, *, blocks, n_pad):
    ib = pl.program_id(1)
    half = SWA_Q_HEADS // 2
    rows = half * BLOCK
    lo = lax.broadcasted_iota(jnp.int32, (BLOCK, LANES), 1) < HEAD_DIM
    lo_k = lax.broadcasted_iota(jnp.int32, (2 * BLOCK, LANES), 1) < HEAD_DIM
    lo_o = lax.broadcasted_iota(jnp.int32, (rows, LANES), 1) < HEAD_DIM
    kidx = lax.broadcasted_iota(jnp.int32, (1, BLOCK), 1)
    sink = sink_ref[...]
    for g in range(blocks):
        cur = slice(g * BLOCK, (g + 1) * BLOCK)
        if g == 0:
            kt_prev, v_prev = ktp_ref[...], vp_ref[0]
        else:
            prev = slice((g - 1) * BLOCK, g * BLOCK)
            kt_prev, v_prev = kt_ref[:, prev], v_ref[0, prev, :]
        kt_band = jnp.concatenate([kt_prev, kt_ref[:, cur]], axis=1)
        v_band = jnp.concatenate([v_prev, v_ref[0, cur, :]], axis=0)
        slabs = [q_ref[0, cur, j * LANES:(j + 1) * LANES] for j in range(half)]
        zero = jnp.zeros_like(slabs[0])
        q_all = jnp.concatenate([jnp.where(lo, s, zero) for s in slabs] + [jnp.where(lo, zero, s) for s in slabs],
                                axis=0)
        s = _dot(q_all, kt_band) + bias_ref[...]
        key0 = (ib * blocks + g - 1) * BLOCK + kidx
        s_prev = jnp.where(key0 >= n_pad, s[:, 0:BLOCK], NEG_INF)
        s_cur = jnp.where(key0 + BLOCK >= n_pad, s[:, BLOCK:], NEG_INF)
        m = jnp.maximum(jnp.max(jnp.maximum(s_prev, s_cur), axis=-1, keepdims=True), sink)
        p = jnp.concatenate([jnp.exp2(s_prev - m), jnp.exp2(s_cur - m)], axis=1).astype(BF16)
        pv = _dot(jnp.concatenate([p[0:rows], p[rows:]], axis=1), _pv_operand(v_band, lo_k))
        esink = jnp.exp2(sink - m)
        den = pv[:, LANES:] + jnp.where(lo_o, esink[0:rows], esink[rows:])
        o = (pv[:, 0:LANES] / den).astype(BF16)
        for j in range(half):
            o_ref[0, cur, j * LANES:(j + 1) * LANES] = o[j * BLOCK:(j + 1) * BLOCK]


def _swa(qv3, kt, bias, sinks, n_pad):
    b, tp, _ = qv3.shape
    nb = tp // BLOCK
    blocks = next(n for n in (33, 11, 3, 1) if nb % n == 0)
    rows = blocks * BLOCK
    steps = nb // blocks
    vblk = SWA_QCOLS // LANES
    krow = FOX_WIDTH // LANES
    return pl.pallas_call(
        functools.partial(_swa_kernel, blocks=blocks, n_pad=n_pad),
        grid=(b, steps),
        in_specs=[
            pl.BlockSpec((1, rows, SWA_QCOLS), lambda i, j: (i, j, 0)),
            pl.BlockSpec((1, rows, LANES), lambda i, j: (i, j, vblk)),
            pl.BlockSpec((1, BLOCK, LANES), lambda i, j: (i, jnp.maximum(j * blocks - 1, 0), vblk)),
            pl.BlockSpec((LANES, rows), lambda i, j: (krow, i * steps + j)),
            pl.BlockSpec((LANES, BLOCK), lambda i, j: (krow, jnp.maximum((i * steps + j) * blocks - 1, 0))),
            _resident((SWA_Q_HEADS * BLOCK, 2 * BLOCK)),
            _resident((SWA_Q_HEADS * BLOCK, LANES)),
        ],
        out_specs=pl.BlockSpec((1, rows, SWA_QCOLS), lambda i, j: (i, j, 0)),
        out_shape=jax.ShapeDtypeStruct((b, tp, SWA_QCOLS), BF16),
        compiler_params=pltpu.CompilerParams(dimension_semantics=("parallel", "parallel"), vmem_limit_bytes=VMEM_LIMIT),
        name="swa",
    )(qv3, qv3, qv3, kt, kt, bias, sinks)


FOX_TQ = 384
FOX_TK = 512


def _fox_kernel(q_ref, v_ref, kt_ref, ck_ref, o_ref, qa_s, m_s, acc_s, *, tq, tk, n_tiles, tail, n_pad):
    i = pl.program_id(1)
    pairs = FOX_HEADS // 2
    lane = lax.broadcasted_iota(jnp.int32, (tq, LANES), 1)
    lo = lane < HEAD_DIM
    for hp in range(pairs):
        q = q_ref[0, :, hp * LANES:(hp + 1) * LANES]
        zero = jnp.zeros_like(q)
        qa_s[hp, 0:tq, :] = jnp.where(lo, q, zero)
        qa_s[hp, tq:, :] = jnp.where(lo, zero, q)
    m_s[...] = jnp.full(m_s.shape, NEG_INF, F32)
    acc_s[...] = jnp.zeros(acc_s.shape, F32)

    def tile(start, size, masked):
        key_ok = start + lax.broadcasted_iota(jnp.int32, (1, size), 1) >= n_pad
        lo_k = lax.broadcasted_iota(jnp.int32, (size, LANES), 1) < HEAD_DIM
        if masked:
            rel = lax.broadcasted_iota(jnp.int32, (tq, size), 1) - lax.broadcasted_iota(jnp.int32, (tq, size), 0)
            ok = rel <= i * tq - start
        for hp in range(pairs):
            kt = kt_ref[hp * LANES:(hp + 1) * LANES, pl.ds(start, size)]
            s2 = _dot(qa_s[hp], kt)
            ps, alphas = [], []
            for e in range(2):
                c = jnp.where(key_ok, ck_ref[0, 2 * hp + e:2 * hp + e + 1, pl.ds(start, size)] * (-LOG2E), NEG_INF)
                s = s2[e * tq:(e + 1) * tq] + c
                if masked:
                    s = jnp.where(ok, s, NEG_INF)
                chunks = [s[:, c0:c0 + LANES] for c0 in range(0, size, LANES)]
                m_prev = m_s[2 * hp + e]
                m_new = jnp.maximum(m_prev, jnp.max(functools.reduce(jnp.maximum, chunks), axis=-1, keepdims=True))
                alphas.append(jnp.exp2(m_prev - m_new))
                ps.append(jnp.concatenate([jnp.exp2(ch - m_new) for ch in chunks], axis=1).astype(BF16))
                m_s[2 * hp + e] = m_new
            alpha = jnp.where(lo, alphas[0], alphas[1])
            alpha = jnp.concatenate([alpha, alpha], axis=1)
            v = v_ref[0, pl.ds(start, size), hp * LANES:(hp + 1) * LANES]
            acc_s[hp] = alpha * acc_s[hp] + _dot(jnp.concatenate(ps, axis=1), _pv_operand(v, lo_k))

    n_last = jnp.minimum((tq * (i + 1) - 1) // tk, n_tiles - 1)
    n_free = (tq * i + 1) // tk

    def full_tile(j, masked):
        tile(pl.multiple_of(j * tk, tk), tk, masked)

    def two_free(j2, carry):
        full_tile(2 * j2, False)
        full_tile(2 * j2 + 1, False)
        return carry

    def one_masked(j, carry):
        full_tile(j, True)
        return carry

    lax.fori_loop(0, n_free // 2, two_free, 0)
    pl.when(n_free % 2 == 1)(lambda: full_tile(n_free - 1, False))
    lax.fori_loop(n_free, n_last, one_masked, 0)
    last_start = pl.multiple_of(n_last * tk, tk)
    reach = jnp.minimum(tq * (i + 1) - n_last * tk, tk)
    for size in range(LANES, tk + 1, LANES):
        pl.when(reach == size)(functools.partial(tile, last_start, size, True))
    if tail:
        pl.when(tq * (i + 1) > n_tiles * tk)(lambda: tile(n_tiles * tk, tail, True))

    for hp in range(pairs):
        acc = acc_s[hp]
        o_ref[0, :, hp * LANES:(hp + 1) * LANES] = (acc[:, 0:LANES] / acc[:, LANES:]).astype(BF16)


def _fox(qv3, kt, ck, n_pad):
    b, tp, _ = qv3.shape
    tq, tk = FOX_TQ, FOX_TK
    n_tiles, tail = tp // tk, tp % tk
    assert tp % tq == 0 and tail % LANES == 0 and tail <= tq and n_tiles >= 1
    pairs = FOX_HEADS // 2
    return pl.pallas_call(
        functools.partial(_fox_kernel, tq=tq, tk=tk, n_tiles=n_tiles, tail=tail, n_pad=n_pad),
        grid=(b, tp // tq),
        in_specs=[
            pl.BlockSpec((1, tq, FOX_WIDTH), lambda i, j: (i, j, 0)),
            pl.BlockSpec((1, tp, FOX_WIDTH), lambda i, j: (i, 0, 1)),
            pl.BlockSpec((FOX_WIDTH, tp), lambda i, j: (0, i)),
            pl.BlockSpec((1, FOX_HEADS, tp), lambda i, j: (i, 0, 0)),
        ],
        out_specs=pl.BlockSpec((1, tq, FOX_WIDTH), lambda i, j: (i, j, 0)),
        out_shape=jax.ShapeDtypeStruct((b, tp, FOX_WIDTH), BF16),
        scratch_shapes=[
            pltpu.VMEM((pairs, 2 * tq, LANES), BF16),
            pltpu.VMEM((FOX_HEADS, tq, LANES), F32),
            pltpu.VMEM((pairs, tq, 2 * LANES), F32),
        ],
        compiler_params=pltpu.CompilerParams(
            dimension_semantics=("parallel", "arbitrary"), vmem_limit_bytes=VMEM_LIMIT),
        name="fox",
    )(qv3, qv3, kt, ck)


SUBLANES = 8


LRU_SEQS = 2


def _lru_kernel(xy_ref, perm_ref, unperm_ref, cw_ref, cb_ref, wg_ref, bg_ref, lam_ref, o_ref,
                xe_s, a_s, h_s, hist_s, carry, *, seqs, tt, n_pad):
    @pl.when(pl.program_id(1) == 0)
    def _():
        hist_s[...] = jnp.zeros(hist_s.shape, F32)
        carry[...] = jnp.zeros(carry.shape, F32)

    for q in range(seqs):
        _lru_one(xy_ref.at[q], perm_ref, unperm_ref, cw_ref, cb_ref, wg_ref, bg_ref, lam_ref, o_ref.at[q],
                 xe_s.at[q], a_s.at[q], h_s.at[q], hist_s.at[q], carry.at[q], tt=tt, n_pad=n_pad)


def _lru_one(xy_ref, perm_ref, unperm_ref, cw_ref, cb_ref, wg_ref, bg_ref, lam_ref, o_ref,
             xe_s, a_s, h_s, hist_s, carry, *, tt, n_pad):
    it = pl.program_id(1)
    seg = tt // SUBLANES
    taps = CONV_WIDTH - 1
    w = LRU_WIDTH
    sub = lax.broadcasted_iota(jnp.int32, (SUBLANES, w), 0)
    xyp = _dot(perm_ref[...], xy_ref[...])
    xp, gate = xyp[:, 0:w], xyp[:, w:]

    for v in range(taps):
        src = slice((seg - taps + v) * SUBLANES, (seg - taps + v + 1) * SUBLANES)
        xe_s[v * SUBLANES:(v + 1) * SUBLANES, :] = jnp.where(
            sub == 0, pltpu.roll(hist_s[v], 1, 0), pltpu.roll(xp[src], 1, 0))
        hist_s[v] = xp[src]
    xe_s[taps * SUBLANES:, :] = xp
    conv = xe_s[0:tt, :] * cw_ref[0:1, :]
    for i in range(1, CONV_WIDTH):
        conv = conv + xe_s[i * SUBLANES:i * SUBLANES + tt, :] * cw_ref[i:i + 1, :]
    conv = conv + cb_ref[...]

    t = jnp.tanh(_dot(conv.astype(BF16), wg_ref[...]) + bg_ref[...])
    k = (0.5 * LRU_C) * _log_sigmoid(lam_ref[...])
    log_a = k * t[:, 0:w] + k
    p_idx = lax.broadcasted_iota(jnp.int32, (tt, 1), 0)
    time = it * tt + (p_idx & (SUBLANES - 1)) * seg + (p_idx >> 3)
    a = jnp.exp(log_a)
    quarter_one_minus_a2 = (-0.25 * jnp.tanh(log_a)) * (a * a + 1.0)
    b = jnp.where(time >= n_pad, jnp.sqrt(quarter_one_minus_a2) * (t[:, w:] * conv + conv), 0.0)

    h_loc = jnp.zeros((SUBLANES, w), F32)
    prod = jnp.ones((SUBLANES, w), F32)
    for v in range(seg):
        rows = slice(v * SUBLANES, (v + 1) * SUBLANES)
        h_loc = a[rows] * h_loc + b[rows]
        prod = a[rows] * prod
        h_s[rows, :] = h_loc
        a_s[rows, :] = prod
    aa, bb = prod, h_loc
    for s in (1, 2, 4):
        keep = sub >= s
        bb = jnp.where(keep, aa * pltpu.roll(bb, s, 0) + bb, bb)
        aa = jnp.where(keep, aa * pltpu.roll(aa, s, 0), aa)
    h_end = aa * carry[...] + bb
    h_in = jnp.where(sub == 0, carry[...], pltpu.roll(h_end, 1, 0))
    carry[...] = h_end[SUBLANES - 1:SUBLANES, :]

    h_in_all = jnp.concatenate([h_in] * seg, axis=0)
    out = ((h_s[...] + a_s[...] * h_in_all) * gate).astype(BF16)
    o_ref[...] = _dot(unperm_ref[...], out).astype(BF16)


def _lru(xy3, cw, cb, wg, bg, lam, n_pad):
    b, tp, _ = xy3.shape
    tt = _row_tile(tp)
    w = LRU_WIDTH
    seg = tt // SUBLANES
    p_idx = jnp.arange(tt)
    perm = (jnp.arange(tt)[None, :] == ((p_idx % SUBLANES) * seg + p_idx // SUBLANES)[:, None]).astype(BF16)
    seqs = LRU_SEQS if b % LRU_SEQS == 0 else 1
    return pl.pallas_call(
        functools.partial(_lru_kernel, seqs=seqs, tt=tt, n_pad=n_pad),
        grid=(b // seqs, tp // tt),
        in_specs=[
            pl.BlockSpec((seqs, tt, C_COLS), lambda i, j: (i, j, 0)),
            _resident((tt, tt)), _resident((tt, tt)),
            _resident((CONV_WIDTH, w)), _resident((1, w)), _resident((w, 2 * w)), _resident((1, 2 * w)),
            _resident((1, w)),
        ],
        out_specs=pl.BlockSpec((seqs, tt, w), lambda i, j: (i, j, 0)),
        out_shape=jax.ShapeDtypeStruct((b, tp, w), BF16),
        scratch_shapes=[
            pltpu.VMEM((seqs, tt + (CONV_WIDTH - 1) * SUBLANES, w), F32), pltpu.VMEM((seqs, tt, w), F32),
            pltpu.VMEM((seqs, tt, w), F32), pltpu.VMEM((seqs, CONV_WIDTH - 1, SUBLANES, w), F32),
            pltpu.VMEM((seqs, 1, w), F32),
        ],
        compiler_params=pltpu.CompilerParams(dimension_semantics=("parallel", "arbitrary"), vmem_limit_bytes=VMEM_LIMIT),
        name="lru",
    )(xy3, perm, perm.T, cw, cb, wg, bg, lam)


FF_CHUNK = 256


def _mix_ffn_rows(h, oa, of, oc, gt_ref, wb_ref, wo_ref, gn_ref, w1_ref, w2_ref, act_s):
    d = D_MODEL
    merged = gt_ref[:, 0:d].astype(F32) * _dot(oa, wb_ref[0])
    merged = merged + gt_ref[:, d:2 * d].astype(F32) * _dot(of, wb_ref[1])
    merged = merged + gt_ref[:, 2 * d:3 * d].astype(F32) * _dot(oc, wb_ref[2])
    hm = h + _dot(merged.astype(BF16), wo_ref[...])
    u = _rms_norm(hm, gn_ref[...]).astype(BF16)
    for c in range(0, D_FF, FF_CHUNK):
        gate = _dot(u, w1_ref[:, c:c + FF_CHUNK])
        up = _dot(u, w1_ref[:, D_FF + c:D_FF + c + FF_CHUNK])
        act_s[:, c:c + FF_CHUNK] = (gate * _sigmoid(gate) * up).astype(BF16)
    return hm + _dot(act_s[...], w2_ref[...])


def _mix_ffn_kernel(h_ref, oa_ref, of_ref, oc_ref, gt_ref, wb_ref, wo_ref, gn_ref, w1_ref, w2_ref, out_ref, act_s,
                    *, tm, n_pad):
    j = pl.program_id(1)
    y = _mix_ffn_rows(h_ref[0], oa_ref[0], of_ref[0], oc_ref[0], gt_ref.at[0], wb_ref, wo_ref, gn_ref, w1_ref, w2_ref,
                      act_s)
    row = j * tm + lax.broadcasted_iota(jnp.int32, (tm, 1), 0)
    out_ref[0] = jnp.where(row >= n_pad, y, 0.0)


def _mix_ffn_last_kernel(h_ref, oa_ref, of_ref, oc_ref, gt_ref, wb_ref, wo_ref, gn_ref, w1_ref, w2_ref, gf_ref,
                         out_ref, act_s):
    y = _mix_ffn_rows(h_ref[...], oa_ref[...], of_ref[...], oc_ref[...], gt_ref, wb_ref, wo_ref, gn_ref, w1_ref,
                      w2_ref, act_s)
    out_ref[0] = _rms_norm(y, gf_ref[...])


def _mix_ffn_last(h3, oa, of, oc, gt3, wb, wo, gn, w1, w2, gf, skip_rows, seq):
    b, tp, d = h3.shape
    tm = 512 if seq % 512 == 0 else BLOCK
    w = LRU_WIDTH

    def rows(cols):
        return pl.BlockSpec((pl.Element(tm), pl.Element(cols)),
                            lambda i, j: (pl.multiple_of(i * tp + skip_rows + j * tm, BLOCK), 0))

    flat = lambda a: a.reshape(b * tp, a.shape[-1])
    return pl.pallas_call(
        _mix_ffn_last_kernel,
        grid=(b, seq // tm),
        in_specs=[
            rows(d), rows(w), rows(w), rows(w), rows(G_COLS),
            _resident((N_BRANCH, w, d)), _resident((d, d)), _resident((1, d)),
            _resident((d, 2 * D_FF)), _resident((D_FF, d)), _resident((1, d)),
        ],
        out_specs=pl.BlockSpec((1, tm, d), lambda i, j: (i, j, 0)),
        out_shape=jax.ShapeDtypeStruct((b, seq, d), F32),
        scratch_shapes=[pltpu.VMEM((tm, D_FF), BF16)],
        compiler_params=pltpu.CompilerParams(dimension_semantics=("parallel", "parallel"), vmem_limit_bytes=VMEM_LIMIT),
        name="mix_ffn_last",
    )(flat(h3), flat(oa), flat(of), flat(oc), flat(gt3), wb, wo, gn, w1, w2, gf)


def _mix_ffn(h3, oa, of, oc, gt3, wb, wo, gn, w1, w2, n_pad):
    b, tp, d = h3.shape
    tm = _row_tile(tp)
    w = LRU_WIDTH
    tile = lambda cols: pl.BlockSpec((1, tm, cols), lambda i, j: (i, j, 0))
    return pl.pallas_call(
        functools.partial(_mix_ffn_kernel, tm=tm, n_pad=n_pad),
        grid=(b, tp // tm),
        in_specs=[
            tile(d), tile(w), tile(w), tile(w), tile(G_COLS),
            _resident((N_BRANCH, w, d)), _resident((d, d)), _resident((1, d)),
            _resident((d, 2 * D_FF)), _resident((D_FF, d)),
        ],
        out_specs=tile(d),
        out_shape=jax.ShapeDtypeStruct((b, tp, d), F32),
        scratch_shapes=[pltpu.VMEM((tm, D_FF), BF16)],
        compiler_params=pltpu.CompilerParams(dimension_semantics=("parallel", "parallel"), vmem_limit_bytes=VMEM_LIMIT),
        name="mix_ffn",
    )(h3, oa, of, oc, gt3, wb, wo, gn, w1, w2)


def _t5_bucket(dist):
    max_exact = REL_BUCKETS // 2
    d = jnp.maximum(dist, 0)
    scaled = jnp.log(jnp.maximum(d, 1).astype(F32) / max_exact) / math.log(REL_MAX_DIST / max_exact)
    large = jnp.minimum(max_exact + (scaled * (REL_BUCKETS - max_exact)).astype(jnp.int32), REL_BUCKETS - 1)
    return jnp.where(d < max_exact, d, large)


def _swa_bias_kernel(bucket_ref, table_ref, o_ref):
    bucket = bucket_ref[...]
    for h in range(SWA_Q_HEADS):
        acc = jnp.full(bucket.shape, NEG_INF, F32)
        for c in range(REL_BUCKETS):
            acc = jnp.where(bucket == c, table_ref[c, h] * LOG2E, acc)
        o_ref[h * BLOCK:(h + 1) * BLOCK, :] = acc


def _swa_bias(rel_table):
    q_idx = jnp.arange(BLOCK)[:, None]
    k_idx = jnp.arange(2 * BLOCK)[None, :]
    dist = q_idx + BLOCK - k_idx
    bucket = jnp.where((dist >= 0) & (dist < SWA_WINDOW), _t5_bucket(dist), -1).astype(jnp.int32)
    return pl.pallas_call(
        _swa_bias_kernel,
        in_specs=[pl.BlockSpec(memory_space=pltpu.VMEM), pl.BlockSpec(memory_space=pltpu.SMEM)],
        out_specs=pl.BlockSpec(memory_space=pltpu.VMEM),
        out_shape=jax.ShapeDtypeStruct((SWA_Q_HEADS * BLOCK, 2 * BLOCK), F32),
        name="swa_bias",
    )(bucket, rel_table.astype(F32))


def _to_slab_order(a, axis):
    half = SWA_Q_HEADS // 2
    shape = a.shape
    a = a.reshape(shape[:axis] + (2, half, HEAD_DIM) + shape[axis + 1:])
    return jnp.swapaxes(a, axis, axis + 1).reshape(shape)


def _pack_w_in(w_in):
    sizes = (512, 128, 128, 512, 512, 512, 8, 512, 512, 3072)
    offs = [0]
    for s in sizes:
        offs.append(offs[-1] + s)
    qa, ka, va, qf, kf, vf, fl, xc, yc, gates = [w_in[:, :, offs[i]:offs[i + 1]] for i in range(len(sizes))]
    scale = HEAD_DIM ** -0.5 * LOG2E
    qa = _to_slab_order(qa, axis=2) * scale
    fl = jnp.pad(fl, ((0, 0), (0, 0), (0, LANES - FOX_HEADS)))
    packed = jnp.concatenate([qa, va, fl, qf * scale, vf, xc, yc, gates], axis=-1)
    keys_t = jnp.swapaxes(jnp.concatenate([kf, ka], axis=-1), 1, 2)
    return packed.astype(BF16), keys_t.astype(BF16)


def _block_diag(w):
    depth, nb, n, _ = w.shape
    eye = jnp.eye(nb, dtype=w.dtype)
    return jnp.einsum('lhij,hk->lhikj', w, eye).reshape(depth, nb * n, nb * n)


def kernel(x, meta_tokens, rel_bias_table, norm_mix, w_in, swa_sinks, fox_forget_bias, conv_w, conv_b,
           lru_w_r, lru_b_r, lru_w_i, lru_b_i, lru_lambda, w_branch, w_out, norm_ffn, w_ffn_in, w_ffn_out,
           norm_final):
    b, seq, d = x.shape
    depth = w_in.shape[0]
    t = N_META + seq
    n_pad = (-t) % BLOCK
    tp = t + n_pad
    assert d == D_MODEL and (n_pad + N_META) % BLOCK == 0

    meta = jnp.broadcast_to(meta_tokens.astype(x.dtype)[None], (b, N_META, d))
    h = jnp.concatenate([jnp.zeros((b, n_pad, d), x.dtype), meta, x], axis=1)

    w_in_p, wkt = _pack_w_in(w_in)
    bias_a = _swa_bias(rel_bias_table)
    sinks = jnp.broadcast_to((swa_sinks.astype(F32) * LOG2E)[:, :, None, None], (depth, SWA_Q_HEADS, BLOCK, LANES))
    sinks = sinks.reshape(depth, SWA_Q_HEADS * BLOCK, LANES)
    fb = jnp.pad(fox_forget_bias.astype(F32), ((0, 0), (0, LANES - FOX_HEADS)))[:, None, :]
    wg = (0.5 * jnp.concatenate([_block_diag(lru_w_r), _block_diag(lru_w_i)], axis=-1)).astype(BF16)
    bg = 0.5 * jnp.concatenate([lru_b_r, lru_b_i], axis=-1).astype(F32)[:, None, :]
    wb_a = _to_slab_order(w_branch[:, 0], axis=1)
    wb = jnp.stack([wb_a, w_branch[:, 1], w_branch[:, 2]], axis=1).astype(BF16)
    wo = w_out.astype(BF16)
    w1 = w_ffn_in.astype(BF16)
    w2 = w_ffn_out.astype(BF16)

    for l in range(depth):
        qva, qvf, kt, fl, xy, gt = _in_proj(h.reshape(b * tp, d), norm_mix[l][None, :], w_in_p[l], wkt[l])
        ck = _fox_prefix(fl.reshape(b, tp, LANES), fb[l], n_pad)
        o_a = _swa(qva.reshape(b, tp, A_COLS), kt, bias_a, sinks[l], n_pad)
        o_f = _fox(qvf.reshape(b, tp, F_COLS), kt, ck, n_pad)
        o_c = _lru(xy.reshape(b, tp, C_COLS), conv_w[l], conv_b[l][None, :], wg[l], bg[l], lru_lambda[l][None, :],
                   n_pad)
        mix_args = (h, o_a, o_f, o_c, gt.reshape(b, tp, G_COLS), wb[l], wo[l], norm_ffn[l][None, :], w1[l], w2[l])
        if l + 1 < depth:
            h = _mix_ffn(*mix_args, n_pad)
    return _mix_ffn_last(*mix_args, norm_final[None, :], n_pad + N_META, seq)
```

```python
import functools
import math

import jax
import jax.numpy as jnp
from jax import lax
from jax.experimental import pallas as pl
from jax.experimental.pallas import tpu as pltpu

F32 = jnp.float32
BF16 = jnp.bfloat16

D_MODEL = 1024
HEAD_DIM = 64
N_META = 16
BLOCK = 128
LANES = 128
NEG_INF = -1e30
LOG2E = math.log2(math.e)
SWA_WINDOW = 128
SWA_Q_HEADS = 8
SWA_KV_HEADS = 2
FOX_HEADS = 8
LRU_WIDTH = D_MODEL // 2
LRU_BLOCKS = 8
CONV_WIDTH = 4
LRU_C = 8.0
REL_BUCKETS = 32
REL_MAX_DIST = 128
D_FF = 2816
N_BRANCH = 3
EPS = 1e-6

SWA_QCOLS = SWA_Q_HEADS * HEAD_DIM
A_COLS = SWA_QCOLS + SWA_KV_HEADS * HEAD_DIM
FOX_WIDTH = FOX_HEADS * HEAD_DIM
F_COLS = 2 * FOX_WIDTH
KT_ROWS = FOX_WIDTH + SWA_KV_HEADS * HEAD_DIM
C_COLS = 2 * LRU_WIDTH
G_COLS = N_BRANCH * D_MODEL
OFF_A = 0
OFF_FL = OFF_A + A_COLS
OFF_F = OFF_FL + LANES
OFF_C = OFF_F + F_COLS
OFF_G = OFF_C + C_COLS
IN_COLS_PACKED = OFF_G + G_COLS

VMEM_LIMIT = 56 * 1024 * 1024


def _sigmoid(x):
    return 0.5 * jnp.tanh(0.5 * x) + 0.5


def _log_sigmoid(x):
    return jnp.minimum(x, 0.0) - jnp.log1p(jnp.exp(-jnp.abs(x)))


def _gelu_tanh(x):
    c = math.sqrt(2.0 / math.pi)
    return 0.5 * x * (1.0 + jnp.tanh(c * (x + 0.044715 * (x * x * x))))


def _rms_norm(x, g):
    ms = jnp.mean(x * x, axis=-1, keepdims=True)
    return x * lax.rsqrt(ms + EPS) * g


def _dot(a, b):
    return jnp.dot(a, b, preferred_element_type=F32)


def _dot_nt(a, b):
    return lax.dot_general(a, b, (((1,), (1,)), ((), ())), preferred_element_type=F32)


def _resident(shape):
    nd = len(shape)
    return pl.BlockSpec(shape, lambda *_: (0,) * nd, pipeline_mode=pl.Buffered(1))


def _row_tile(tp):
    return 384 if tp % 384 == 0 else BLOCK


def _pv_operand(v, lo_k):
    zero = jnp.zeros_like(v)
    ind_lo = jnp.where(lo_k, 1.0, 0.0).astype(BF16)
    ind_hi = jnp.where(lo_k, 0.0, 1.0).astype(BF16)
    top = jnp.concatenate([jnp.where(lo_k, v, zero), ind_lo], axis=1)
    bot = jnp.concatenate([jnp.where(lo_k, zero, v), ind_hi], axis=1)
    return jnp.concatenate([top, bot], axis=0)


def _in_proj_kernel(x_ref, g_ref, w_ref, wkt_ref, qa_ref, qf_ref, kt_ref, fl_ref, xy_ref, gt_ref):
    rows = x_ref.shape[0]
    group = rows // 4 if rows % 64 == 0 else rows
    parts = []
    for r0 in range(0, rows, group):
        part = _rms_norm(x_ref[r0:r0 + group, :], g_ref[...]).astype(BF16)
        qa_ref[r0:r0 + group, 0:SWA_QCOLS] = _dot(part, w_ref[:, OFF_A:OFF_A + SWA_QCOLS]).astype(BF16)
        parts.append(part)
    u = jnp.concatenate(parts, axis=0)

    def proj(c0, width):
        return _dot(u, w_ref[:, c0:c0 + width])

    v_fl = proj(OFF_A + SWA_QCOLS, 2 * LANES)
    qa_ref[:, SWA_QCOLS:] = v_fl[:, 0:LANES].astype(BF16)
    fl_ref[...] = v_fl[:, LANES:]
    for c in range(0, F_COLS, 512):
        qf_ref[:, c:c + 512] = proj(OFF_F + c, 512).astype(BF16)
    kt_ref[...] = _dot_nt(wkt_ref[...], u).astype(BF16)
    xy_ref[:, 0:LRU_WIDTH] = proj(OFF_C, LRU_WIDTH).astype(BF16)
    xy_ref[:, LRU_WIDTH:] = _gelu_tanh(proj(OFF_C + LRU_WIDTH, LRU_WIDTH)).astype(BF16)
    for c in range(0, G_COLS, 512):
        gt_ref[:, c:c + 512] = _sigmoid(proj(OFF_G + c, 512)).astype(BF16)


def _in_proj(h2, g, w, wkt):
    rows = h2.shape[0]
    tm = next(t for t in (768, 384, BLOCK) if rows % t == 0)
    row_spec = lambda cols: pl.BlockSpec((tm, cols), lambda i: (i, 0))
    return pl.pallas_call(
        _in_proj_kernel,
        grid=(rows // tm,),
        in_specs=[row_spec(D_MODEL), _resident((1, D_MODEL)), _resident((D_MODEL, IN_COLS_PACKED)),
                  _resident((KT_ROWS, D_MODEL))],
        out_specs=[row_spec(A_COLS), row_spec(F_COLS), pl.BlockSpec((KT_ROWS, tm), lambda i: (0, i)),
                   row_spec(LANES), row_spec(C_COLS), row_spec(G_COLS)],
        out_shape=[
            jax.ShapeDtypeStruct((rows, A_COLS), BF16),
            jax.ShapeDtypeStruct((rows, F_COLS), BF16),
            jax.ShapeDtypeStruct((KT_ROWS, rows), BF16),
            jax.ShapeDtypeStruct((rows, LANES), F32),
            jax.ShapeDtypeStruct((rows, C_COLS), BF16),
            jax.ShapeDtypeStruct((rows, G_COLS), BF16),
        ],
        compiler_params=pltpu.CompilerParams(dimension_semantics=("parallel",), vmem_limit_bytes=VMEM_LIMIT),
        name="in_proj",
    )(h2, g, w, wkt)


def _fox_prefix_kernel(fl_ref, fb_ref, ck_ref, cum_ref, *, tp, n_pad):
    lf = _log_sigmoid(fl_ref[0] + fb_ref[...])
    row = lax.broadcasted_iota(jnp.int32, (tp, 1), 0)
    lf = jnp.where(row >= n_pad, lf, 0.0)
    r = lax.broadcasted_iota(jnp.int32, (BLOCK, BLOCK), 0)
    c = lax.broadcasted_iota(jnp.int32, (BLOCK, BLOCK), 1)
    tri = jnp.where(r >= c, 1.0, 0.0).astype(F32)
    carry = jnp.zeros((1, LANES), F32)
    for blk in range(tp // BLOCK):
        sl = slice(blk * BLOCK, (blk + 1) * BLOCK)
        cum = jnp.dot(tri, lf[sl], preferred_element_type=F32, precision=lax.Precision.HIGHEST) + carry
        cum_ref[sl, :] = cum
        carry = cum[BLOCK - 1:BLOCK, :]
    ck_ref[0] = cum_ref[...].T[0:FOX_HEADS, :]


def _fox_prefix(fl3, fb, n_pad):
    b, tp, _ = fl3.shape
    return pl.pallas_call(
        functools.partial(_fox_prefix_kernel, tp=tp, n_pad=n_pad),
        grid=(b,),
        in_specs=[pl.BlockSpec((1, tp, LANES), lambda i: (i, 0, 0)), _resident((1, LANES))],
        out_specs=pl.BlockSpec((1, FOX_HEADS, tp), lambda i: (i, 0, 0)),
        out_shape=jax.ShapeDtypeStruct((b, FOX_HEADS, tp), F32),
        scratch_shapes=[pltpu.VMEM((tp, LANES), F32)],
        compiler_params=pltpu.CompilerParams(dimension_semantics=("parallel",), vmem_limit_bytes=VMEM_LIMIT),
        name="fox_prefix",
    )(fl3, fb)


def _swa_kernel(q_ref, v_ref, vp_ref, kt_ref, ktp_ref, bias_ref, sink_ref, o_ref, *, blocks, n_pad):
    ib = pl.program_id(1)
    half = SWA_Q_HEADS // 2
    rows = half * BLOCK
    lo = lax.broadcasted_iota(jnp.int32, (BLOCK, LANES), 1) < HEAD_DIM
    lo_k = lax.broadcasted_iota(jnp.int32, (2 * BLOCK, LANES), 1) < HEAD_DIM
    lo_o = lax.broadcasted_iota(jnp.int32, (rows, LANES), 1) < HEAD_DIM
    kidx = lax.broadcasted_iota(jnp.int32, (1, BLOCK), 1)
    sink = sink_ref[...]
    for g in range(blocks):
        cur = slice(g * BLOCK, (g + 1) * BLOCK)
        if g == 0:
            kt_prev, v_prev = ktp_ref[...], vp_ref[0]
        else:
            prev = slice((g - 1) * BLOCK, g * BLOCK)
            kt_prev, v_prev = kt_ref[:, prev], v_ref[0, prev, :]
        kt_band = jnp.concatenate([kt_prev, kt_ref[:, cur]], axis=1)
        v_band = jnp.concatenate([v_prev, v_ref[0, cur, :]], axis=0)
        slabs = [q_ref[0, cur, j * LANES:(j + 1) * LANES] for j in range(half)]
        zero = jnp.zeros_like(slabs[0])
        q_all = jnp.concatenate([jnp.where(lo, s, zero) for s in slabs] + [jnp.where(lo, zero, s) for s in slabs],
                                axis=0)
        s = _dot(q_all, kt_band) + bias_ref[...]
        key0 = (ib * blocks + g - 1) * BLOCK + kidx
        s_prev = jnp.where(key0 >= n_pad, s[:, 0:BLOCK], NEG_INF)
        s_cur = jnp.where(key0 + BLOCK >= n_pad, s[:, BLOCK:], NEG_INF)
        m = jnp.maximum(jnp.max(jnp.maximum(s_prev, s_cur), axis=-1, keepdims=True), sink)
        p = jnp.concatenate([jnp.exp2(s_prev - m), jnp.exp2(s_cur - m)], axis=1).astype(BF16)
        pv = _dot(jnp.concatenate([p[0:rows], p[rows:]], axis=1), _pv_operand(v_band, lo_k))
        esink = jnp.exp2(sink - m)
        den = pv[:, LANES:] + jnp.where(lo_o, esink[0:rows], esink[rows:])
        o = (pv[:, 0:LANES] / den).astype(BF16)
        for j in range(half):
            o_ref[0, cur, j * LANES:(j + 1) * LANES] = o[j * BLOCK:(j + 1) * BLOCK]


def _swa(qv3, kt, bias, sinks, n_pad):
    b, tp, _ = qv3.shape
    nb = tp // BLOCK
    blocks = next(n for n in (33, 11, 3, 1) if nb % n == 0)
    rows = blocks * BLOCK
    steps = nb // blocks
    vblk = SWA_QCOLS // LANES
    krow = FOX_WIDTH // LANES
    return pl.pallas_call(
        functools.partial(_swa_kernel, blocks=blocks, n_pad=n_pad),
        grid=(b, steps),
        in_specs=[
            pl.BlockSpec((1, rows, SWA_QCOLS), lambda i, j: (i, j, 0)),
            pl.BlockSpec((1, rows, LANES), lambda i, j: (i, j, vblk)),
            pl.BlockSpec((1, BLOCK, LANES), lambda i, j: (i, jnp.maximum(j * blocks - 1, 0), vblk)),
            pl.BlockSpec((LANES, rows), lambda i, j: (krow, i * steps + j)),
            pl.BlockSpec((LANES, BLOCK), lambda i, j: (krow, jnp.maximum((i * steps + j) * blocks - 1, 0))),
            _resident((SWA_Q_HEADS * BLOCK, 2 * BLOCK)),
            _resident((SWA_Q_HEADS * BLOCK, LANES)),
        ],
        out_specs=pl.BlockSpec((1, rows, SWA_QCOLS), lambda i, j: (i, j, 0)),
        out_shape=jax.ShapeDtypeStruct((b, tp, SWA_QCOLS), BF16),
        compiler_params=pltpu.CompilerParams(dimension_semantics=("parallel", "parallel"), vmem_limit_bytes=VMEM_LIMIT),
        name="swa",
    )(qv3, qv3, qv3, kt, kt, bias, sinks)


FOX_TQ = 384
FOX_TK = 512


def _fox_kernel(q_ref, v_ref, kt_ref, ck_ref, o_ref, qa_s, m_s, acc_s, *, tq, tk, n_tiles, tail, n_pad):
    i = pl.program_id(1)
    pairs = FOX_HEADS // 2
    lane = lax.broadcasted_iota(jnp.int32, (tq, LANES), 1)
    lo = lane < HEAD_DIM
    for hp in range(pairs):
        q = q_ref[0, :, hp * LANES:(hp + 1) * LANES]
        zero = jnp.zeros_like(q)
        qa_s[hp, 0:tq, :] = jnp.where(lo, q, zero)
        qa_s[hp, tq:, :] = jnp.where(lo, zero, q)
    m_s[...] = jnp.full(m_s.shape, NEG_INF, F32)
    acc_s[...] = jnp.zeros(acc_s.shape, F32)

    def tile(start, size, masked):
        key_ok = start + lax.broadcasted_iota(jnp.int32, (1, size), 1) >= n_pad
        lo_k = lax.broadcasted_iota(jnp.int32, (size, LANES), 1) < HEAD_DIM
        if masked:
            rel = lax.broadcasted_iota(jnp.int32, (tq, size), 1) - lax.broadcasted_iota(jnp.int32, (tq, size), 0)
            ok = rel <= i * tq - start
        for hp in range(pairs):
            kt = kt_ref[hp * LANES:(hp + 1) * LANES, pl.ds(start, size)]
            s2 = _dot(qa_s[hp], kt)
            ps, alphas = [], []
            for e in range(2):
                c = jnp.where(key_ok, ck_ref[0, 2 * hp + e:2 * hp + e + 1, pl.ds(start, size)] * (-LOG2E), NEG_INF)
                s = s2[e * tq:(e + 1) * tq] + c
                if masked:
                    s = jnp.where(ok, s, NEG_INF)
                chunks = [s[:, c0:c0 + LANES] for c0 in range(0, size, LANES)]
                m_prev = m_s[2 * hp + e]
                m_new = jnp.maximum(m_prev, jnp.max(functools.reduce(jnp.maximum, chunks), axis=-1, keepdims=True))
                alphas.append(jnp.exp2(m_prev - m_new))
                ps.append(jnp.concatenate([jnp.exp2(ch - m_new) for ch in chunks], axis=1).astype(BF16))
                m_s[2 * hp + e] = m_new
            alpha = jnp.where(lo, alphas[0], alphas[1])
            alpha = jnp.concatenate([alpha, alpha], axis=1)
            v = v_ref[0, pl.ds(start, size), hp * LANES:(hp + 1) * LANES]
            acc_s[hp] = alpha * acc_s[hp] + _dot(jnp.concatenate(ps, axis=1), _pv_operand(v, lo_k))

    n_last = jnp.minimum((tq * (i + 1) - 1) // tk, n_tiles - 1)
    n_free = (tq * i + 1) // tk

    def full_tile(j, masked):
        tile(pl.multiple_of(j * tk, tk), tk, masked)

    def four_free(j4, carry):
        for u in range(4):
            full_tile(4 * j4 + u, False)
        return carry

    def one_masked(j, carry):
        full_tile(j, True)
        return carry

    def two_free(j):
        full_tile(j, False)
        full_tile(j + 1, False)

    lax.fori_loop(0, n_free // 4, four_free, 0)
    pl.when(n_free % 4 >= 2)(lambda: two_free(n_free - n_free % 4))
    pl.when(n_free % 2 == 1)(lambda: full_tile(n_free - 1, False))
    lax.fori_loop(n_free, n_last, one_masked, 0)
    last_start = pl.multiple_of(n_last * tk, tk)
    reach = jnp.minimum(tq * (i + 1) - n_last * tk, tk)
    for size in range(LANES, tk + 1, LANES):
        pl.when(reach == size)(functools.partial(tile, last_start, size, True))
    if tail:
        pl.when(tq * (i + 1) > n_tiles * tk)(lambda: tile(n_tiles * tk, tail, True))

    for hp in range(pairs):
        acc = acc_s[hp]
        o_ref[0, :, hp * LANES:(hp + 1) * LANES] = (acc[:, 0:LANES] / acc[:, LANES:]).astype(BF16)


def _fox(qv3, kt, ck, n_pad):
    b, tp, _ = qv3.shape
    tq, tk = FOX_TQ, FOX_TK
    n_tiles, tail = tp // tk, tp % tk
    assert tp % tq == 0 and tail % LANES == 0 and tail <= tq and n_tiles >= 1
    pairs = FOX_HEADS // 2
    return pl.pallas_call(
        functools.partial(_fox_kernel, tq=tq, tk=tk, n_tiles=n_tiles, tail=tail, n_pad=n_pad),
        grid=(b, tp // tq),
        in_specs=[
            pl.BlockSpec((1, tq, FOX_WIDTH), lambda i, j: (i, j, 0)),
            pl.BlockSpec((1, tp, FOX_WIDTH), lambda i, j: (i, 0, 1)),
            pl.BlockSpec((FOX_WIDTH, tp), lambda i, j: (0, i)),
            pl.BlockSpec((1, FOX_HEADS, tp), lambda i, j: (i, 0, 0)),
        ],
        out_specs=pl.BlockSpec((1, tq, FOX_WIDTH), lambda i, j: (i, j, 0)),
        out_shape=jax.ShapeDtypeStruct((b, tp, FOX_WIDTH), BF16),
        scratch_shapes=[
            pltpu.VMEM((pairs, 2 * tq, LANES), BF16),
            pltpu.VMEM((FOX_HEADS, tq, LANES), F32),
            pltpu.VMEM((pairs, tq, 2 * LANES), F32),
        ],
        compiler_params=pltpu.CompilerParams(
            dimension_semantics=("parallel", "arbitrary"), vmem_limit_bytes=VMEM_LIMIT),
        name="fox",
    )(qv3, qv3, kt, ck)


SUBLANES = 8


LRU_SEQS = 2


def _lru_kernel(xy_ref, perm_ref, unperm_ref, cw_ref, cb_ref, wg_ref, bg_ref, lam_ref, o_ref,
                xe_s, a_s, h_s, hist_s, carry, *, seqs, tt, n_pad):
    @pl.when(pl.program_id(1) == 0)
    def _():
        hist_s[...] = jnp.zeros(hist_s.shape, F32)
        carry[...] = jnp.zeros(carry.shape, F32)

    for q in range(seqs):
        _lru_one(xy_ref.at[q], perm_ref, unperm_ref, cw_ref, cb_ref, wg_ref, bg_ref, lam_ref, o_ref.at[q],
                 xe_s.at[q], a_s.at[q], h_s.at[q], hist_s.at[q], carry.at[q], tt=tt, n_pad=n_pad)


def _lru_one(xy_ref, perm_ref, unperm_ref, cw_ref, cb_ref, wg_ref, bg_ref, lam_ref, o_ref,
             xe_s, a_s, h_s, hist_s, carry, *, tt, n_pad):
    it = pl.program_id(1)
    seg = tt // SUBLANES
    taps = CONV_WIDTH - 1
    w = LRU_WIDTH
    sub = lax.broadcasted_iota(jnp.int32, (SUBLANES, w), 0)
    xyp = _dot(perm_ref[...], xy_ref[...])
    xp, gate = xyp[:, 0:w], xyp[:, w:]

    for v in range(taps):
        src = slice((seg - taps + v) * SUBLANES, (seg - taps + v + 1) * SUBLANES)
        xe_s[v * SUBLANES:(v + 1) * SUBLANES, :] = jnp.where(
            sub == 0, pltpu.roll(hist_s[v], 1, 0), pltpu.roll(xp[src], 1, 0))
        hist_s[v] = xp[src]
    xe_s[taps * SUBLANES:, :] = xp
    conv = xe_s[0:tt, :] * cw_ref[0:1, :]
    for i in range(1, CONV_WIDTH):
        conv = conv + xe_s[i * SUBLANES:i * SUBLANES + tt, :] * cw_ref[i:i + 1, :]
    conv = conv + cb_ref[...]

    t = jnp.tanh(_dot(conv.astype(BF16), wg_ref[...]) + bg_ref[...])
    k = (0.5 * LRU_C) * _log_sigmoid(lam_ref[...])
    log_a = k * t[:, 0:w] + k
    p_idx = lax.broadcasted_iota(jnp.int32, (tt, 1), 0)
    time = it * tt + (p_idx & (SUBLANES - 1)) * seg + (p_idx >> 3)
    a = jnp.exp(log_a)
    quarter_one_minus_a2 = (-0.25 * jnp.tanh(log_a)) * (a * a + 1.0)
    b = jnp.where(time >= n_pad, jnp.sqrt(quarter_one_minus_a2) * (t[:, w:] * conv + conv), 0.0)

    h_loc = jnp.zeros((SUBLANES, w), F32)
    prod = jnp.ones((SUBLANES, w), F32)
    for v in range(seg):
        rows = slice(v * SUBLANES, (v + 1) * SUBLANES)
        h_loc = a[rows] * h_loc + b[rows]
        prod = a[rows] * prod
        h_s[rows, :] = h_loc
        a_s[rows, :] = prod
    aa, bb = prod, h_loc
    for s in (1, 2, 4):
        keep = sub >= s
        bb = jnp.where(keep, aa * pltpu.roll(bb, s, 0) + bb, bb)
        aa = jnp.where(keep, aa * pltpu.roll(aa, s, 0), aa)
    h_end = aa * carry[...] + bb
    h_in = jnp.where(sub == 0, carry[...], pltpu.roll(h_end, 1, 0))
    carry[...] = h_end[SUBLANES - 1:SUBLANES, :]

    h_in_all = jnp.concatenate([h_in] * seg, axis=0)
    out = ((h_s[...] + a_s[...] * h_in_all) * gate).astype(BF16)
    o_ref[...] = _dot(unperm_ref[...], out).astype(BF16)


def _lru(xy3, cw, cb, wg, bg, lam, n_pad):
    b, tp, _ = xy3.shape
    tt = _row_tile(tp)
    w = LRU_WIDTH
    seg = tt // SUBLANES
    p_idx = jnp.arange(tt)
    perm = (jnp.arange(tt)[None, :] == ((p_idx % SUBLANES) * seg + p_idx // SUBLANES)[:, None]).astype(BF16)
    seqs = LRU_SEQS if b % LRU_SEQS == 0 else 1
    return pl.pallas_call(
        functools.partial(_lru_kernel, seqs=seqs, tt=tt, n_pad=n_pad),
        grid=(b // seqs, tp // tt),
        in_specs=[
            pl.BlockSpec((seqs, tt, C_COLS), lambda i, j: (i, j, 0)),
            _resident((tt, tt)), _resident((tt, tt)),
            _resident((CONV_WIDTH, w)), _resident((1, w)), _resident((w, 2 * w)), _resident((1, 2 * w)),
            _resident((1, w)),
        ],
        out_specs=pl.BlockSpec((seqs, tt, w), lambda i, j: (i, j, 0)),
        out_shape=jax.ShapeDtypeStruct((b, tp, w), BF16),
        scratch_shapes=[
            pltpu.VMEM((seqs, tt + (CONV_WIDTH - 1) * SUBLANES, w), F32), pltpu.VMEM((seqs, tt, w), F32),
            pltpu.VMEM((seqs, tt, w), F32), pltpu.VMEM((seqs, CONV_WIDTH - 1, SUBLANES, w), F32),
            pltpu.VMEM((seqs, 1, w), F32),
        ],
        compiler_params=pltpu.CompilerParams(dimension_semantics=("parallel", "arbitrary"), vmem_limit_bytes=VMEM_LIMIT),
        name="lru",
    )(xy3, perm, perm.T, cw, cb, wg, bg, lam)


FF_CHUNK = 256


def _mix_ffn_rows(h, oa, of, oc, gt_ref, wb_ref, wo_ref, gn_ref, w1_ref, w2_ref, act_s):
    d = D_MODEL
    merged = gt_ref[:, 0:d].astype(F32) * _dot(oa, wb_ref[0])
    merged = merged + gt_ref[:, d:2 * d].astype(F32) * _dot(of, wb_ref[1])
    merged = merged + gt_ref[:, 2 * d:3 * d].astype(F32) * _dot(oc, wb_ref[2])
    hm = h + _dot(merged.astype(BF16), wo_ref[...])
    u = _rms_norm(hm, gn_ref[...]).astype(BF16)
    for c in range(0, D_FF, FF_CHUNK):
        gate = _dot(u, w1_ref[:, c:c + FF_CHUNK])
        up = _dot(u, w1_ref[:, D_FF + c:D_FF + c + FF_CHUNK])
        act_s[:, c:c + FF_CHUNK] = (gate * _sigmoid(gate) * up).astype(BF16)
    return hm + _dot(act_s[...], w2_ref[...])


def _mix_ffn_kernel(h_ref, oa_ref, of_ref, oc_ref, gt_ref, wb_ref, wo_ref, gn_ref, w1_ref, w2_ref, out_ref, act_s,
                    *, tm, n_pad):
    j = pl.program_id(1)
    y = _mix_ffn_rows(h_ref[0], oa_ref[0], of_ref[0], oc_ref[0], gt_ref.at[0], wb_ref, wo_ref, gn_ref, w1_ref, w2_ref,
                      act_s)
    row = j * tm + lax.broadcasted_iota(jnp.int32, (tm, 1), 0)
    out_ref[0] = jnp.where(row >= n_pad, y, 0.0)


def _mix_ffn_last_kernel(h_ref, oa_ref, of_ref, oc_ref, gt_ref, wb_ref, wo_ref, gn_ref, w1_ref, w2_ref, gf_ref,
                         out_ref, act_s):
    y = _mix_ffn_rows(h_ref[...], oa_ref[...], of_ref[...], oc_ref[...], gt_ref, wb_ref, wo_ref, gn_ref, w1_ref,
                      w2_ref, act_s)
    out_ref[0] = _rms_norm(y, gf_ref[...])


def _mix_ffn_last(h3, oa, of, oc, gt3, wb, wo, gn, w1, w2, gf, skip_rows, seq):
    b, tp, d = h3.shape
    tm = 512 if seq % 512 == 0 else BLOCK
    w = LRU_WIDTH

    def rows(cols):
        return pl.BlockSpec((pl.Element(tm), pl.Element(cols)),
                            lambda i, j: (pl.multiple_of(i * tp + skip_rows + j * tm, BLOCK), 0))

    flat = lambda a: a.reshape(b * tp, a.shape[-1])
    return pl.pallas_call(
        _mix_ffn_last_kernel,
        grid=(b, seq // tm),
        in_specs=[
            rows(d), rows(w), rows(w), rows(w), rows(G_COLS),
            _resident((N_BRANCH, w, d)), _resident((d, d)), _resident((1, d)),
            _resident((d, 2 * D_FF)), _resident((D_FF, d)), _resident((1, d)),
        ],
        out_specs=pl.BlockSpec((1, tm, d), lambda i, j: (i, j, 0)),
        out_shape=jax.ShapeDtypeStruct((b, seq, d), F32),
        scratch_shapes=[pltpu.VMEM((tm, D_FF), BF16)],
        compiler_params=pltpu.CompilerParams(dimension_semantics=("parallel", "parallel"), vmem_limit_bytes=VMEM_LIMIT),
        name="mix_ffn_last",
    )(flat(h3), flat(oa), flat(of), flat(oc), flat(gt3), wb, wo, gn, w1, w2, gf)


def _mix_ffn(h3, oa, of, oc, gt3, wb, wo, gn, w1, w2, n_pad):
    b, tp, d = h3.shape
    tm = _row_tile(tp)
    w = LRU_WIDTH
    tile = lambda cols: pl.BlockSpec((1, tm, cols), lambda i, j: (i, j, 0))
    return pl.pallas_call(
        functools.partial(_mix_ffn_kernel, tm=tm, n_pad=n_pad),
        grid=(b, tp // tm),
        in_specs=[
            tile(d), tile(w), tile(w), tile(w), tile(G_COLS),
            _resident((N_BRANCH, w, d)), _resident((d, d)), _resident((1, d)),
            _resident((d, 2 * D_FF)), _resident((D_FF, d)),
        ],
        out_specs=tile(d),
        out_shape=jax.ShapeDtypeStruct((b, tp, d), F32),
        scratch_shapes=[pltpu.VMEM((tm, D_FF), BF16)],
        compiler_params=pltpu.CompilerParams(dimension_semantics=("parallel", "parallel"), vmem_limit_bytes=VMEM_LIMIT),
        name="mix_ffn",
    )(h3, oa, of, oc, gt3, wb, wo, gn, w1, w2)


def _t5_bucket(dist):
    max_exact = REL_BUCKETS // 2
    d = jnp.maximum(dist, 0)
    scaled = jnp.log(jnp.maximum(d, 1).astype(F32) / max_exact) / math.log(REL_MAX_DIST / max_exact)
    large = jnp.minimum(max_exact + (scaled * (REL_BUCKETS - max_exact)).astype(jnp.int32), REL_BUCKETS - 1)
    return jnp.where(d < max_exact, d, large)


def _swa_bias_kernel(bucket_ref, table_ref, o_ref):
    bucket = bucket_ref[...]
    for h in range(SWA_Q_HEADS):
        acc = jnp.full(bucket.shape, NEG_INF, F32)
        for c in range(REL_BUCKETS):
            acc = jnp.where(bucket == c, table_ref[c, h] * LOG2E, acc)
        o_ref[h * BLOCK:(h + 1) * BLOCK, :] = acc


def _swa_bias(rel_table):
    q_idx = jnp.arange(BLOCK)[:, None]
    k_idx = jnp.arange(2 * BLOCK)[None, :]
    dist = q_idx + BLOCK - k_idx
    bucket = jnp.where((dist >= 0) & (dist < SWA_WINDOW), _t5_bucket(dist), -1).astype(jnp.int32)
    return pl.pallas_call(
        _swa_bias_kernel,
        in_specs=[pl.BlockSpec(memory_space=pltpu.VMEM), pl.BlockSpec(memory_space=pltpu.SMEM)],
        out_specs=pl.BlockSpec(memory_space=pltpu.VMEM),
        out_shape=jax.ShapeDtypeStruct((SWA_Q_HEADS * BLOCK, 2 * BLOCK), F32),
        name="swa_bias",
    )(bucket, rel_table.astype(F32))


def _to_slab_order(a, axis):
    half = SWA_Q_HEADS // 2
    shape = a.shape
    a = a.reshape(shape[:axis] + (2, half, HEAD_DIM) + shape[axis + 1:])
    return jnp.swapaxes(a, axis, axis + 1).reshape(shape)


def _pack_w_in(w_in):
    sizes = (512, 128, 128, 512, 512, 512, 8, 512, 512, 3072)
    offs = [0]
    for s in sizes:
        offs.append(offs[-1] + s)
    qa, ka, va, qf, kf, vf, fl, xc, yc, gates = [w_in[:, :, offs[i]:offs[i + 1]] for i in range(len(sizes))]
    scale = HEAD_DIM ** -0.5 * LOG2E
    qa = _to_slab_order(qa, axis=2) * scale
    fl = jnp.pad(fl, ((0, 0), (0, 0), (0, LANES - FOX_HEADS)))
    packed = jnp.concatenate([qa, va, fl, qf * scale, vf, xc, yc, gates], axis=-1)
    keys_t = jnp.swapaxes(jnp.concatenate([kf, ka], axis=-1), 1, 2)
    return packed.astype(BF16), keys_t.astype(BF16)


def _block_diag(w):
    depth, nb, n, _ = w.shape
    eye = jnp.eye(nb, dtype=w.dtype)
    return jnp.einsum('lhij,hk->lhikj', w, eye).reshape(depth, nb * n, nb * n)


def kernel(x, meta_tokens, rel_bias_table, norm_mix, w_in, swa_sinks, fox_forget_bias, conv_w, conv_b,
           lru_w_r, lru_b_r, lru_w_i, lru_b_i, lru_lambda, w_branch, w_out, norm_ffn, w_ffn_in, w_ffn_out,
           norm_final):
    b, seq, d = x.shape
    depth = w_in.shape[0]
    t = N_META + seq
    n_pad = (-t) % BLOCK
    tp = t + n_pad
    assert d == D_MODEL and (n_pad + N_META) % BLOCK == 0

    meta = jnp.broadcast_to(meta_tokens.astype(x.dtype)[None], (b, N_META, d))
    h = jnp.concatenate([jnp.zeros((b, n_pad, d), x.dtype), meta, x], axis=1)

    w_in_p, wkt = _pack_w_in(w_in)
    bias_a = _swa_bias(rel_bias_table)
    sinks = jnp.broadcast_to((swa_sinks.astype(F32) * LOG2E)[:, :, None, None], (depth, SWA_Q_HEADS, BLOCK, LANES))
    sinks = sinks.reshape(depth, SWA_Q_HEADS * BLOCK, LANES)
    fb = jnp.pad(fox_forget_bias.astype(F32), ((0, 0), (0, LANES - FOX_HEADS)))[:, None, :]
    wg = (0.5 * jnp.concatenate([_block_diag(lru_w_r), _block_diag(lru_w_i)], axis=-1)).astype(BF16)
    bg = 0.5 * jnp.concatenate([lru_b_r, lru_b_i], axis=-1).astype(F32)[:, None, :]
    wb_a = _to_slab_order(w_branch[:, 0], axis=1)
    wb = jnp.stack([wb_a, w_branch[:, 1], w_branch[:, 2]], axis=1).astype(BF16)
    wo = w_out.astype(BF16)
    w1 = w_ffn_in.astype(BF16)
    w2 = w_ffn_out.astype(BF16)

    for l in range(depth):
        qva, qvf, kt, fl, xy, gt = _in_proj(h.reshape(b * tp, d), norm_mix[l][None, :], w_in_p[l], wkt[l])
        ck = _fox_prefix(fl.reshape(b, tp, LANES), fb[l], n_pad)
        o_a = _swa(qva.reshape(b, tp, A_COLS), kt, bias_a, sinks[l], n_pad)
        o_f = _fox(qvf.reshape(b, tp, F_COLS), kt, ck, n_pad)
        o_c = _lru(xy.reshape(b, tp, C_COLS), conv_w[l], conv_b[l][None, :], wg[l], bg[l], lru_lambda[l][None, :],
                   n_pad)
        mix_args = (h, o_a, o_f, o_c, gt.reshape(b, tp, G_COLS), wb[l], wo[l], norm_ffn[l][None, :], w1[l], w2[l])
        if l + 1 < depth:
            h = _mix_ffn(*mix_args, n_pad)
    return _mix_ffn_last(*mix_args, norm_final[None, :], n_pad + N_META, seq)
```

```python
import functools
import math

import jax
import jax.numpy as jnp
from jax import lax
from jax.experimental import pallas as pl
from jax.experimental.pallas import tpu as pltpu

F32 = jnp.float32
BF16 = jnp.bfloat16

D_MODEL = 1024
HEAD_DIM = 64
N_META = 16
BLOCK = 128
LANES = 128
NEG_INF = -1e30
LOG2E = math.log2(math.e)
SWA_WINDOW = 128
SWA_Q_HEADS = 8
SWA_KV_HEADS = 2
FOX_HEADS = 8
LRU_WIDTH = D_MODEL // 2
LRU_BLOCKS = 8
CONV_WIDTH = 4
LRU_C = 8.0
REL_BUCKETS = 32
REL_MAX_DIST = 128
D_FF = 2816
N_BRANCH = 3
EPS = 1e-6

SWA_QCOLS = SWA_Q_HEADS * HEAD_DIM
A_COLS = SWA_QCOLS + SWA_KV_HEADS * HEAD_DIM
FOX_WIDTH = FOX_HEADS * HEAD_DIM
F_COLS = 2 * FOX_WIDTH
KT_ROWS = FOX_WIDTH + SWA_KV_HEADS * HEAD_DIM
C_COLS = 2 * LRU_WIDTH
G_COLS = N_BRANCH * D_MODEL
OFF_A = 0
OFF_FL = OFF_A + A_COLS
OFF_F = OFF_FL + LANES
OFF_C = OFF_F + F_COLS
OFF_G = OFF_C + C_COLS
IN_COLS_PACKED = OFF_G + G_COLS

VMEM_LIMIT = 56 * 1024 * 1024


def _sigmoid(x):
    return 0.5 * jnp.tanh(0.5 * x) + 0.5


def _log_sigmoid(x):
    return jnp.minimum(x, 0.0) - jnp.log1p(jnp.exp(-jnp.abs(x)))


def _gelu_tanh(x):
    c = math.sqrt(2.0 / math.pi)
    return 0.5 * x * (1.0 + jnp.tanh(c * (x + 0.044715 * (x * x * x))))


def _rms_norm(x, g):
    ms = jnp.mean(x * x, axis=-1, keepdims=True)
    return x * lax.rsqrt(ms + EPS) * g


def _dot(a, b):
    return jnp.dot(a, b, preferred_element_type=F32)


def _dot_nt(a, b):
    return lax.dot_general(a, b, (((1,), (1,)), ((), ())), preferred_element_type=F32)


def _resident(shape):
    nd = len(shape)
    return pl.BlockSpec(shape, lambda *_: (0,) * nd, pipeline_mode=pl.Buffered(1))


def _row_tile(tp):
    return 384 if tp % 384 == 0 else BLOCK


def _pv_operand(v, lo_k):
    zero = jnp.zeros_like(v)
    ind_lo = jnp.where(lo_k, 1.0, 0.0).astype(BF16)
    ind_hi = jnp.where(lo_k, 0.0, 1.0).astype(BF16)
    top = jnp.concatenate([jnp.where(lo_k, v, zero), ind_lo], axis=1)
    bot = jnp.concatenate([jnp.where(lo_k, zero, v), ind_hi], axis=1)
    return jnp.concatenate([top, bot], axis=0)


def _in_proj_kernel(x_ref, g_ref, w_ref, wkt_ref, qa_ref, qf_ref, kt_ref, fl_ref, xy_ref, gt_ref):
    rows = x_ref.shape[0]
    group = rows // 4 if rows % 64 == 0 else rows
    parts = []
    for r0 in range(0, rows, group):
        part = _rms_norm(x_ref[r0:r0 + group, :], g_ref[...]).astype(BF16)
        qa_ref[r0:r0 + group, 0:SWA_QCOLS] = _dot(part, w_ref[:, OFF_A:OFF_A + SWA_QCOLS]).astype(BF16)
        parts.append(part)
    u = jnp.concatenate(parts, axis=0)

    def proj(c0, width):
        return _dot(u, w_ref[:, c0:c0 + width])

    v_fl = proj(OFF_A + SWA_QCOLS, 2 * LANES)
    qa_ref[:, SWA_QCOLS:] = v_fl[:, 0:LANES].astype(BF16)
    fl_ref[...] = v_fl[:, LANES:]
    for c in range(0, F_COLS, 512):
        qf_ref[:, c:c + 512] = proj(OFF_F + c, 512).astype(BF16)
    kt_ref[...] = _dot_nt(wkt_ref[...], u).astype(BF16)
    xy_ref[:, 0:LRU_WIDTH] = proj(OFF_C, LRU_WIDTH).astype(BF16)
    xy_ref[:, LRU_WIDTH:] = _gelu_tanh(proj(OFF_C + LRU_WIDTH, LRU_WIDTH)).astype(BF16)
    for c in range(0, G_COLS, 512):
        gt_ref[:, c:c + 512] = _sigmoid(proj(OFF_G + c, 512)).astype(BF16)


def _in_proj(h2, g, w, wkt):
    rows = h2.shape[0]
    tm = next(t for t in (768, 384, BLOCK) if rows % t == 0)
    row_spec = lambda cols: pl.BlockSpec((tm, cols), lambda i: (i, 0))
    return pl.pallas_call(
        _in_proj_kernel,
        grid=(rows // tm,),
        in_specs=[row_spec(D_MODEL), _resident((1, D_MODEL)), _resident((D_MODEL, IN_COLS_PACKED)),
                  _resident((KT_ROWS, D_MODEL))],
        out_specs=[row_spec(A_COLS), row_spec(F_COLS), pl.BlockSpec((KT_ROWS, tm), lambda i: (0, i)),
                   row_spec(LANES), row_spec(C_COLS), row_spec(G_COLS)],
        out_shape=[
            jax.ShapeDtypeStruct((rows, A_COLS), BF16),
            jax.ShapeDtypeStruct((rows, F_COLS), BF16),
            jax.ShapeDtypeStruct((KT_ROWS, rows), BF16),
            jax.ShapeDtypeStruct((rows, LANES), F32),
            jax.ShapeDtypeStruct((rows, C_COLS), BF16),
            jax.ShapeDtypeStruct((rows, G_COLS), BF16),
        ],
        compiler_params=pltpu.CompilerParams(dimension_semantics=("parallel",), vmem_limit_bytes=VMEM_LIMIT),
        name="in_proj",
    )(h2, g, w, wkt)


def _fox_prefix_kernel(fl_ref, fb_ref, ck_ref, cum_ref, *, tp, n_pad):
    lf = _log_sigmoid(fl_ref[0] + fb_ref[...])
    row = lax.broadcasted_iota(jnp.int32, (tp, 1), 0)
    lf = jnp.where(row >= n_pad, lf, 0.0)
    r = lax.broadcasted_iota(jnp.int32, (BLOCK, BLOCK), 0)
    c = lax.broadcasted_iota(jnp.int32, (BLOCK, BLOCK), 1)
    tri = jnp.where(r >= c, 1.0, 0.0).astype(F32)
    carry = jnp.zeros((1, LANES), F32)
    for blk in range(tp // BLOCK):
        sl = slice(blk * BLOCK, (blk + 1) * BLOCK)
        cum = jnp.dot(tri, lf[sl], preferred_element_type=F32, precision=lax.Precision.HIGHEST) + carry
        cum_ref[sl, :] = cum
        carry = cum[BLOCK - 1:BLOCK, :]
    ck_ref[0] = cum_ref[...].T[0:FOX_HEADS, :]


def _fox_prefix(fl3, fb, n_pad):
    b, tp, _ = fl3.shape
    return pl.pallas_call(
        functools.partial(_fox_prefix_kernel, tp=tp, n_pad=n_pad),
        grid=(b,),
        in_specs=[pl.BlockSpec((1, tp, LANES), lambda i: (i, 0, 0)), _resident((1, LANES))],
        out_specs=pl.BlockSpec((1, FOX_HEADS, tp), lambda i: (i, 0, 0)),
        out_shape=jax.ShapeDtypeStruct((b, FOX_HEADS, tp), F32),
        scratch_shapes=[pltpu.VMEM((tp, LANES), F32)],
        compiler_params=pltpu.CompilerParams(dimension_semantics=("parallel",), vmem_limit_bytes=VMEM_LIMIT),
        name="fox_prefix",
    )(fl3, fb)


def _swa_kernel(q_ref, v_ref, vp_ref, kt_ref, ktp_ref, bias_ref, sink_ref, o_ref, *, blocks, n_pad):
    ib = pl.program_id(1)
    half = SWA_Q_HEADS // 2
    rows = half * BLOCK
    lo = lax.broadcasted_iota(jnp.int32, (BLOCK, LANES), 1) < HEAD_DIM
    lo_k = lax.broadcasted_iota(jnp.int32, (2 * BLOCK, LANES), 1) < HEAD_DIM
    lo_o = lax.broadcasted_iota(jnp.int32, (rows, LANES), 1) < HEAD_DIM
    kidx = lax.broadcasted_iota(jnp.int32, (1, BLOCK), 1)
    sink = sink_ref[...]
    for g in range(blocks):
        cur = slice(g * BLOCK, (g + 1) * BLOCK)
        if g == 0:
            kt_prev, v_prev = ktp_ref[...], vp_ref[0]
        else:
            prev = slice((g - 1) * BLOCK, g * BLOCK)
            kt_prev, v_prev = kt_ref[:, prev], v_ref[0, prev, :]
        kt_band = jnp.concatenate([kt_prev, kt_ref[:, cur]], axis=1)
        v_band = jnp.concatenate([v_prev, v_ref[0, cur, :]], axis=0)
        slabs = [q_ref[0, cur, j * LANES:(j + 1) * LANES] for j in range(half)]
        zero = jnp.zeros_like(slabs[0])
        q_all = jnp.concatenate([jnp.where(lo, s, zero) for s in slabs] + [jnp.where(lo, zero, s) for s in slabs],
                                axis=0)
        s = _dot(q_all, kt_band) + bias_ref[...]
        key0 = (ib * blocks + g - 1) * BLOCK + kidx
        s_prev = jnp.where(key0 >= n_pad, s[:, 0:BLOCK], NEG_INF)
        s_cur = jnp.where(key0 + BLOCK >= n_pad, s[:, BLOCK:], NEG_INF)
        m = jnp.maximum(jnp.max(jnp.maximum(s_prev, s_cur), axis=-1, keepdims=True), sink)
        p = jnp.concatenate([jnp.exp2(s_prev - m), jnp.exp2(s_cur - m)], axis=1).astype(BF16)
        pv = _dot(jnp.concatenate([p[0:rows], p[rows:]], axis=1), _pv_operand(v_band, lo_k))
        esink = jnp.exp2(sink - m)
        den = pv[:, LANES:] + jnp.where(lo_o, esink[0:rows], esink[rows:])
        o = (pv[:, 0:LANES] / den).astype(BF16)
        for j in range(half):
            o_ref[0, cur, j * LANES:(j + 1) * LANES] = o[j * BLOCK:(j + 1) * BLOCK]


def _swa(qv3, kt, bias, sinks, n_pad):
    b, tp, _ = qv3.shape
    nb = tp // BLOCK
    blocks = next(n for n in (33, 11, 3, 1) if nb % n == 0)
    rows = blocks * BLOCK
    steps = nb // blocks
    vblk = SWA_QCOLS // LANES
    krow = FOX_WIDTH // LANES
    return pl.pallas_call(
        functools.partial(_swa_kernel, blocks=blocks, n_pad=n_pad),
        grid=(b, steps),
        in_specs=[
            pl.BlockSpec((1, rows, SWA_QCOLS), lambda i, j: (i, j, 0)),
            pl.BlockSpec((1, rows, LANES), lambda i, j: (i, j, vblk)),
            pl.BlockSpec((1, BLOCK, LANES), lambda i, j: (i, jnp.maximum(j * blocks - 1, 0), vblk)),
            pl.BlockSpec((LANES, rows), lambda i, j: (krow, i * steps + j)),
            pl.BlockSpec((LANES, BLOCK), lambda i, j: (krow, jnp.maximum((i * steps + j) * blocks - 1, 0))),
            _resident((SWA_Q_HEADS * BLOCK, 2 * BLOCK)),
            _resident((SWA_Q_HEADS * BLOCK, LANES)),
        ],
        out_specs=pl.BlockSpec((1, rows, SWA_QCOLS), lambda i, j: (i, j, 0)),
        out_shape=jax.ShapeDtypeStruct((b, tp, SWA_QCOLS), BF16),
        compiler_params=pltpu.CompilerParams(dimension_semantics=("parallel", "parallel"), vmem_limit_bytes=VMEM_LIMIT),
        name="swa",
    )(qv3, qv3, qv3, kt, kt, bias, sinks)


FOX_TQ = 384
FOX_TK = 512


def _fox_kernel(q_ref, v_ref, kt_ref, ck_ref, o_ref, qa_s, m_s, acc_s, *, tq, tk, n_tiles, tail, n_pad):
    i = pl.program_id(1)
    pairs = FOX_HEADS // 2
    lane = lax.broadcasted_iota(jnp.int32, (tq, LANES), 1)
    lo = lane < HEAD_DIM
    for hp in range(pairs):
        q = q_ref[0, :, hp * LANES:(hp + 1) * LANES]
        zero = jnp.zeros_like(q)
        qa_s[hp, 0:tq, :] = jnp.where(lo, q, zero)
        qa_s[hp, tq:, :] = jnp.where(lo, zero, q)
    m_s[...] = jnp.full(m_s.shape, NEG_INF, F32)
    acc_s[...] = jnp.zeros(acc_s.shape, F32)

    def tile(start, size, masked):
        key_ok = start + lax.broadcasted_iota(jnp.int32, (1, size), 1) >= n_pad
        lo_k = lax.broadcasted_iota(jnp.int32, (size, LANES), 1) < HEAD_DIM
        if masked:
            rel = lax.broadcasted_iota(jnp.int32, (tq, size), 1) - lax.broadcasted_iota(jnp.int32, (tq, size), 0)
            causal = jnp.where(rel <= i * tq - start, 0.0, NEG_INF)
        for hp in range(pairs):
            kt = kt_ref[hp * LANES:(hp + 1) * LANES, pl.ds(start, size)]
            s2 = _dot(qa_s[hp], kt)
            ps, alphas = [], []
            for e in range(2):
                c = jnp.where(key_ok, ck_ref[0, 2 * hp + e:2 * hp + e + 1, pl.ds(start, size)] * (-LOG2E), NEG_INF)
                s = s2[e * tq:(e + 1) * tq] + c
                if masked:
                    s = s + causal
                chunks = [s[:, c0:c0 + LANES] for c0 in range(0, size, LANES)]
                m_prev = m_s[2 * hp + e]
                m_new = jnp.maximum(m_prev, jnp.max(functools.reduce(jnp.maximum, chunks), axis=-1, keepdims=True))
                alphas.append(jnp.exp2(m_prev - m_new))
                ps.append(jnp.concatenate([jnp.exp2(ch - m_new) for ch in chunks], axis=1).astype(BF16))
                m_s[2 * hp + e] = m_new
            alpha = jnp.where(lo, alphas[0], alphas[1])
            alpha = jnp.concatenate([alpha, alpha], axis=1)
            v = v_ref[0, pl.ds(start, size), hp * LANES:(hp + 1) * LANES]
            acc_s[hp] = alpha * acc_s[hp] + _dot(jnp.concatenate(ps, axis=1), _pv_operand(v, lo_k))

    n_last = jnp.minimum((tq * (i + 1) - 1) // tk, n_tiles - 1)
    n_free = (tq * i + 1) // tk

    def full_tile(j, masked):
        tile(pl.multiple_of(j * tk, tk), tk, masked)

    def four_free(j4, carry):
        for u in range(4):
            full_tile(4 * j4 + u, False)
        return carry

    def one_masked(j, carry):
        full_tile(j, True)
        return carry

    def two_free(j):
        full_tile(j, False)
        full_tile(j + 1, False)

    lax.fori_loop(0, n_free // 4, four_free, 0)
    pl.when(n_free % 4 >= 2)(lambda: two_free(n_free - n_free % 4))
    pl.when(n_free % 2 == 1)(lambda: full_tile(n_free - 1, False))
    lax.fori_loop(n_free, n_last, one_masked, 0)
    last_start = pl.multiple_of(n_last * tk, tk)
    reach = jnp.minimum(tq * (i + 1) - n_last * tk, tk)
    for size in range(LANES, tk + 1, LANES):
        pl.when(reach == size)(functools.partial(tile, last_start, size, True))
    if tail:
        pl.when(tq * (i + 1) > n_tiles * tk)(lambda: tile(n_tiles * tk, tail, True))

    for hp in range(pairs):
        acc = acc_s[hp]
        o_ref[0, :, hp * LANES:(hp + 1) * LANES] = (acc[:, 0:LANES] / acc[:, LANES:]).astype(BF16)


def _fox(qv3, kt, ck, n_pad):
    b, tp, _ = qv3.shape
    tq, tk = FOX_TQ, FOX_TK
    n_tiles, tail = tp // tk, tp % tk
    assert tp % tq == 0 and tail % LANES == 0 and tail <= tq and n_tiles >= 1
    pairs = FOX_HEADS // 2
    return pl.pallas_call(
        functools.partial(_fox_kernel, tq=tq, tk=tk, n_tiles=n_tiles, tail=tail, n_pad=n_pad),
        grid=(b, tp // tq),
        in_specs=[
            pl.BlockSpec((1, tq, FOX_WIDTH), lambda i, j: (i, j, 0)),
            pl.BlockSpec((1, tp, FOX_WIDTH), lambda i, j: (i, 0, 1)),
            pl.BlockSpec((FOX_WIDTH, tp), lambda i, j: (0, i)),
            pl.BlockSpec((1, FOX_HEADS, tp), lambda i, j: (i, 0, 0)),
        ],
        out_specs=pl.BlockSpec((1, tq, FOX_WIDTH), lambda i, j: (i, j, 0)),
        out_shape=jax.ShapeDtypeStruct((b, tp, FOX_WIDTH), BF16),
        scratch_shapes=[
            pltpu.VMEM((pairs, 2 * tq, LANES), BF16),
            pltpu.VMEM((FOX_HEADS, tq, LANES), F32),
            pltpu.VMEM((pairs, tq, 2 * LANES), F32),
        ],
        compiler_params=pltpu.CompilerParams(
            dimension_semantics=("parallel", "arbitrary"), vmem_limit_bytes=VMEM_LIMIT),
        name="fox",
    )(qv3, qv3, kt, ck)


SUBLANES = 8


LRU_SEQS = 2


def _lru_kernel(xy_ref, perm_ref, unperm_ref, cw_ref, cb_ref, wg_ref, bg_ref, lam_ref, o_ref,
                xe_s, a_s, h_s, hist_s, carry, *, seqs, tt, n_pad):
    @pl.when(pl.program_id(1) == 0)
    def _():
        hist_s[...] = jnp.zeros(hist_s.shape, F32)
        carry[...] = jnp.zeros(carry.shape, F32)

    for q in range(seqs):
        _lru_one(xy_ref.at[q], perm_ref, unperm_ref, cw_ref, cb_ref, wg_ref, bg_ref, lam_ref, o_ref.at[q],
                 xe_s.at[q], a_s.at[q], h_s.at[q], hist_s.at[q], carry.at[q], tt=tt, n_pad=n_pad)


def _lru_one(xy_ref, perm_ref, unperm_ref, cw_ref, cb_ref, wg_ref, bg_ref, lam_ref, o_ref,
             xe_s, a_s, h_s, hist_s, carry, *, tt, n_pad):
    it = pl.program_id(1)
    seg = tt // SUBLANES
    taps = CONV_WIDTH - 1
    w = LRU_WIDTH
    sub = lax.broadcasted_iota(jnp.int32, (SUBLANES, w), 0)
    xyp = _dot(perm_ref[...], xy_ref[...])
    xp, gate = xyp[:, 0:w], xyp[:, w:]

    for v in range(taps):
        src = slice((seg - taps + v) * SUBLANES, (seg - taps + v + 1) * SUBLANES)
        xe_s[v * SUBLANES:(v + 1) * SUBLANES, :] = jnp.where(
            sub == 0, pltpu.roll(hist_s[v], 1, 0), pltpu.roll(xp[src], 1, 0))
        hist_s[v] = xp[src]
    xe_s[taps * SUBLANES:, :] = xp
    conv = xe_s[0:tt, :] * cw_ref[0:1, :]
    for i in range(1, CONV_WIDTH):
        conv = conv + xe_s[i * SUBLANES:i * SUBLANES + tt, :] * cw_ref[i:i + 1, :]
    conv = conv + cb_ref[...]

    t = jnp.tanh(_dot(conv.astype(BF16), wg_ref[...]) + bg_ref[...])
    k = (0.5 * LRU_C) * _log_sigmoid(lam_ref[...])
    log_a = k * t[:, 0:w] + k
    p_idx = lax.broadcasted_iota(jnp.int32, (tt, 1), 0)
    time = it * tt + (p_idx & (SUBLANES - 1)) * seg + (p_idx >> 3)
    a = jnp.exp(log_a)
    quarter_one_minus_a2 = (-0.25 * jnp.tanh(log_a)) * (a * a + 1.0)
    b = jnp.where(time >= n_pad, jnp.sqrt(quarter_one_minus_a2) * (t[:, w:] * conv + conv), 0.0)

    h_loc = jnp.zeros((SUBLANES, w), F32)
    prod = jnp.ones((SUBLANES, w), F32)
    for v in range(seg):
        rows = slice(v * SUBLANES, (v + 1) * SUBLANES)
        h_loc = a[rows] * h_loc + b[rows]
        prod = a[rows] * prod
        h_s[rows, :] = h_loc
        a_s[rows, :] = prod
    aa, bb = prod, h_loc
    for s in (1, 2, 4):
        keep = sub >= s
        bb = jnp.where(keep, aa * pltpu.roll(bb, s, 0) + bb, bb)
        aa = jnp.where(keep, aa * pltpu.roll(aa, s, 0), aa)
    h_end = aa * carry[...] + bb
    h_in = jnp.where(sub == 0, carry[...], pltpu.roll(h_end, 1, 0))
    carry[...] = h_end[SUBLANES - 1:SUBLANES, :]

    h_in_all = jnp.concatenate([h_in] * seg, axis=0)
    out = ((h_s[...] + a_s[...] * h_in_all) * gate).astype(BF16)
    o_ref[...] = _dot(unperm_ref[...], out).astype(BF16)


def _lru(xy3, cw, cb, wg, bg, lam, n_pad):
    b, tp, _ = xy3.shape
    tt = _row_tile(tp)
    w = LRU_WIDTH
    seg = tt // SUBLANES
    p_idx = jnp.arange(tt)
    perm = (jnp.arange(tt)[None, :] == ((p_idx % SUBLANES) * seg + p_idx // SUBLANES)[:, None]).astype(BF16)
    seqs = LRU_SEQS if b % LRU_SEQS == 0 else 1
    return pl.pallas_call(
        functools.partial(_lru_kernel, seqs=seqs, tt=tt, n_pad=n_pad),
        grid=(b // seqs, tp // tt),
        in_specs=[
            pl.BlockSpec((seqs, tt, C_COLS), lambda i, j: (i, j, 0)),
            _resident((tt, tt)), _resident((tt, tt)),
            _resident((CONV_WIDTH, w)), _resident((1, w)), _resident((w, 2 * w)), _resident((1, 2 * w)),
            _resident((1, w)),
        ],
        out_specs=pl.BlockSpec((seqs, tt, w), lambda i, j: (i, j, 0)),
        out_shape=jax.ShapeDtypeStruct((b, tp, w), BF16),
        scratch_shapes=[
            pltpu.VMEM((seqs, tt + (CONV_WIDTH - 1) * SUBLANES, w), F32), pltpu.VMEM((seqs, tt, w), F32),
            pltpu.VMEM((seqs, tt, w), F32), pltpu.VMEM((seqs, CONV_WIDTH - 1, SUBLANES, w), F32),
            pltpu.VMEM((seqs, 1, w), F32),
        ],
        compiler_params=pltpu.CompilerParams(dimension_semantics=("parallel", "arbitrary"), vmem_limit_bytes=VMEM_LIMIT),
        name="lru",
    )(xy3, perm, perm.T, cw, cb, wg, bg, lam)


FF_CHUNK = 256


def _mix_ffn_rows(h, oa, of, oc, gt_ref, wb_ref, wo_ref, gn_ref, w1_ref, w2_ref, act_s):
    d = D_MODEL
    merged = gt_ref[:, 0:d].astype(F32) * _dot(oa, wb_ref[0])
    merged = merged + gt_ref[:, d:2 * d].astype(F32) * _dot(of, wb_ref[1])
    merged = merged + gt_ref[:, 2 * d:3 * d].astype(F32) * _dot(oc, wb_ref[2])
    hm = h + _dot(merged.astype(BF16), wo_ref[...])
    u = _rms_norm(hm, gn_ref[...]).astype(BF16)
    for c in range(0, D_FF, FF_CHUNK):
        gate = _dot(u, w1_ref[:, c:c + FF_CHUNK])
        up = _dot(u, w1_ref[:, D_FF + c:D_FF + c + FF_CHUNK])
        act_s[:, c:c + FF_CHUNK] = (gate * _sigmoid(gate) * up).astype(BF16)
    return hm + _dot(act_s[...], w2_ref[...])


def _mix_ffn_kernel(h_ref, oa_ref, of_ref, oc_ref, gt_ref, wb_ref, wo_ref, gn_ref, w1_ref, w2_ref, out_ref, act_s,
                    *, tm, n_pad):
    j = pl.program_id(1)
    y = _mix_ffn_rows(h_ref[0], oa_ref[0], of_ref[0], oc_ref[0], gt_ref.at[0], wb_ref, wo_ref, gn_ref, w1_ref, w2_ref,
                      act_s)
    row = j * tm + lax.broadcasted_iota(jnp.int32, (tm, 1), 0)
    out_ref[0] = jnp.where(row >= n_pad, y, 0.0)


def _mix_ffn_last_kernel(h_ref, oa_ref, of_ref, oc_ref, gt_ref, wb_ref, wo_ref, gn_ref, w1_ref, w2_ref, gf_ref,
                         out_ref, act_s):
    y = _mix_ffn_rows(h_ref[...], oa_ref[...], of_ref[...], oc_ref[...], gt_ref, wb_ref, wo_ref, gn_ref, w1_ref,
                      w2_ref, act_s)
    out_ref[0] = _rms_norm(y, gf_ref[...])


def _mix_ffn_last(h3, oa, of, oc, gt3, wb, wo, gn, w1, w2, gf, skip_rows, seq):
    b, tp, d = h3.shape
    tm = 512 if seq % 512 == 0 else BLOCK
    w = LRU_WIDTH

    def rows(cols):
        return pl.BlockSpec((pl.Element(tm), pl.Element(cols)),
                            lambda i, j: (pl.multiple_of(i * tp + skip_rows + j * tm, BLOCK), 0))

    flat = lambda a: a.reshape(b * tp, a.shape[-1])
    return pl.pallas_call(
        _mix_ffn_last_kernel,
        grid=(b, seq // tm),
        in_specs=[
            rows(d), rows(w), rows(w), rows(w), rows(G_COLS),
            _resident((N_BRANCH, w, d)), _resident((d, d)), _resident((1, d)),
            _resident((d, 2 * D_FF)), _resident((D_FF, d)), _resident((1, d)),
        ],
        out_specs=pl.BlockSpec((1, tm, d), lambda i, j: (i, j, 0)),
        out_shape=jax.ShapeDtypeStruct((b, seq, d), F32),
        scratch_shapes=[pltpu.VMEM((tm, D_FF), BF16)],
        compiler_params=pltpu.CompilerParams(dimension_semantics=("parallel", "parallel"), vmem_limit_bytes=VMEM_LIMIT),
        name="mix_ffn_last",
    )(flat(h3), flat(oa), flat(of), flat(oc), flat(gt3), wb, wo, gn, w1, w2, gf)


def _mix_ffn(h3, oa, of, oc, gt3, wb, wo, gn, w1, w2, n_pad):
    b, tp, d = h3.shape
    tm = _row_tile(tp)
    w = LRU_WIDTH
    tile = lambda cols: pl.BlockSpec((1, tm, cols), lambda i, j: (i, j, 0))
    return pl.pallas_call(
        functools.partial(_mix_ffn_kernel, tm=tm, n_pad=n_pad),
        grid=(b, tp // tm),
        in_specs=[
            tile(d), tile(w), tile(w), tile(w), tile(G_COLS),
            _resident((N_BRANCH, w, d)), _resident((d, d)), _resident((1, d)),
            _resident((d, 2 * D_FF)), _resident((D_FF, d)),
        ],
        out_specs=tile(d),
        out_shape=jax.ShapeDtypeStruct((b, tp, d), F32),
        scratch_shapes=[pltpu.VMEM((tm, D_FF), BF16)],
        compiler_params=pltpu.CompilerParams(dimension_semantics=("parallel", "parallel"), vmem_limit_bytes=VMEM_LIMIT),
        name="mix_ffn",
    )(h3, oa, of, oc, gt3, wb, wo, gn, w1, w2)


def _t5_bucket(dist):
    max_exact = REL_BUCKETS // 2
    d = jnp.maximum(dist, 0)
    scaled = jnp.log(jnp.maximum(d, 1).astype(F32) / max_exact) / math.log(REL_MAX_DIST / max_exact)
    large = jnp.minimum(max_exact + (scaled * (REL_BUCKETS - max_exact)).astype(jnp.int32), REL_BUCKETS - 1)
    return jnp.where(d < max_exact, d, large)


def _swa_bias_kernel(bucket_ref, table_ref, o_ref):
    bucket = bucket_ref[...]
    for h in range(SWA_Q_HEADS):
        acc = jnp.full(bucket.shape, NEG_INF, F32)
        for c in range(REL_BUCKETS):
            acc = jnp.where(bucket == c, table_ref[c, h] * LOG2E, acc)
        o_ref[h * BLOCK:(h + 1) * BLOCK, :] = acc


def _swa_bias(rel_table):
    q_idx = jnp.arange(BLOCK)[:, None]
    k_idx = jnp.arange(2 * BLOCK)[None, :]
    dist = q_idx + BLOCK - k_idx
    bucket = jnp.where((dist >= 0) & (dist < SWA_WINDOW), _t5_bucket(dist), -1).astype(jnp.int32)
    return pl.pallas_call(
        _swa_bias_kernel,
        in_specs=[pl.BlockSpec(memory_space=pltpu.VMEM), pl.BlockSpec(memory_space=pltpu.SMEM)],
        out_specs=pl.BlockSpec(memory_space=pltpu.VMEM),
        out_shape=jax.ShapeDtypeStruct((SWA_Q_HEADS * BLOCK, 2 * BLOCK), F32),
        name="swa_bias",
    )(bucket, rel_table.astype(F32))


def _to_slab_order(a, axis):
    half = SWA_Q_HEADS // 2
    shape = a.shape
    a = a.reshape(shape[:axis] + (2, half, HEAD_DIM) + shape[axis + 1:])
    return jnp.swapaxes(a, axis, axis + 1).reshape(shape)


def _pack_w_in(w_in):
    sizes = (512, 128, 128, 512, 512, 512, 8, 512, 512, 3072)
    offs = [0]
    for s in sizes:
        offs.append(offs[-1] + s)
    qa, ka, va, qf, kf, vf, fl, xc, yc, gates = [w_in[:, :, offs[i]:offs[i + 1]] for i in range(len(sizes))]
    scale = HEAD_DIM ** -0.5 * LOG2E
    qa = _to_slab_order(qa, axis=2) * scale
    fl = jnp.pad(fl, ((0, 0), (0, 0), (0, LANES - FOX_HEADS)))
    packed = jnp.concatenate([qa, va, fl, qf * scale, vf, xc, yc, gates], axis=-1)
    keys_t = jnp.swapaxes(jnp.concatenate([kf, ka], axis=-1), 1, 2)
    return packed.astype(BF16), keys_t.astype(BF16)


def _block_diag(w):
    depth, nb, n, _ = w.shape
    eye = jnp.eye(nb, dtype=w.dtype)
    return jnp.einsum('lhij,hk->lhikj', w, eye).reshape(depth, nb * n, nb * n)


def kernel(x, meta_tokens, rel_bias_table, norm_mix, w_in, swa_sinks, fox_forget_bias, conv_w, conv_b,
           lru_w_r, lru_b_r, lru_w_i, lru_b_i, lru_lambda, w_branch, w_out, norm_ffn, w_ffn_in, w_ffn_out,
           norm_final):
    b, seq, d = x.shape
    depth = w_in.shape[0]
    t = N_META + seq
    n_pad = (-t) % BLOCK
    tp = t + n_pad
    assert d == D_MODEL and (n_pad + N_META) % BLOCK == 0

    meta = jnp.broadcast_to(meta_tokens.astype(x.dtype)[None], (b, N_META, d))
    h = jnp.concatenate([jnp.zeros((b, n_pad, d), x.dtype), meta, x], axis=1)

    w_in_p, wkt = _pack_w_in(w_in)
    bias_a = _swa_bias(rel_bias_table)
    sinks = jnp.broadcast_to((swa_sinks.astype(F32) * LOG2E)[:, :, None, None], (depth, SWA_Q_HEADS, BLOCK, LANES))
    sinks = sinks.reshape(depth, SWA_Q_HEADS * BLOCK, LANES)
    fb = jnp.pad(fox_forget_bias.astype(F32), ((0, 0), (0, LANES - FOX_HEADS)))[:, None, :]
    wg = (0.5 * jnp.concatenate([_block_diag(lru_w_r), _block_diag(lru_w_i)], axis=-1)).astype(BF16)
    bg = 0.5 * jnp.concatenate([lru_b_r, lru_b_i], axis=-1).astype(F32)[:, None, :]
    wb_a = _to_slab_order(w_branch[:, 0], axis=1)
    wb = jnp.stack([wb_a, w_branch[:, 1], w_branch[:, 2]], axis=1).astype(BF16)
    wo = w_out.astype(BF16)
    w1 = w_ffn_in.astype(BF16)
    w2 = w_ffn_out.astype(BF16)

    for l in range(depth):
        qva, qvf, kt, fl, xy, gt = _in_proj(h.reshape(b * tp, d), norm_mix[l][None, :], w_in_p[l], wkt[l])
        ck = _fox_prefix(fl.reshape(b, tp, LANES), fb[l], n_pad)
        o_a = _swa(qva.reshape(b, tp, A_COLS), kt, bias_a, sinks[l], n_pad)
        o_f = _fox(qvf.reshape(b, tp, F_COLS), kt, ck, n_pad)
        o_c = _lru(xy.reshape(b, tp, C_COLS), conv_w[l], conv_b[l][None, :], wg[l], bg[l], lru_lambda[l][None, :],
                   n_pad)
        mix_args = (h, o_a, o_f, o_c, gt.reshape(b, tp, G_COLS), wb[l], wo[l], norm_ffn[l][None, :], w1[l], w2[l])
        if l + 1 < depth:
            h = _mix_ffn(*mix_args, n_pad)
    return _mix_ffn_last(*mix_args, norm_final[None, :], n_pad + N_META, seq)
```

```python
import functools
import math

import jax
import jax.numpy as jnp
from jax import lax
from jax.experimental import pallas as pl
from jax.experimental.pallas import tpu as pltpu

F32 = jnp.float32
BF16 = jnp.bfloat16

D_MODEL = 1024
HEAD_DIM = 64
N_META = 16
BLOCK = 128
LANES = 128
NEG_INF = -1e30
LOG2E = math.log2(math.e)
SWA_WINDOW = 128
SWA_Q_HEADS = 8
SWA_KV_HEADS = 2
FOX_HEADS = 8
LRU_WIDTH = D_MODEL // 2
LRU_BLOCKS = 8
CONV_WIDTH = 4
LRU_C = 8.0
REL_BUCKETS = 32
REL_MAX_DIST = 128
D_FF = 2816
N_BRANCH = 3
EPS = 1e-6

SWA_QCOLS = SWA_Q_HEADS * HEAD_DIM
A_COLS = SWA_QCOLS + SWA_KV_HEADS * HEAD_DIM
FOX_WIDTH = FOX_HEADS * HEAD_DIM
F_COLS = 2 * FOX_WIDTH
KT_ROWS = FOX_WIDTH + SWA_KV_HEADS * HEAD_DIM
C_COLS = 2 * LRU_WIDTH
G_COLS = N_BRANCH * D_MODEL
OFF_A = 0
OFF_FL = OFF_A + A_COLS
OFF_F = OFF_FL + LANES
OFF_C = OFF_F + F_COLS
OFF_G = OFF_C + C_COLS
IN_COLS_PACKED = OFF_G + G_COLS

VMEM_LIMIT = 56 * 1024 * 1024


def _sigmoid(x):
    return 0.5 * jnp.tanh(0.5 * x) + 0.5


def _log_sigmoid(x):
    return jnp.minimum(x, 0.0) - jnp.log1p(jnp.exp(-jnp.abs(x)))


def _gelu_tanh(x):
    c = math.sqrt(2.0 / math.pi)
    return 0.5 * x * (1.0 + jnp.tanh(c * (x + 0.044715 * (x * x * x))))


def _rms_norm(x, g):
    ms = jnp.mean(x * x, axis=-1, keepdims=True)
    return x * lax.rsqrt(ms + EPS) * g


def _dot(a, b):
    return jnp.dot(a, b, preferred_element_type=F32)


def _dot_nt(a, b):
    return lax.dot_general(a, b, (((1,), (1,)), ((), ())), preferred_element_type=F32)


def _resident(shape):
    nd = len(shape)
    return pl.BlockSpec(shape, lambda *_: (0,) * nd, pipeline_mode=pl.Buffered(1))


def _row_tile(tp):
    return 384 if tp % 384 == 0 else BLOCK


def _pv_operand(v, lo_k):
    zero = jnp.zeros_like(v)
    ind_lo = jnp.where(lo_k, 1.0, 0.0).astype(BF16)
    ind_hi = jnp.where(lo_k, 0.0, 1.0).astype(BF16)
    top = jnp.concatenate([jnp.where(lo_k, v, zero), ind_lo], axis=1)
    bot = jnp.concatenate([jnp.where(lo_k, zero, v), ind_hi], axis=1)
    return jnp.concatenate([top, bot], axis=0)


def _in_proj_kernel(x_ref, g_ref, w_ref, wkt_ref, qa_ref, qf_ref, kt_ref, fl_ref, xy_ref, gt_ref):
    rows = x_ref.shape[0]
    group = rows // 4 if rows % 64 == 0 else rows
    parts = []
    for r0 in range(0, rows, group):
        part = _rms_norm(x_ref[r0:r0 + group, :], g_ref[...]).astype(BF16)
        qa_ref[r0:r0 + group, 0:SWA_QCOLS] = _dot(part, w_ref[:, OFF_A:OFF_A + SWA_QCOLS]).astype(BF16)
        parts.append(part)
    u = jnp.concatenate(parts, axis=0)

    def proj(c0, width):
        return _dot(u, w_ref[:, c0:c0 + width])

    v_fl = proj(OFF_A + SWA_QCOLS, 2 * LANES)
    qa_ref[:, SWA_QCOLS:] = v_fl[:, 0:LANES].astype(BF16)
    fl_ref[...] = v_fl[:, LANES:]
    for c in range(0, F_COLS, 512):
        qf_ref[:, c:c + 512] = proj(OFF_F + c, 512).astype(BF16)
    kt_ref[...] = _dot_nt(wkt_ref[...], u).astype(BF16)
    xy_ref[:, 0:LRU_WIDTH] = proj(OFF_C, LRU_WIDTH).astype(BF16)
    xy_ref[:, LRU_WIDTH:] = _gelu_tanh(proj(OFF_C + LRU_WIDTH, LRU_WIDTH)).astype(BF16)
    for c in range(0, G_COLS, 512):
        gt_ref[:, c:c + 512] = _sigmoid(proj(OFF_G + c, 512)).astype(BF16)


def _in_proj(h2, g, w, wkt):
    rows = h2.shape[0]
    tm = next(t for t in (768, 384, BLOCK) if rows % t == 0)
    row_spec = lambda cols: pl.BlockSpec((tm, cols), lambda i: (i, 0))
    return pl.pallas_call(
        _in_proj_kernel,
        grid=(rows // tm,),
        in_specs=[row_spec(D_MODEL), _resident((1, D_MODEL)), _resident((D_MODEL, IN_COLS_PACKED)),
                  _resident((KT_ROWS, D_MODEL))],
        out_specs=[row_spec(A_COLS), row_spec(F_COLS), pl.BlockSpec((KT_ROWS, tm), lambda i: (0, i)),
                   row_spec(LANES), row_spec(C_COLS), row_spec(G_COLS)],
        out_shape=[
            jax.ShapeDtypeStruct((rows, A_COLS), BF16),
            jax.ShapeDtypeStruct((rows, F_COLS), BF16),
            jax.ShapeDtypeStruct((KT_ROWS, rows), BF16),
            jax.ShapeDtypeStruct((rows, LANES), F32),
            jax.ShapeDtypeStruct((rows, C_COLS), BF16),
            jax.ShapeDtypeStruct((rows, G_COLS), BF16),
        ],
        compiler_params=pltpu.CompilerParams(dimension_semantics=("parallel",), vmem_limit_bytes=VMEM_LIMIT),
        name="in_proj",
    )(h2, g, w, wkt)


def _fox_prefix_kernel(fl_ref, fb_ref, ck_ref, cum_ref, *, tp, n_pad):
    lf = _log_sigmoid(fl_ref[0] + fb_ref[...])
    row = lax.broadcasted_iota(jnp.int32, (tp, 1), 0)
    lf = jnp.where(row >= n_pad, lf, 0.0)
    r = lax.broadcasted_iota(jnp.int32, (BLOCK, BLOCK), 0)
    c = lax.broadcasted_iota(jnp.int32, (BLOCK, BLOCK), 1)
    tri = jnp.where(r >= c, 1.0, 0.0).astype(F32)
    carry = jnp.zeros((1, LANES), F32)
    for blk in range(tp // BLOCK):
        sl = slice(blk * BLOCK, (blk + 1) * BLOCK)
        cum = jnp.dot(tri, lf[sl], preferred_element_type=F32, precision=lax.Precision.HIGHEST) + carry
        cum_ref[sl, :] = cum
        carry = cum[BLOCK - 1:BLOCK, :]
    ck_ref[0] = cum_ref[...].T[0:FOX_HEADS, :]


def _fox_prefix(fl3, fb, n_pad):
    b, tp, _ = fl3.shape
    return pl.pallas_call(
        functools.partial(_fox_prefix_kernel, tp=tp, n_pad=n_pad),
        grid=(b,),
        in_specs=[pl.BlockSpec((1, tp, LANES), lambda i: (i, 0, 0)), _resident((1, LANES))],
        out_specs=pl.BlockSpec((1, FOX_HEADS, tp), lambda i: (i, 0, 0)),
        out_shape=jax.ShapeDtypeStruct((b, FOX_HEADS, tp), F32),
        scratch_shapes=[pltpu.VMEM((tp, LANES), F32)],
        compiler_params=pltpu.CompilerParams(dimension_semantics=("parallel",), vmem_limit_bytes=VMEM_LIMIT),
        name="fox_prefix",
    )(fl3, fb)


def _swa_kernel(q_ref, v_ref, vp_ref, kt_ref, ktp_ref, bias_ref, sink_ref, o_ref, *, blocks, n_pad):
    ib = pl.program_id(1)
    half = SWA_Q_HEADS // 2
    rows = half * BLOCK
    lo = lax.broadcasted_iota(jnp.int32, (BLOCK, LANES), 1) < HEAD_DIM
    lo_k = lax.broadcasted_iota(jnp.int32, (2 * BLOCK, LANES), 1) < HEAD_DIM
    lo_o = lax.broadcasted_iota(jnp.int32, (rows, LANES), 1) < HEAD_DIM
    kidx = lax.broadcasted_iota(jnp.int32, (1, BLOCK), 1)
    sink = sink_ref[...]
    for g in range(blocks):
        cur = slice(g * BLOCK, (g + 1) * BLOCK)
        if g == 0:
            kt_prev, v_prev = ktp_ref[...], vp_ref[0]
        else:
            prev = slice((g - 1) * BLOCK, g * BLOCK)
            kt_prev, v_prev = kt_ref[:, prev], v_ref[0, prev, :]
        kt_band = jnp.concatenate([kt_prev, kt_ref[:, cur]], axis=1)
        v_band = jnp.concatenate([v_prev, v_ref[0, cur, :]], axis=0)
        slabs = [q_ref[0, cur, j * LANES:(j + 1) * LANES] for j in range(half)]
        zero = jnp.zeros_like(slabs[0])
        q_all = jnp.concatenate([jnp.where(lo, s, zero) for s in slabs] + [jnp.where(lo, zero, s) for s in slabs],
                                axis=0)
        s = _dot(q_all, kt_band) + bias_ref[...]
        key0 = (ib * blocks + g - 1) * BLOCK + kidx
        s_prev = jnp.where(key0 >= n_pad, s[:, 0:BLOCK], NEG_INF)
        s_cur = jnp.where(key0 + BLOCK >= n_pad, s[:, BLOCK:], NEG_INF)
        m = jnp.maximum(jnp.max(jnp.maximum(s_prev, s_cur), axis=-1, keepdims=True), sink)
        p = jnp.concatenate([jnp.exp2(s_prev - m), jnp.exp2(s_cur - m)], axis=1).astype(BF16)
        pv = _dot(jnp.concatenate([p[0:rows], p[rows:]], axis=1), _pv_operand(v_band, lo_k))
        esink = jnp.exp2(sink - m)
        den = pv[:, LANES:] + jnp.where(lo_o, esink[0:rows], esink[rows:])
        o = (pv[:, 0:LANES] / den).astype(BF16)
        for j in range(half):
            o_ref[0, cur, j * LANES:(j + 1) * LANES] = o[j * BLOCK:(j + 1) * BLOCK]


def _swa(qv3, kt, bias, sinks, n_pad):
    b, tp, _ = qv3.shape
    nb = tp // BLOCK
    blocks = next(n for n in (33, 11, 3, 1) if nb % n == 0)
    rows = blocks * BLOCK
    steps = nb // blocks
    vblk = SWA_QCOLS // LANES
    krow = FOX_WIDTH // LANES
    return pl.pallas_call(
        functools.partial(_swa_kernel, blocks=blocks, n_pad=n_pad),
        grid=(b, steps),
        in_specs=[
            pl.BlockSpec((1, rows, SWA_QCOLS), lambda i, j: (i, j, 0)),
            pl.BlockSpec((1, rows, LANES), lambda i, j: (i, j, vblk)),
            pl.BlockSpec((1, BLOCK, LANES), lambda i, j: (i, jnp.maximum(j * blocks - 1, 0), vblk)),
            pl.BlockSpec((LANES, rows), lambda i, j: (krow, i * steps + j)),
            pl.BlockSpec((LANES, BLOCK), lambda i, j: (krow, jnp.maximum((i * steps + j) * blocks - 1, 0))),
            _resident((SWA_Q_HEADS * BLOCK, 2 * BLOCK)),
            _resident((SWA_Q_HEADS * BLOCK, LANES)),
        ],
        out_specs=pl.BlockSpec((1, rows, SWA_QCOLS), lambda i, j: (i, j, 0)),
        out_shape=jax.ShapeDtypeStruct((b, tp, SWA_QCOLS), BF16),
        compiler_params=pltpu.CompilerParams(dimension_semantics=("parallel", "parallel"), vmem_limit_bytes=VMEM_LIMIT),
        name="swa",
    )(qv3, qv3, qv3, kt, kt, bias, sinks)


FOX_TQ = 384
FOX_TK = 512


def _fox_kernel(q_ref, v_ref, kt_ref, ck_ref, o_ref, qa_s, m_s, acc_s, *, tq, tk, n_tiles, tail, n_pad):
    i = pl.program_id(1)
    pairs = FOX_HEADS // 2
    lane = lax.broadcasted_iota(jnp.int32, (tq, LANES), 1)
    lo = lane < HEAD_DIM
    for hp in range(pairs):
        q = q_ref[0, :, hp * LANES:(hp + 1) * LANES]
        zero = jnp.zeros_like(q)
        qa_s[hp, 0:tq, :] = jnp.where(lo, q, zero)
        qa_s[hp, tq:, :] = jnp.where(lo, zero, q)
    m_s[...] = jnp.full(m_s.shape, NEG_INF, F32)
    acc_s[...] = jnp.zeros(acc_s.shape, F32)

    def tile(start, size, masked):
        key_ok = start + lax.broadcasted_iota(jnp.int32, (1, size), 1) >= n_pad
        lo_k = lax.broadcasted_iota(jnp.int32, (size, LANES), 1) < HEAD_DIM
        if masked:
            rel = lax.broadcasted_iota(jnp.int32, (tq, size), 1) - lax.broadcasted_iota(jnp.int32, (tq, size), 0)
            causal = jnp.where(rel <= i * tq - start, 0.0, NEG_INF)
        for hp in range(pairs):
            kt = kt_ref[hp * LANES:(hp + 1) * LANES, pl.ds(start, size)]
            s2 = _dot(qa_s[hp], kt)
            ps, alphas = [], []
            for e in range(2):
                c = jnp.where(key_ok, ck_ref[0, 2 * hp + e:2 * hp + e + 1, pl.ds(start, size)] * (-LOG2E), NEG_INF)
                s = s2[e * tq:(e + 1) * tq] + c
                if masked:
                    s = s + causal
                chunks = [s[:, c0:c0 + LANES] for c0 in range(0, size, LANES)]
                m_prev = m_s[2 * hp + e]
                m_new = jnp.maximum(m_prev, jnp.max(functools.reduce(jnp.maximum, chunks), axis=-1, keepdims=True))
                alphas.append(jnp.exp2(m_prev - m_new))
                ps.append(jnp.concatenate([jnp.exp2(ch - m_new) for ch in chunks], axis=1).astype(BF16))
                m_s[2 * hp + e] = m_new
            alpha = jnp.where(lo, alphas[0], alphas[1])
            alpha = jnp.concatenate([alpha, alpha], axis=1)
            v = v_ref[0, pl.ds(start, size), hp * LANES:(hp + 1) * LANES]
            acc_s[hp] = alpha * acc_s[hp] + _dot(jnp.concatenate(ps, axis=1), _pv_operand(v, lo_k))

    n_last = jnp.minimum((tq * (i + 1) - 1) // tk, n_tiles - 1)
    n_free = (tq * i + 1) // tk

    def full_tile(j, masked):
        tile(pl.multiple_of(j * tk, tk), tk, masked)

    def four_free(j4, carry):
        for u in range(4):
            full_tile(4 * j4 + u, False)
        return carry

    def one_masked(j, carry):
        full_tile(j, True)
        return carry

    def two_free(j):
        full_tile(j, False)
        full_tile(j + 1, False)

    lax.fori_loop(0, n_free // 4, four_free, 0)
    pl.when(n_free % 4 >= 2)(lambda: two_free(n_free - n_free % 4))
    pl.when(n_free % 2 == 1)(lambda: full_tile(n_free - 1, False))
    lax.fori_loop(n_free, n_last, one_masked, 0)
    last_start = pl.multiple_of(n_last * tk, tk)
    reach = jnp.minimum(tq * (i + 1) - n_last * tk, tk)
    for size in range(LANES, tk + 1, LANES):
        pl.when(reach == size)(functools.partial(tile, last_start, size, True))
    if tail:
        pl.when(tq * (i + 1) > n_tiles * tk)(lambda: tile(n_tiles * tk, tail, True))

    for hp in range(pairs):
        acc = acc_s[hp]
        o_ref[0, :, hp * LANES:(hp + 1) * LANES] = (acc[:, 0:LANES] / acc[:, LANES:]).astype(BF16)


def _fox(qv3, kt, ck, n_pad):
    b, tp, _ = qv3.shape
    tq, tk = FOX_TQ, FOX_TK
    n_tiles, tail = tp // tk, tp % tk
    assert tp % tq == 0 and tail % LANES == 0 and tail <= tq and n_tiles >= 1
    pairs = FOX_HEADS // 2
    return pl.pallas_call(
        functools.partial(_fox_kernel, tq=tq, tk=tk, n_tiles=n_tiles, tail=tail, n_pad=n_pad),
        grid=(b, tp // tq),
        in_specs=[
            pl.BlockSpec((1, tq, FOX_WIDTH), lambda i, j: (i, j, 0)),
            pl.BlockSpec((1, tp, FOX_WIDTH), lambda i, j: (i, 0, 1)),
            pl.BlockSpec((FOX_WIDTH, tp), lambda i, j: (0, i)),
            pl.BlockSpec((1, FOX_HEADS, tp), lambda i, j: (i, 0, 0)),
        ],
        out_specs=pl.BlockSpec((1, tq, FOX_WIDTH), lambda i, j: (i, j, 0)),
        out_shape=jax.ShapeDtypeStruct((b, tp, FOX_WIDTH), BF16),
        scratch_shapes=[
            pltpu.VMEM((pairs, 2 * tq, LANES), BF16),
            pltpu.VMEM((FOX_HEADS, tq, LANES), F32),
            pltpu.VMEM((pairs, tq, 2 * LANES), F32),
        ],
        compiler_params=pltpu.CompilerParams(
            dimension_semantics=("parallel", "arbitrary"), vmem_limit_bytes=VMEM_LIMIT),
        name="fox",
    )(qv3, qv3, kt, ck)


SUBLANES = 8
SUBLANE_BITS = SUBLANES.bit_length() - 1


LRU_SEQS = 2


def _lru_kernel(xy_ref, perm_ref, unperm_ref, cw_ref, cb_ref, wg_ref, bg_ref, lam_ref, o_ref,
                xe_s, a_s, h_s, hist_s, carry, *, seqs, tt, n_pad):
    @pl.when(pl.program_id(1) == 0)
    def _():
        hist_s[...] = jnp.zeros(hist_s.shape, F32)
        carry[...] = jnp.zeros(carry.shape, F32)

    for q in range(seqs):
        _lru_one(xy_ref.at[q], perm_ref, unperm_ref, cw_ref, cb_ref, wg_ref, bg_ref, lam_ref, o_ref.at[q],
                 xe_s.at[q], a_s.at[q], h_s.at[q], hist_s.at[q], carry.at[q], tt=tt, n_pad=n_pad)


def _lru_one(xy_ref, perm_ref, unperm_ref, cw_ref, cb_ref, wg_ref, bg_ref, lam_ref, o_ref,
             xe_s, a_s, h_s, hist_s, carry, *, tt, n_pad):
    it = pl.program_id(1)
    seg = tt // SUBLANES
    taps = CONV_WIDTH - 1
    w = LRU_WIDTH
    sub = lax.broadcasted_iota(jnp.int32, (SUBLANES, w), 0)
    xyp = _dot(perm_ref[...], xy_ref[...])
    xp, gate = xyp[:, 0:w], xyp[:, w:]

    for v in range(taps):
        src = slice((seg - taps + v) * SUBLANES, (seg - taps + v + 1) * SUBLANES)
        xe_s[v * SUBLANES:(v + 1) * SUBLANES, :] = jnp.where(
            sub == 0, pltpu.roll(hist_s[v], 1, 0), pltpu.roll(xp[src], 1, 0))
        hist_s[v] = xp[src]
    xe_s[taps * SUBLANES:, :] = xp
    conv = xe_s[0:tt, :] * cw_ref[0:1, :]
    for i in range(1, CONV_WIDTH):
        conv = conv + xe_s[i * SUBLANES:i * SUBLANES + tt, :] * cw_ref[i:i + 1, :]
    conv = conv + cb_ref[...]

    t = jnp.tanh(_dot(conv.astype(BF16), wg_ref[...]) + bg_ref[...])
    k = (0.5 * LRU_C) * _log_sigmoid(lam_ref[...])
    log_a = k * t[:, 0:w] + k
    p_idx = lax.broadcasted_iota(jnp.int32, (tt, 1), 0)
    time = it * tt + (p_idx & (SUBLANES - 1)) * seg + (p_idx >> SUBLANE_BITS)
    a = jnp.exp(log_a)
    quarter_one_minus_a2 = (-0.25 * jnp.tanh(log_a)) * (a * a + 1.0)
    b = jnp.where(time >= n_pad, jnp.sqrt(quarter_one_minus_a2) * (t[:, w:] * conv + conv), 0.0)

    h_loc = jnp.zeros((SUBLANES, w), F32)
    prod = jnp.ones((SUBLANES, w), F32)
    for v in range(seg):
        rows = slice(v * SUBLANES, (v + 1) * SUBLANES)
        h_loc = a[rows] * h_loc + b[rows]
        prod = a[rows] * prod
        h_s[rows, :] = h_loc
        a_s[rows, :] = prod
    aa, bb = prod, h_loc
    for s in (1, 2, 4):
        keep = sub >= s
        bb = jnp.where(keep, aa * pltpu.roll(bb, s, 0) + bb, bb)
        aa = jnp.where(keep, aa * pltpu.roll(aa, s, 0), aa)
    h_end = aa * carry[...] + bb
    h_in = jnp.where(sub == 0, carry[...], pltpu.roll(h_end, 1, 0))
    carry[...] = h_end[SUBLANES - 1:SUBLANES, :]

    h_in_all = jnp.concatenate([h_in] * seg, axis=0)
    out = ((h_s[...] + a_s[...] * h_in_all) * gate).astype(BF16)
    o_ref[...] = _dot(unperm_ref[...], out).astype(BF16)


def _lru(xy3, cw, cb, wg, bg, lam, n_pad):
    b, tp, _ = xy3.shape
    tt = _row_tile(tp)
    w = LRU_WIDTH
    seg = tt // SUBLANES
    p_idx = jnp.arange(tt)
    perm = (jnp.arange(tt)[None, :] == ((p_idx % SUBLANES) * seg + p_idx // SUBLANES)[:, None]).astype(BF16)
    seqs = LRU_SEQS if b % LRU_SEQS == 0 else 1
    return pl.pallas_call(
        functools.partial(_lru_kernel, seqs=seqs, tt=tt, n_pad=n_pad),
        grid=(b // seqs, tp // tt),
        in_specs=[
            pl.BlockSpec((seqs, tt, C_COLS), lambda i, j: (i, j, 0)),
            _resident((tt, tt)), _resident((tt, tt)),
            _resident((CONV_WIDTH, w)), _resident((1, w)), _resident((w, 2 * w)), _resident((1, 2 * w)),
            _resident((1, w)),
        ],
        out_specs=pl.BlockSpec((seqs, tt, w), lambda i, j: (i, j, 0)),
        out_shape=jax.ShapeDtypeStruct((b, tp, w), BF16),
        scratch_shapes=[
            pltpu.VMEM((seqs, tt + (CONV_WIDTH - 1) * SUBLANES, w), F32), pltpu.VMEM((seqs, tt, w), F32),
            pltpu.VMEM((seqs, tt, w), F32), pltpu.VMEM((seqs, CONV_WIDTH - 1, SUBLANES, w), F32),
            pltpu.VMEM((seqs, 1, w), F32),
        ],
        compiler_params=pltpu.CompilerParams(dimension_semantics=("parallel", "arbitrary"), vmem_limit_bytes=VMEM_LIMIT),
        name="lru",
    )(xy3, perm, perm.T, cw, cb, wg, bg, lam)


FF_CHUNK = 256


def _mix_ffn_rows(h, oa, of, oc, gt_ref, wb_ref, wo_ref, gn_ref, w1_ref, w2_ref, act_s):
    d = D_MODEL
    merged = gt_ref[:, 0:d].astype(F32) * _dot(oa, wb_ref[0])
    merged = merged + gt_ref[:, d:2 * d].astype(F32) * _dot(of, wb_ref[1])
    merged = merged + gt_ref[:, 2 * d:3 * d].astype(F32) * _dot(oc, wb_ref[2])
    hm = h + _dot(merged.astype(BF16), wo_ref[...])
    u = _rms_norm(hm, gn_ref[...]).astype(BF16)
    for c in range(0, D_FF, FF_CHUNK):
        gate = _dot(u, w1_ref[:, c:c + FF_CHUNK])
        up = _dot(u, w1_ref[:, D_FF + c:D_FF + c + FF_CHUNK])
        act_s[:, c:c + FF_CHUNK] = (gate * _sigmoid(gate) * up).astype(BF16)
    return hm + _dot(act_s[...], w2_ref[...])


def _mix_ffn_kernel(h_ref, oa_ref, of_ref, oc_ref, gt_ref, wb_ref, wo_ref, gn_ref, w1_ref, w2_ref, out_ref, act_s,
                    *, tm, n_pad):
    j = pl.program_id(1)
    y = _mix_ffn_rows(h_ref[0], oa_ref[0], of_ref[0], oc_ref[0], gt_ref.at[0], wb_ref, wo_ref, gn_ref, w1_ref, w2_ref,
                      act_s)
    row = j * tm + lax.broadcasted_iota(jnp.int32, (tm, 1), 0)
    out_ref[0] = jnp.where(row >= n_pad, y, 0.0)


def _mix_ffn_last_kernel(h_ref, oa_ref, of_ref, oc_ref, gt_ref, wb_ref, wo_ref, gn_ref, w1_ref, w2_ref, gf_ref,
                         out_ref, act_s):
    y = _mix_ffn_rows(h_ref[...], oa_ref[...], of_ref[...], oc_ref[...], gt_ref, wb_ref, wo_ref, gn_ref, w1_ref,
                      w2_ref, act_s)
    out_ref[0] = _rms_norm(y, gf_ref[...])


def _mix_ffn_last(h3, oa, of, oc, gt3, wb, wo, gn, w1, w2, gf, skip_rows, seq):
    b, tp, d = h3.shape
    tm = 512 if seq % 512 == 0 else BLOCK
    w = LRU_WIDTH

    def rows(cols):
        return pl.BlockSpec((pl.Element(tm), pl.Element(cols)),
                            lambda i, j: (pl.multiple_of(i * tp + skip_rows + j * tm, BLOCK), 0))

    flat = lambda a: a.reshape(b * tp, a.shape[-1])
    return pl.pallas_call(
        _mix_ffn_last_kernel,
        grid=(b, seq // tm),
        in_specs=[
            rows(d), rows(w), rows(w), rows(w), rows(G_COLS),
            _resident((N_BRANCH, w, d)), _resident((d, d)), _resident((1, d)),
            _resident((d, 2 * D_FF)), _resident((D_FF, d)), _resident((1, d)),
        ],
        out_specs=pl.BlockSpec((1, tm, d), lambda i, j: (i, j, 0)),
        out_shape=jax.ShapeDtypeStruct((b, seq, d), F32),
        scratch_shapes=[pltpu.VMEM((tm, D_FF), BF16)],
        compiler_params=pltpu.CompilerParams(dimension_semantics=("parallel", "parallel"), vmem_limit_bytes=VMEM_LIMIT),
        name="mix_ffn_last",
    )(flat(h3), flat(oa), flat(of), flat(oc), flat(gt3), wb, wo, gn, w1, w2, gf)


def _mix_ffn(h3, oa, of, oc, gt3, wb, wo, gn, w1, w2, n_pad):
    b, tp, d = h3.shape
    tm = _row_tile(tp)
    w = LRU_WIDTH
    tile = lambda cols: pl.BlockSpec((1, tm, cols), lambda i, j: (i, j, 0))
    return pl.pallas_call(
        functools.partial(_mix_ffn_kernel, tm=tm, n_pad=n_pad),
        grid=(b, tp // tm),
        in_specs=[
            tile(d), tile(w), tile(w), tile(w), tile(G_COLS),
            _resident((N_BRANCH, w, d)), _resident((d, d)), _resident((1, d)),
            _resident((d, 2 * D_FF)), _resident((D_FF, d)),
        ],
        out_specs=tile(d),
        out_shape=jax.ShapeDtypeStruct((b, tp, d), F32),
        scratch_shapes=[pltpu.VMEM((tm, D_FF), BF16)],
        compiler_params=pltpu.CompilerParams(dimension_semantics=("parallel", "parallel"), vmem_limit_bytes=VMEM_LIMIT),
        name="mix_ffn",
    )(h3, oa, of, oc, gt3, wb, wo, gn, w1, w2)


def _t5_bucket(dist):
    max_exact = REL_BUCKETS // 2
    d = jnp.maximum(dist, 0)
    scaled = jnp.log(jnp.maximum(d, 1).astype(F32) / max_exact) / math.log(REL_MAX_DIST / max_exact)
    large = jnp.minimum(max_exact + (scaled * (REL_BUCKETS - max_exact)).astype(jnp.int32), REL_BUCKETS - 1)
    return jnp.where(d < max_exact, d, large)


def _swa_bias_kernel(bucket_ref, table_ref, o_ref):
    bucket = bucket_ref[...]
    for h in range(SWA_Q_HEADS):
        acc = jnp.full(bucket.shape, NEG_INF, F32)
        for c in range(REL_BUCKETS):
            acc = jnp.where(bucket == c, table_ref[c, h] * LOG2E, acc)
        o_ref[h * BLOCK:(h + 1) * BLOCK, :] = acc


def _swa_bias(rel_table):
    q_idx = jnp.arange(BLOCK)[:, None]
    k_idx = jnp.arange(2 * BLOCK)[None, :]
    dist = q_idx + BLOCK - k_idx
    bucket = jnp.where((dist >= 0) & (dist < SWA_WINDOW), _t5_bucket(dist), -1).astype(jnp.int32)
    return pl.pallas_call(
        _swa_bias_kernel,
        in_specs=[pl.BlockSpec(memory_space=pltpu.VMEM), pl.BlockSpec(memory_space=pltpu.SMEM)],
        out_specs=pl.BlockSpec(memory_space=pltpu.VMEM),
        out_shape=jax.ShapeDtypeStruct((SWA_Q_HEADS * BLOCK, 2 * BLOCK), F32),
        name="swa_bias",
    )(bucket, rel_table.astype(F32))


def _to_slab_order(a, axis):
    half = SWA_Q_HEADS // 2
    shape = a.shape
    a = a.reshape(shape[:axis] + (2, half, HEAD_DIM) + shape[axis + 1:])
    return jnp.swapaxes(a, axis, axis + 1).reshape(shape)


def _pack_w_in(w_in):
    sizes = (512, 128, 128, 512, 512, 512, 8, 512, 512, 3072)
    offs = [0]
    for s in sizes:
        offs.append(offs[-1] + s)
    qa, ka, va, qf, kf, vf, fl, xc, yc, gates = [w_in[:, :, offs[i]:offs[i + 1]] for i in range(len(sizes))]
    scale = HEAD_DIM ** -0.5 * LOG2E
    qa = _to_slab_order(qa, axis=2) * scale
    fl = jnp.pad(fl, ((0, 0), (0, 0), (0, LANES - FOX_HEADS)))
    packed = jnp.concatenate([qa, va, fl, qf * scale, vf, xc, yc, gates], axis=-1)
    keys_t = jnp.swapaxes(jnp.concatenate([kf, ka], axis=-1), 1, 2)
    return packed.astype(BF16), keys_t.astype(BF16)


def _block_diag(w):
    depth, nb, n, _ = w.shape
    eye = jnp.eye(nb, dtype=w.dtype)
    return jnp.einsum('lhij,hk->lhikj', w, eye).reshape(depth, nb * n, nb * n)


def kernel(x, meta_tokens, rel_bias_table, norm_mix, w_in, swa_sinks, fox_forget_bias, conv_w, conv_b,
           lru_w_r, lru_b_r, lru_w_i, lru_b_i, lru_lambda, w_branch, w_out, norm_ffn, w_ffn_in, w_ffn_out,
           norm_final):
    b, seq, d = x.shape
    depth = w_in.shape[0]
    t = N_META + seq
    n_pad = (-t) % BLOCK
    tp = t + n_pad
    assert d == D_MODEL and (n_pad + N_META) % BLOCK == 0

    meta = jnp.broadcast_to(meta_tokens.astype(x.dtype)[None], (b, N_META, d))
    h = jnp.concatenate([jnp.zeros((b, n_pad, d), x.dtype), meta, x], axis=1)

    w_in_p, wkt = _pack_w_in(w_in)
    bias_a = _swa_bias(rel_bias_table)
    sinks = jnp.broadcast_to((swa_sinks.astype(F32) * LOG2E)[:, :, None, None], (depth, SWA_Q_HEADS, BLOCK, LANES))
    sinks = sinks.reshape(depth, SWA_Q_HEADS * BLOCK, LANES)
    fb = jnp.pad(fox_forget_bias.astype(F32), ((0, 0), (0, LANES - FOX_HEADS)))[:, None, :]
    wg = (0.5 * jnp.concatenate([_block_diag(lru_w_r), _block_diag(lru_w_i)], axis=-1)).astype(BF16)
    bg = 0.5 * jnp.concatenate([lru_b_r, lru_b_i], axis=-1).astype(F32)[:, None, :]
    wb_a = _to_slab_order(w_branch[:, 0], axis=1)
    wb = jnp.stack([wb_a, w_branch[:, 1], w_branch[:, 2]], axis=1).astype(BF16)
    wo = w_out.astype(BF16)
    w1 = w_ffn_in.astype(BF16)
    w2 = w_ffn_out.astype(BF16)

    for l in range(depth):
        qva, qvf, kt, fl, xy, gt = _in_proj(h.reshape(b * tp, d), norm_mix[l][None, :], w_in_p[l], wkt[l])
        ck = _fox_prefix(fl.reshape(b, tp, LANES), fb[l], n_pad)
        o_a = _swa(qva.reshape(b, tp, A_COLS), kt, bias_a, sinks[l], n_pad)
        o_f = _fox(qvf.reshape(b, tp, F_COLS), kt, ck, n_pad)
        o_c = _lru(xy.reshape(b, tp, C_COLS), conv_w[l], conv_b[l][None, :], wg[l], bg[l], lru_lambda[l][None, :],
                   n_pad)
        mix_args = (h, o_a, o_f, o_c, gt.reshape(b, tp, G_COLS), wb[l], wo[l], norm_ffn[l][None, :], w1[l], w2[l])
        if l + 1 < depth:
            h = _mix_ffn(*mix_args, n_pad)
    return _mix_ffn_last(*mix_args, norm_final[None, :], n_pad + N_META, seq)
```

```python
import functools
import math

import jax
import jax.numpy as jnp
from jax import lax
from jax.experimental import pallas as pl
from jax.experimental.pallas import tpu as pltpu

F32 = jnp.float32
BF16 = jnp.bfloat16

D_MODEL = 1024
HEAD_DIM = 64
N_META = 16
BLOCK = 128
LANES = 128
NEG_INF = -1e30
LOG2E = math.log2(math.e)
SWA_WINDOW = 128
SWA_Q_HEADS = 8
SWA_KV_HEADS = 2
FOX_HEADS = 8
LRU_WIDTH = D_MODEL // 2
LRU_BLOCKS = 8
CONV_WIDTH = 4
LRU_C = 8.0
REL_BUCKETS = 32
REL_MAX_DIST = 128
D_FF = 2816
N_BRANCH = 3
EPS = 1e-6

SWA_QCOLS = SWA_Q_HEADS * HEAD_DIM
A_COLS = SWA_QCOLS + SWA_KV_HEADS * HEAD_DIM
FOX_WIDTH = FOX_HEADS * HEAD_DIM
F_COLS = 2 * FOX_WIDTH
KT_ROWS = FOX_WIDTH + SWA_KV_HEADS * HEAD_DIM
C_COLS = 2 * LRU_WIDTH
G_COLS = N_BRANCH * D_MODEL
OFF_A = 0
OFF_FL = OFF_A + A_COLS
OFF_F = OFF_FL + LANES
OFF_C = OFF_F + F_COLS
OFF_G = OFF_C + C_COLS
IN_COLS_PACKED = OFF_G + G_COLS

VMEM_LIMIT = 56 * 1024 * 1024


def _sigmoid(x):
    return 0.5 * jnp.tanh(0.5 * x) + 0.5


def _log_sigmoid(x):
    return jnp.minimum(x, 0.0) - jnp.log1p(jnp.exp(-jnp.abs(x)))


def _gelu_tanh(x):
    c = math.sqrt(2.0 / math.pi)
    return 0.5 * x * (1.0 + jnp.tanh(c * (x + 0.044715 * (x * x * x))))


def _rms_norm(x, g):
    ms = jnp.mean(x * x, axis=-1, keepdims=True)
    return x * lax.rsqrt(ms + EPS) * g


def _dot(a, b):
    return jnp.dot(a, b, preferred_element_type=F32)


def _dot_nt(a, b):
    return lax.dot_general(a, b, (((1,), (1,)), ((), ())), preferred_element_type=F32)


def _resident(shape):
    nd = len(shape)
    return pl.BlockSpec(shape, lambda *_: (0,) * nd, pipeline_mode=pl.Buffered(1))


def _row_tile(tp):
    return 384 if tp % 384 == 0 else BLOCK


def _pv_operand(v, lo_k):
    zero = jnp.zeros_like(v)
    ind_lo = jnp.where(lo_k, 1.0, 0.0).astype(BF16)
    ind_hi = jnp.where(lo_k, 0.0, 1.0).astype(BF16)
    top = jnp.concatenate([jnp.where(lo_k, v, zero), ind_lo], axis=1)
    bot = jnp.concatenate([jnp.where(lo_k, zero, v), ind_hi], axis=1)
    return jnp.concatenate([top, bot], axis=0)


def _in_proj_kernel(x_ref, g_ref, w_ref, wkt_ref, qa_ref, qf_ref, kt_ref, fl_ref, xy_ref, gt_ref):
    rows = x_ref.shape[0]
    group = rows // 4 if rows % 64 == 0 else rows
    parts = []
    for r0 in range(0, rows, group):
        part = _rms_norm(x_ref[r0:r0 + group, :], g_ref[...]).astype(BF16)
        qa_ref[r0:r0 + group, 0:SWA_QCOLS] = _dot(part, w_ref[:, OFF_A:OFF_A + SWA_QCOLS]).astype(BF16)
        parts.append(part)
    u = jnp.concatenate(parts, axis=0)

    def proj(c0, width):
        return _dot(u, w_ref[:, c0:c0 + width])

    v_fl = proj(OFF_A + SWA_QCOLS, 2 * LANES)
    qa_ref[:, SWA_QCOLS:] = v_fl[:, 0:LANES].astype(BF16)
    fl_ref[...] = v_fl[:, LANES:]
    for c in range(0, F_COLS, 512):
        qf_ref[:, c:c + 512] = proj(OFF_F + c, 512).astype(BF16)
    kt_ref[...] = _dot_nt(wkt_ref[...], u).astype(BF16)
    xy_ref[:, 0:LRU_WIDTH] = proj(OFF_C, LRU_WIDTH).astype(BF16)
    xy_ref[:, LRU_WIDTH:] = _gelu_tanh(proj(OFF_C + LRU_WIDTH, LRU_WIDTH)).astype(BF16)
    for c in range(0, G_COLS, 512):
        gt_ref[:, c:c + 512] = _sigmoid(proj(OFF_G + c, 512)).astype(BF16)


def _in_proj(h2, g, w, wkt):
    rows = h2.shape[0]
    tm = next(t for t in (768, 384, BLOCK) if rows % t == 0)
    row_spec = lambda cols: pl.BlockSpec((tm, cols), lambda i: (i, 0))
    return pl.pallas_call(
        _in_proj_kernel,
        grid=(rows // tm,),
        in_specs=[row_spec(D_MODEL), _resident((1, D_MODEL)), _resident((D_MODEL, IN_COLS_PACKED)),
                  _resident((KT_ROWS, D_MODEL))],
        out_specs=[row_spec(A_COLS), row_spec(F_COLS), pl.BlockSpec((KT_ROWS, tm), lambda i: (0, i)),
                   row_spec(LANES), row_spec(C_COLS), row_spec(G_COLS)],
        out_shape=[
            jax.ShapeDtypeStruct((rows, A_COLS), BF16),
            jax.ShapeDtypeStruct((rows, F_COLS), BF16),
            jax.ShapeDtypeStruct((KT_ROWS, rows), BF16),
            jax.ShapeDtypeStruct((rows, LANES), F32),
            jax.ShapeDtypeStruct((rows, C_COLS), BF16),
            jax.ShapeDtypeStruct((rows, G_COLS), BF16),
        ],
        compiler_params=pltpu.CompilerParams(dimension_semantics=("parallel",), vmem_limit_bytes=VMEM_LIMIT),
        name="in_proj",
    )(h2, g, w, wkt)


def _fox_prefix_kernel(fl_ref, fb_ref, ck_ref, cum_ref, *, tp, n_pad):
    lf = _log_sigmoid(fl_ref[0] + fb_ref[...])
    row = lax.broadcasted_iota(jnp.int32, (tp, 1), 0)
    lf = jnp.where(row >= n_pad, lf, 0.0)
    r = lax.broadcasted_iota(jnp.int32, (BLOCK, BLOCK), 0)
    c = lax.broadcasted_iota(jnp.int32, (BLOCK, BLOCK), 1)
    tri = jnp.where(r >= c, 1.0, 0.0).astype(F32)
    carry = jnp.zeros((1, LANES), F32)
    for blk in range(tp // BLOCK):
        sl = slice(blk * BLOCK, (blk + 1) * BLOCK)
        cum = jnp.dot(tri, lf[sl], preferred_element_type=F32, precision=lax.Precision.HIGHEST) + carry
        cum_ref[sl, :] = cum
        carry = cum[BLOCK - 1:BLOCK, :]
    ck_ref[0] = cum_ref[...].T[0:FOX_HEADS, :]


def _fox_prefix(fl3, fb, n_pad):
    b, tp, _ = fl3.shape
    return pl.pallas_call(
        functools.partial(_fox_prefix_kernel, tp=tp, n_pad=n_pad),
        grid=(b,),
        in_specs=[pl.BlockSpec((1, tp, LANES), lambda i: (i, 0, 0)), _resident((1, LANES))],
        out_specs=pl.BlockSpec((1, FOX_HEADS, tp), lambda i: (i, 0, 0)),
        out_shape=jax.ShapeDtypeStruct((b, FOX_HEADS, tp), F32),
        scratch_shapes=[pltpu.VMEM((tp, LANES), F32)],
        compiler_params=pltpu.CompilerParams(dimension_semantics=("parallel",), vmem_limit_bytes=VMEM_LIMIT),
        name="fox_prefix",
    )(fl3, fb)


def _swa_kernel(q_ref, v_ref, vp_ref, kt_ref, ktp_ref, bias_ref, sink_ref, o_ref, *, blocks, n_pad):
    ib = pl.program_id(1)
    half = SWA_Q_HEADS // 2
    rows = half * BLOCK
    lo = lax.broadcasted_iota(jnp.int32, (BLOCK, LANES), 1) < HEAD_DIM
    lo_k = lax.broadcasted_iota(jnp.int32, (2 * BLOCK, LANES), 1) < HEAD_DIM
    lo_o = lax.broadcasted_iota(jnp.int32, (rows, LANES), 1) < HEAD_DIM
    kidx = lax.broadcasted_iota(jnp.int32, (1, BLOCK), 1)
    sink = sink_ref[...]
    for g in range(blocks):
        cur = slice(g * BLOCK, (g + 1) * BLOCK)
        if g == 0:
            kt_prev, v_prev = ktp_ref[...], vp_ref[0]
        else:
            prev = slice((g - 1) * BLOCK, g * BLOCK)
            kt_prev, v_prev = kt_ref[:, prev], v_ref[0, prev, :]
        kt_band = jnp.concatenate([kt_prev, kt_ref[:, cur]], axis=1)
        v_band = jnp.concatenate([v_prev, v_ref[0, cur, :]], axis=0)
        slabs = [q_ref[0, cur, j * LANES:(j + 1) * LANES] for j in range(half)]
        zero = jnp.zeros_like(slabs[0])
        q_all = jnp.concatenate([jnp.where(lo, s, zero) for s in slabs] + [jnp.where(lo, zero, s) for s in slabs],
                                axis=0)
        s = _dot(q_all, kt_band) + bias_ref[...]
        key0 = (ib * blocks + g - 1) * BLOCK + kidx
        s_prev = jnp.where(key0 >= n_pad, s[:, 0:BLOCK], NEG_INF)
        s_cur = jnp.where(key0 + BLOCK >= n_pad, s[:, BLOCK:], NEG_INF)
        m = jnp.maximum(jnp.max(jnp.maximum(s_prev, s_cur), axis=-1, keepdims=True), sink)
        p = jnp.concatenate([jnp.exp2(s_prev - m), jnp.exp2(s_cur - m)], axis=1).astype(BF16)
        pv = _dot(jnp.concatenate([p[0:rows], p[rows:]], axis=1), _pv_operand(v_band, lo_k))
        esink = jnp.exp2(sink - m)
        den = pv[:, LANES:] + jnp.where(lo_o, esink[0:rows], esink[rows:])
        o = (pv[:, 0:LANES] / den).astype(BF16)
        for j in range(half):
            o_ref[0, cur, j * LANES:(j + 1) * LANES] = o[j * BLOCK:(j + 1) * BLOCK]


def _swa(qv3, kt, bias, sinks, n_pad):
    b, tp, _ = qv3.shape
    nb = tp // BLOCK
    blocks = next(n for n in (33, 11, 3, 1) if nb % n == 0)
    rows = blocks * BLOCK
    steps = nb // blocks
    vblk = SWA_QCOLS // LANES
    krow = FOX_WIDTH // LANES
    return pl.pallas_call(
        functools.partial(_swa_kernel, blocks=blocks, n_pad=n_pad),
        grid=(b, steps),
        in_specs=[
            pl.BlockSpec((1, rows, SWA_QCOLS), lambda i, j: (i, j, 0)),
            pl.BlockSpec((1, rows, LANES), lambda i, j: (i, j, vblk)),
            pl.BlockSpec((1, BLOCK, LANES), lambda i, j: (i, jnp.maximum(j * blocks - 1, 0), vblk)),
            pl.BlockSpec((LANES, rows), lambda i, j: (krow, i * steps + j)),
            pl.BlockSpec((LANES, BLOCK), lambda i, j: (krow, jnp.maximum((i * steps + j) * blocks - 1, 0))),
            _resident((SWA_Q_HEADS * BLOCK, 2 * BLOCK)),
            _resident((SWA_Q_HEADS * BLOCK, LANES)),
        ],
        out_specs=pl.BlockSpec((1, rows, SWA_QCOLS), lambda i, j: (i, j, 0)),
        out_shape=jax.ShapeDtypeStruct((b, tp, SWA_QCOLS), BF16),
        compiler_params=pltpu.CompilerParams(dimension_semantics=("parallel", "parallel"), vmem_limit_bytes=VMEM_LIMIT),
        name="swa",
    )(qv3, qv3, qv3, kt, kt, bias, sinks)


FOX_TQ = 384
FOX_TK = 512
FOX_PAIRS_PER_STEP = 4


def _fox_kernel(q_ref, v_ref, kt_ref, ck_ref, o_ref, qa_s, m_s, acc_s, *, tq, tk, n_tiles, tail, n_pad):
    i = pl.program_id(2)
    pairs = qa_s.shape[0]
    lane = lax.broadcasted_iota(jnp.int32, (tq, LANES), 1)
    lo = lane < HEAD_DIM
    for hp in range(pairs):
        q = q_ref[0, :, hp * LANES:(hp + 1) * LANES]
        zero = jnp.zeros_like(q)
        qa_s[hp, 0:tq, :] = jnp.where(lo, q, zero)
        qa_s[hp, tq:, :] = jnp.where(lo, zero, q)
    m_s[...] = jnp.full(m_s.shape, NEG_INF, F32)
    acc_s[...] = jnp.zeros(acc_s.shape, F32)

    def tile(start, size, masked, ahead=False):
        key_ok = start + lax.broadcasted_iota(jnp.int32, (1, size), 1) >= n_pad
        lo_k = lax.broadcasted_iota(jnp.int32, (size, LANES), 1) < HEAD_DIM
        if masked:
            rel = lax.broadcasted_iota(jnp.int32, (tq, size), 1) - lax.broadcasted_iota(jnp.int32, (tq, size), 0)
            causal = jnp.where(rel <= i * tq - start, 0.0, NEG_INF)

        def qk(hp):
            return _dot(qa_s[hp], kt_ref[hp * LANES:(hp + 1) * LANES, pl.ds(start, size)])

        s2_next = qk(0) if ahead else None
        for hp in range(pairs):
            if ahead:
                s2 = s2_next
                if hp + 1 < pairs:
                    s2_next = qk(hp + 1)
            else:
                s2 = qk(hp)
            ps, alphas = [], []
            for e in range(2):
                c = jnp.where(key_ok, ck_ref[0, hp, e:e + 1, pl.ds(start, size)] * (-LOG2E), NEG_INF)
                s = s2[e * tq:(e + 1) * tq] + c
                if masked:
                    s = s + causal
                chunks = [s[:, c0:c0 + LANES] for c0 in range(0, size, LANES)]
                m_prev = m_s[2 * hp + e]
                m_new = jnp.maximum(m_prev, jnp.max(functools.reduce(jnp.maximum, chunks), axis=-1, keepdims=True))
                alphas.append(jnp.exp2(m_prev - m_new))
                ps.append(jnp.concatenate([jnp.exp2(ch - m_new) for ch in chunks], axis=1).astype(BF16))
                m_s[2 * hp + e] = m_new
            alpha = jnp.where(lo, alphas[0], alphas[1])
            alpha = jnp.concatenate([alpha, alpha], axis=1)
            v = v_ref[0, pl.ds(start, size), hp * LANES:(hp + 1) * LANES]
            acc_s[hp] = alpha * acc_s[hp] + _dot(jnp.concatenate(ps, axis=1), _pv_operand(v, lo_k))

    n_last = jnp.minimum((tq * (i + 1) - 1) // tk, n_tiles - 1)
    n_free = (tq * i + 1) // tk

    def full_tile(j, masked, ahead=False):
        tile(pl.multiple_of(j * tk, tk), tk, masked, ahead)

    def four_free(j4, carry):
        for u in range(4):
            full_tile(4 * j4 + u, False)
        return carry

    def one_masked(j, carry):
        full_tile(j, True, ahead=True)
        return carry

    def two_free(j):
        full_tile(j, False)
        full_tile(j + 1, False)

    lax.fori_loop(0, n_free // 4, four_free, 0)
    pl.when(n_free % 4 >= 2)(lambda: two_free(n_free - n_free % 4))
    pl.when(n_free % 2 == 1)(lambda: full_tile(n_free - 1, False, ahead=True))
    lax.fori_loop(n_free, n_last, one_masked, 0)
    last_start = pl.multiple_of(n_last * tk, tk)
    reach = jnp.minimum(tq * (i + 1) - n_last * tk, tk)
    for size in range(LANES, tk + 1, LANES):
        pl.when(reach == size)(functools.partial(tile, last_start, size, True, size == tk))
    if tail:
        pl.when(tq * (i + 1) > n_tiles * tk)(lambda: tile(n_tiles * tk, tail, True))

    for hp in range(pairs):
        acc = acc_s[hp]
        o_ref[0, :, hp * LANES:(hp + 1) * LANES] = (acc[:, 0:LANES] / acc[:, LANES:]).astype(BF16)


def _fox(qv3, kt, ck, n_pad):
    b, tp, _ = qv3.shape
    tq, tk = FOX_TQ, FOX_TK
    n_tiles, tail = tp // tk, tp % tk
    assert tp % tq == 0 and tail % LANES == 0 and tail <= tq and n_tiles >= 1
    pairs = FOX_PAIRS_PER_STEP
    groups = FOX_HEADS // 2 // pairs
    cols = pairs * LANES
    return pl.pallas_call(
        functools.partial(_fox_kernel, tq=tq, tk=tk, n_tiles=n_tiles, tail=tail, n_pad=n_pad),
        grid=(b, groups, tp // tq),
        in_specs=[
            pl.BlockSpec((1, tq, cols), lambda i, g, j: (i, j, g)),
            pl.BlockSpec((1, tp, cols), lambda i, g, j: (i, 0, groups + g)),
            pl.BlockSpec((cols, tp), lambda i, g, j: (g, i)),
            pl.BlockSpec((1, pairs, 2, tp), lambda i, g, j: (i, g, 0, 0)),
        ],
        out_specs=pl.BlockSpec((1, tq, cols), lambda i, g, j: (i, j, g)),
        out_shape=jax.ShapeDtypeStruct((b, tp, FOX_WIDTH), BF16),
        scratch_shapes=[
            pltpu.VMEM((pairs, 2 * tq, LANES), BF16),
            pltpu.VMEM((2 * pairs, tq, LANES), F32),
            pltpu.VMEM((pairs, tq, 2 * LANES), F32),
        ],
        compiler_params=pltpu.CompilerParams(
            dimension_semantics=("parallel", "parallel", "arbitrary"), vmem_limit_bytes=VMEM_LIMIT),
        name="fox",
    )(qv3, qv3, kt, ck.reshape(b, FOX_HEADS // 2, 2, tp))


SUBLANES = 8
SUBLANE_BITS = SUBLANES.bit_length() - 1


LRU_SEQS = 2


def _lru_kernel(xy_ref, perm_ref, unperm_ref, cw_ref, cb_ref, wg_ref, bg_ref, lam_ref, o_ref,
                xe_s, a_s, h_s, hist_s, carry, *, seqs, tt, n_pad):
    @pl.when(pl.program_id(1) == 0)
    def _():
        hist_s[...] = jnp.zeros(hist_s.shape, F32)
        carry[...] = jnp.zeros(carry.shape, F32)

    for q in range(seqs):
        _lru_one(xy_ref.at[q], perm_ref, unperm_ref, cw_ref, cb_ref, wg_ref, bg_ref, lam_ref, o_ref.at[q],
                 xe_s.at[q], a_s.at[q], h_s.at[q], hist_s.at[q], carry.at[q], tt=tt, n_pad=n_pad)


def _lru_one(xy_ref, perm_ref, unperm_ref, cw_ref, cb_ref, wg_ref, bg_ref, lam_ref, o_ref,
             xe_s, a_s, h_s, hist_s, carry, *, tt, n_pad):
    it = pl.program_id(1)
    seg = tt // SUBLANES
    taps = CONV_WIDTH - 1
    w = LRU_WIDTH
    sub = lax.broadcasted_iota(jnp.int32, (SUBLANES, w), 0)
    xyp = _dot(perm_ref[...], xy_ref[...])
    xp, gate = xyp[:, 0:w], xyp[:, w:]

    for v in range(taps):
        src = slice((seg - taps + v) * SUBLANES, (seg - taps + v + 1) * SUBLANES)
        xe_s[v * SUBLANES:(v + 1) * SUBLANES, :] = jnp.where(
            sub == 0, pltpu.roll(hist_s[v], 1, 0), pltpu.roll(xp[src], 1, 0))
        hist_s[v] = xp[src]
    xe_s[taps * SUBLANES:, :] = xp
    conv = xe_s[0:tt, :] * cw_ref[0:1, :]
    for i in range(1, CONV_WIDTH):
        conv = conv + xe_s[i * SUBLANES:i * SUBLANES + tt, :] * cw_ref[i:i + 1, :]
    conv = conv + cb_ref[...]

    t = jnp.tanh(_dot(conv.astype(BF16), wg_ref[...]) + bg_ref[...])
    k = (0.5 * LRU_C) * _log_sigmoid(lam_ref[...])
    log_a = k * t[:, 0:w] + k
    p_idx = lax.broadcasted_iota(jnp.int32, (tt, 1), 0)
    time = it * tt + (p_idx & (SUBLANES - 1)) * seg + (p_idx >> SUBLANE_BITS)
    a = jnp.exp(log_a)
    quarter_one_minus_a2 = (-0.25 * jnp.tanh(log_a)) * (a * a + 1.0)
    b = jnp.where(time >= n_pad, jnp.sqrt(quarter_one_minus_a2) * (t[:, w:] * conv + conv), 0.0)

    h_loc = jnp.zeros((SUBLANES, w), F32)
    prod = jnp.ones((SUBLANES, w), F32)
    for v in range(seg):
        rows = slice(v * SUBLANES, (v + 1) * SUBLANES)
        h_loc = a[rows] * h_loc + b[rows]
        prod = a[rows] * prod
        h_s[rows, :] = h_loc
        a_s[rows, :] = prod
    aa, bb = prod, h_loc
    for s in (1, 2, 4):
        keep = sub >= s
        bb = jnp.where(keep, aa * pltpu.roll(bb, s, 0) + bb, bb)
        aa = jnp.where(keep, aa * pltpu.roll(aa, s, 0), aa)
    h_end = aa * carry[...] + bb
    h_in = jnp.where(sub == 0, carry[...], pltpu.roll(h_end, 1, 0))
    carry[...] = h_end[SUBLANES - 1:SUBLANES, :]

    h_in_all = jnp.concatenate([h_in] * seg, axis=0)
    out = ((h_s[...] + a_s[...] * h_in_all) * gate).astype(BF16)
    o_ref[...] = _dot(unperm_ref[...], out).astype(BF16)


def _lru(xy3, cw, cb, wg, bg, lam, n_pad):
    b, tp, _ = xy3.shape
    tt = _row_tile(tp)
    w = LRU_WIDTH
    seg = tt // SUBLANES
    p_idx = jnp.arange(tt)
    perm = (jnp.arange(tt)[None, :] == ((p_idx % SUBLANES) * seg + p_idx // SUBLANES)[:, None]).astype(BF16)
    seqs = LRU_SEQS if b % LRU_SEQS == 0 else 1
    return pl.pallas_call(
        functools.partial(_lru_kernel, seqs=seqs, tt=tt, n_pad=n_pad),
        grid=(b // seqs, tp // tt),
        in_specs=[
            pl.BlockSpec((seqs, tt, C_COLS), lambda i, j: (i, j, 0)),
            _resident((tt, tt)), _resident((tt, tt)),
            _resident((CONV_WIDTH, w)), _resident((1, w)), _resident((w, 2 * w)), _resident((1, 2 * w)),
            _resident((1, w)),
        ],
        out_specs=pl.BlockSpec((seqs, tt, w), lambda i, j: (i, j, 0)),
        out_shape=jax.ShapeDtypeStruct((b, tp, w), BF16),
        scratch_shapes=[
            pltpu.VMEM((seqs, tt + (CONV_WIDTH - 1) * SUBLANES, w), F32), pltpu.VMEM((seqs, tt, w), F32),
            pltpu.VMEM((seqs, tt, w), F32), pltpu.VMEM((seqs, CONV_WIDTH - 1, SUBLANES, w), F32),
            pltpu.VMEM((seqs, 1, w), F32),
        ],
        compiler_params=pltpu.CompilerParams(dimension_semantics=("parallel", "arbitrary"), vmem_limit_bytes=VMEM_LIMIT),
        name="lru",
    )(xy3, perm, perm.T, cw, cb, wg, bg, lam)


FF_CHUNK = 256


def _mix_ffn_rows(h, oa, of, oc, gt_ref, wb_ref, wo_ref, gn_ref, w1_ref, w2_ref, act_s):
    d = D_MODEL
    merged = gt_ref[:, 0:d].astype(F32) * _dot(oa, wb_ref[0])
    merged = merged + gt_ref[:, d:2 * d].astype(F32) * _dot(of, wb_ref[1])
    merged = merged + gt_ref[:, 2 * d:3 * d].astype(F32) * _dot(oc, wb_ref[2])
    hm = h + _dot(merged.astype(BF16), wo_ref[...])
    u = _rms_norm(hm, gn_ref[...]).astype(BF16)
    for c in range(0, D_FF, FF_CHUNK):
        gate = _dot(u, w1_ref[:, c:c + FF_CHUNK])
        up = _dot(u, w1_ref[:, D_FF + c:D_FF + c + FF_CHUNK])
        act_s[:, c:c + FF_CHUNK] = (gate * _sigmoid(gate) * up).astype(BF16)
    return hm + _dot(act_s[...], w2_ref[...])


def _mix_ffn_kernel(h_ref, oa_ref, of_ref, oc_ref, gt_ref, wb_ref, wo_ref, gn_ref, w1_ref, w2_ref, out_ref, act_s,
                    *, tm, n_pad):
    j = pl.program_id(1)
    y = _mix_ffn_rows(h_ref[0], oa_ref[0], of_ref[0], oc_ref[0], gt_ref.at[0], wb_ref, wo_ref, gn_ref, w1_ref, w2_ref,
                      act_s)
    row = j * tm + lax.broadcasted_iota(jnp.int32, (tm, 1), 0)
    out_ref[0] = jnp.where(row >= n_pad, y, 0.0)


def _mix_ffn_last_kernel(h_ref, oa_ref, of_ref, oc_ref, gt_ref, wb_ref, wo_ref, gn_ref, w1_ref, w2_ref, gf_ref,
                         out_ref, act_s):
    y = _mix_ffn_rows(h_ref[...], oa_ref[...], of_ref[...], oc_ref[...], gt_ref, wb_ref, wo_ref, gn_ref, w1_ref,
                      w2_ref, act_s)
    out_ref[0] = _rms_norm(y, gf_ref[...])


def _mix_ffn_last(h3, oa, of, oc, gt3, wb, wo, gn, w1, w2, gf, skip_rows, seq):
    b, tp, d = h3.shape
    tm = 512 if seq % 512 == 0 else BLOCK
    w = LRU_WIDTH

    def rows(cols):
        return pl.BlockSpec((pl.Element(tm), pl.Element(cols)),
                            lambda i, j: (pl.multiple_of(i * tp + skip_rows + j * tm, BLOCK), 0))

    flat = lambda a: a.reshape(b * tp, a.shape[-1])
    return pl.pallas_call(
        _mix_ffn_last_kernel,
        grid=(b, seq // tm),
        in_specs=[
            rows(d), rows(w), rows(w), rows(w), rows(G_COLS),
            _resident((N_BRANCH, w, d)), _resident((d, d)), _resident((1, d)),
            _resident((d, 2 * D_FF)), _resident((D_FF, d)), _resident((1, d)),
        ],
        out_specs=pl.BlockSpec((1, tm, d), lambda i, j: (i, j, 0)),
        out_shape=jax.ShapeDtypeStruct((b, seq, d), F32),
        scratch_shapes=[pltpu.VMEM((tm, D_FF), BF16)],
        compiler_params=pltpu.CompilerParams(dimension_semantics=("parallel", "parallel"), vmem_limit_bytes=VMEM_LIMIT),
        name="mix_ffn_last",
    )(flat(h3), flat(oa), flat(of), flat(oc), flat(gt3), wb, wo, gn, w1, w2, gf)


def _mix_ffn(h3, oa, of, oc, gt3, wb, wo, gn, w1, w2, n_pad):
    b, tp, d = h3.shape
    tm = _row_tile(tp)
    w = LRU_WIDTH
    tile = lambda cols: pl.BlockSpec((1, tm, cols), lambda i, j: (i, j, 0))
    return pl.pallas_call(
        functools.partial(_mix_ffn_kernel, tm=tm, n_pad=n_pad),
        grid=(b, tp // tm),
        in_specs=[
            tile(d), tile(w), tile(w), tile(w), tile(G_COLS),
            _resident((N_BRANCH, w, d)), _resident((d, d)), _resident((1, d)),
            _resident((d, 2 * D_FF)), _resident((D_FF, d)),
        ],
        out_specs=tile(d),
        out_shape=jax.ShapeDtypeStruct((b, tp, d), F32),
        scratch_shapes=[pltpu.VMEM((tm, D_FF), BF16)],
        compiler_params=pltpu.CompilerParams(dimension_semantics=("parallel", "parallel"), vmem_limit_bytes=VMEM_LIMIT),
        name="mix_ffn",
    )(h3, oa, of, oc, gt3, wb, wo, gn, w1, w2)


def _t5_bucket(dist):
    max_exact = REL_BUCKETS // 2
    d = jnp.maximum(dist, 0)
    scaled = jnp.log(jnp.maximum(d, 1).astype(F32) / max_exact) / math.log(REL_MAX_DIST / max_exact)
    large = jnp.minimum(max_exact + (scaled * (REL_BUCKETS - max_exact)).astype(jnp.int32), REL_BUCKETS - 1)
    return jnp.where(d < max_exact, d, large)


def _swa_bias_kernel(bucket_ref, table_ref, o_ref):
    bucket = bucket_ref[...]
    for h in range(SWA_Q_HEADS):
        acc = jnp.full(bucket.shape, NEG_INF, F32)
        for c in range(REL_BUCKETS):
            acc = jnp.where(bucket == c, table_ref[c, h] * LOG2E, acc)
        o_ref[h * BLOCK:(h + 1) * BLOCK, :] = acc


def _swa_bias(rel_table):
    q_idx = jnp.arange(BLOCK)[:, None]
    k_idx = jnp.arange(2 * BLOCK)[None, :]
    dist = q_idx + BLOCK - k_idx
    bucket = jnp.where((dist >= 0) & (dist < SWA_WINDOW), _t5_bucket(dist), -1).astype(jnp.int32)
    return pl.pallas_call(
        _swa_bias_kernel,
        in_specs=[pl.BlockSpec(memory_space=pltpu.VMEM), pl.BlockSpec(memory_space=pltpu.SMEM)],
        out_specs=pl.BlockSpec(memory_space=pltpu.VMEM),
        out_shape=jax.ShapeDtypeStruct((SWA_Q_HEADS * BLOCK, 2 * BLOCK), F32),
        name="swa_bias",
    )(bucket, rel_table.astype(F32))


def _to_slab_order(a, axis):
    half = SWA_Q_HEADS // 2
    shape = a.shape
    a = a.reshape(shape[:axis] + (2, half, HEAD_DIM) + shape[axis + 1:])
    return jnp.swapaxes(a, axis, axis + 1).reshape(shape)


def _pack_w_in(w_in):
    sizes = (512, 128, 128, 512, 512, 512, 8, 512, 512, 3072)
    offs = [0]
    for s in sizes:
        offs.append(offs[-1] + s)
    qa, ka, va, qf, kf, vf, fl, xc, yc, gates = [w_in[:, :, offs[i]:offs[i + 1]] for i in range(len(sizes))]
    scale = HEAD_DIM ** -0.5 * LOG2E
    qa = _to_slab_order(qa, axis=2) * scale
    fl = jnp.pad(fl, ((0, 0), (0, 0), (0, LANES - FOX_HEADS)))
    packed = jnp.concatenate([qa, va, fl, qf * scale, vf, xc, yc, gates], axis=-1)
    keys_t = jnp.swapaxes(jnp.concatenate([kf, ka], axis=-1), 1, 2)
    return packed.astype(BF16), keys_t.astype(BF16)


def _block_diag(w):
    depth, nb, n, _ = w.shape
    eye = jnp.eye(nb, dtype=w.dtype)
    return jnp.einsum('lhij,hk->lhikj', w, eye).reshape(depth, nb * n, nb * n)


def kernel(x, meta_tokens, rel_bias_table, norm_mix, w_in, swa_sinks, fox_forget_bias, conv_w, conv_b,
           lru_w_r, lru_b_r, lru_w_i, lru_b_i, lru_lambda, w_branch, w_out, norm_ffn, w_ffn_in, w_ffn_out,
           norm_final):
    b, seq, d = x.shape
    depth = w_in.shape[0]
    t = N_META + seq
    n_pad = (-t) % BLOCK
    tp = t + n_pad
    assert d == D_MODEL and (n_pad + N_META) % BLOCK == 0

    meta = jnp.broadcast_to(meta_tokens.astype(x.dtype)[None], (b, N_META, d))
    h = jnp.concatenate([jnp.zeros((b, n_pad, d), x.dtype), meta, x], axis=1)

    w_in_p, wkt = _pack_w_in(w_in)
    bias_a = _swa_bias(rel_bias_table)
    sinks = jnp.broadcast_to((swa_sinks.astype(F32) * LOG2E)[:, :, None, None], (depth, SWA_Q_HEADS, BLOCK, LANES))
    sinks = sinks.reshape(depth, SWA_Q_HEADS * BLOCK, LANES)
    fb = jnp.pad(fox_forget_bias.astype(F32), ((0, 0), (0, LANES - FOX_HEADS)))[:, None, :]
    wg = (0.5 * jnp.concatenate([_block_diag(lru_w_r), _block_diag(lru_w_i)], axis=-1)).astype(BF16)
    bg = 0.5 * jnp.concatenate([lru_b_r, lru_b_i], axis=-1).astype(F32)[:, None, :]
    wb_a = _to_slab_order(w_branch[:, 0], axis=1)
    wb = jnp.stack([wb_a, w_branch[:, 1], w_branch[:, 2]], axis=1).astype(BF16)
    wo = w_out.astype(BF16)
    w1 = w_ffn_in.astype(BF16)
    w2 = w_ffn_out.astype(BF16)

    for l in range(depth):
        qva, qvf, kt, fl, xy, gt = _in_proj(h.reshape(b * tp, d), norm_mix[l][None, :], w_in_p[l], wkt[l])
        ck = _fox_prefix(fl.reshape(b, tp, LANES), fb[l], n_pad)
        o_a = _swa(qva.reshape(b, tp, A_COLS), kt, bias_a, sinks[l], n_pad)
        o_f = _fox(qvf.reshape(b, tp, F_COLS), kt, ck, n_pad)
        o_c = _lru(xy.reshape(b, tp, C_COLS), conv_w[l], conv_b[l][None, :], wg[l], bg[l], lru_lambda[l][None, :],
                   n_pad)
        mix_args = (h, o_a, o_f, o_c, gt.reshape(b, tp, G_COLS), wb[l], wo[l], norm_ffn[l][None, :], w1[l], w2[l])
        if l + 1 < depth:
            h = _mix_ffn(*mix_args, n_pad)
    return _mix_ffn_last(*mix_args, norm_final[None, :], n_pad + N_META, seq)
```
